```python
import math
import jax
import jax.numpy as jnp
from jax import lax
import numpy as np

D_MODEL = 2048
BATCH = 4
SEQ = 2048
DEPTH = 4
DEC_BATCH = 128
DEC_SEQ = 8
PAST_LEN = 16384
PAGE_SIZE = 128

SSM_GROUP = 16
SSM_WIDTH = D_MODEL // 2
SSM_GROUPS = SSM_WIDTH // SSM_GROUP
SSM_STATE = 64
DT_MIN = 1e-3
DT_MAX = 1e-1
POOL_WINDOWS = (2, 4, 8, 16)
POOL_WIDTH = D_MODEL // 2
POOL_GROUP = POOL_WIDTH // len(POOL_WINDOWS)
POOL_BUF = max(POOL_WINDOWS) - 1
D_FF = 5632
PLE_DIM = 256
IN_WIDTH = SSM_WIDTH + POOL_WIDTH + 2 * D_MODEL
RMS_EPS = 1e-6

kernel_name = 'hybrid_s5_pool_gated_decoder_step'


def _rmsnorm(x, g):
    x32 = x.astype(jnp.float32)
    y = x32 * lax.rsqrt(jnp.mean(x32 * x32, axis=-1, keepdims=True) + RMS_EPS)
    return y.astype(x.dtype) * g


def _swiglu(x, w_gate, w_up, w_down):
    return (jax.nn.silu(x @ w_gate) * (x @ w_up)) @ w_down


def _cmul(ar, ai, br, bi):
    return ar * br - ai * bi, ar * bi + ai * br


def _scan_combine(e1, e2):
    a1r, a1i, b1r, b1i = e1
    a2r, a2i, b2r, b2i = e2
    ar, ai = _cmul(a2r, a2i, a1r, a1i)
    br, bi = _cmul(a2r, a2i, b1r, b1i)
    return ar, ai, br + b2r, bi + b2i


def _s5(u, s0_re, s0_im, a_re, a_im, log_dt, b_re, b_im, c_re, c_im, d):
    f32 = jnp.float32
    n, l, _ = u.shape
    uf = u.astype(f32).reshape(n, l, SSM_GROUPS, SSM_GROUP)
    a_re = a_re.astype(f32)
    a_im = a_im.astype(f32)
    dt = jnp.exp(log_dt.astype(f32))[:, None]
    mag = jnp.exp(a_re * dt)
    ang = a_im * dt
    lam_re = mag * jnp.cos(ang)
    lam_im = mag * jnp.sin(ang)
    den = a_re * a_re + a_im * a_im
    num_re = lam_re - 1.0
    k_re = (num_re * a_re + lam_im * a_im) / den
    k_im = (lam_im * a_re - num_re * a_im) / den
    bb_re, bb_im = _cmul(k_re[..., None], k_im[..., None], b_re.astype(f32), b_im.astype(f32))
    x_re = jnp.einsum('nlgh,gph->nlgp', uf, bb_re)
    x_im = jnp.einsum('nlgh,gph->nlgp', uf, bb_im)
    c0_re, c0_im = _cmul(lam_re, lam_im, s0_re.astype(f32), s0_im.astype(f32))
    x_re = x_re.at[:, 0].add(c0_re)
    x_im = x_im.at[:, 0].add(c0_im)
    lr = jnp.broadcast_to(lam_re, x_re.shape)
    li = jnp.broadcast_to(lam_im, x_re.shape)
    _, _, s_re, s_im = lax.associative_scan(_scan_combine, (lr, li, x_re, x_im), axis=1)
    y = (jnp.einsum('nlgp,ghp->nlgh', s_re, c_re.astype(f32))
         - jnp.einsum('nlgp,ghp->nlgh', s_im, c_im.astype(f32))
         + d.astype(f32) * uf)
    return (y.reshape(n, l, SSM_WIDTH).astype(u.dtype),
            s_re[:, -1].astype(u.dtype), s_im[:, -1].astype(u.dtype))


def _pool(u, prev, pos0, w_pool, pool_scale):
    f32 = jnp.float32
    n, l, _ = u.shape
    z = jnp.concatenate([prev.astype(u.dtype), u], axis=1).astype(f32)
    cs = jnp.pad(jnp.cumsum(z, axis=1), ((0, 0), (1, 0), (0, 0)))
    pos = pos0 + jnp.arange(l)
    cur = z[:, POOL_BUF:]
    means = []
    for gi, w in enumerate(POOL_WINDOWS):
        c0, c1 = gi * POOL_GROUP, (gi + 1) * POOL_GROUP
        win = (cs[:, POOL_BUF + 1:POOL_BUF + 1 + l, c0:c1]
               - cs[:, POOL_BUF + 1 - w:POOL_BUF + 1 - w + l, c0:c1])
        cnt = jnp.minimum(pos + 1, w).astype(f32)[None, :, None]
        means.append(win / cnt)
    mixed = (jnp.concatenate(means, axis=-1) - cur).reshape(n, l, len(POOL_WINDOWS), POOL_GROUP)
    mixed = jnp.einsum('nlgc,gcd->nlgd', mixed, w_pool.astype(f32)).reshape(n, l, POOL_WIDTH)
    out = mixed * pool_scale.astype(f32)
    return out.astype(u.dtype), z[:, -POOL_BUF:].astype(u.dtype)


def _layer(x, p_i, s_re, s_im, pool_prev, pos0, prm, i):
    h = x + 0.5 * _swiglu(_rmsnorm(x, prm['g_ffn1'][i]), prm['w_ffn1_gate'][i],
                          prm['w_ffn1_up'][i], prm['w_ffn1_down'][i])
    proj = _rmsnorm(h, prm['g_mix'][i]) @ prm['w_in'][i]
    o1 = SSM_WIDTH
    o2 = o1 + POOL_WIDTH
    o3 = o2 + D_MODEL
    u_a, u_b, gate_a, gate_b = proj[..., :o1], proj[..., o1:o2], proj[..., o2:o3], proj[..., o3:]
    y_a, s_re_new, s_im_new = _s5(u_a, s_re, s_im, prm['ssm_a_re'][i], prm['ssm_a_im'][i],
                                  prm['ssm_log_dt'][i], prm['ssm_b_re'][i], prm['ssm_b_im'][i],
                                  prm['ssm_c_re'][i], prm['ssm_c_im'][i], prm['ssm_d'][i])
    y_a = jax.nn.gelu(y_a)
    br_a = (y_a @ prm['w_glu_a'][i]) * jax.nn.sigmoid(y_a @ prm['w_glu_b'][i])
    y_b, pool_new = _pool(u_b, pool_prev, pos0, prm['w_pool'][i], prm['pool_scale'][i])
    br_b = y_b @ prm['w_pool_up'][i]
    merged = jax.nn.sigmoid(gate_a) * br_a + jax.nn.sigmoid(gate_b) * br_b
    h = h + merged @ prm['w_out'][i]
    h = h + 0.5 * _swiglu(_rmsnorm(h, prm['g_ffn2'][i]), prm['w_ffn2_gate'][i],
                          prm['w_ffn2_up'][i], prm['w_ffn2_down'][i])
    h = h + (p_i @ prm['w_ple'][i]) * jax.nn.sigmoid(_rmsnorm(h, prm['g_ple'][i]) @ prm['w_ple_gate'][i])
    return h, s_re_new, s_im_new, pool_new


def _trunk(x, p, s_re, s_im, pool, pos0, prm):
    new_re, new_im, new_pool = [], [], []
    for i in range(DEPTH):
        x, r, m, q = _layer(x, p[i], s_re[i], s_im[i], pool[i], pos0, prm, i)
        new_re.append(r)
        new_im.append(m)
        new_pool.append(q)
    return _rmsnorm(x, prm['g_final']), jnp.stack(new_re), jnp.stack(new_im), jnp.stack(new_pool)


def setup_inputs(seed: int = 0) -> dict:
    key = jax.random.key(seed)
    ks = iter(jax.random.split(key, 48))
    f32 = jnp.float32

    def nrm(shape, scale):
        return jax.random.normal(next(ks), shape, f32) * scale

    def gain(shape):
        return 1.0 + nrm(shape, 0.05)

    G, P, H = SSM_GROUPS, SSM_STATE, SSM_GROUP
    inp = {}
    inp['x_prompt'] = nrm((BATCH, SEQ, D_MODEL), 1.0)
    inp['x_sample'] = nrm((DEC_BATCH, DEC_SEQ, D_MODEL), 1.0)
    inp['state_ssm_re'] = nrm((DEPTH, DEC_BATCH, G, P), 0.1)
    inp['state_ssm_im'] = nrm((DEPTH, DEC_BATCH, G, P), 0.1)
    inp['state_pool'] = nrm((DEPTH, DEC_BATCH, POOL_BUF, POOL_WIDTH), 1.0)
    inp['p_prompt'] = nrm((DEPTH, BATCH, SEQ, PLE_DIM), 1.0)
    inp['p_sample'] = nrm((DEPTH, DEC_BATCH, DEC_SEQ, PLE_DIM), 1.0)
    inp['g_ffn1'] = gain((DEPTH, D_MODEL))
    inp['w_ffn1_gate'] = nrm((DEPTH, D_MODEL, D_FF), D_MODEL ** -0.5)
    inp['w_ffn1_up'] = nrm((DEPTH, D_MODEL, D_FF), D_MODEL ** -0.5)
    inp['w_ffn1_down'] = nrm((DEPTH, D_FF, D_MODEL), D_FF ** -0.5)
    inp['g_mix'] = gain((DEPTH, D_MODEL))
    inp['w_in'] = nrm((DEPTH, D_MODEL, IN_WIDTH), D_MODEL ** -0.5)
    inp['ssm_a_re'] = -0.5 + nrm((DEPTH, G, P), 0.01)
    inp['ssm_a_im'] = jnp.broadcast_to(math.pi * jnp.arange(P, dtype=f32), (DEPTH, G, P)) + nrm((DEPTH, G, P), 0.01)
    inp['ssm_log_dt'] = jax.random.uniform(next(ks), (DEPTH, G), f32, math.log(DT_MIN), math.log(DT_MAX))
    inp['ssm_b_re'] = nrm((DEPTH, G, P, H), (2 * H) ** -0.5)
    inp['ssm_b_im'] = nrm((DEPTH, G, P, H), (2 * H) ** -0.5)
    inp['ssm_c_re'] = nrm((DEPTH, G, H, P), P ** -0.5)
    inp['ssm_c_im'] = nrm((DEPTH, G, H, P), P ** -0.5)
    inp['ssm_d'] = nrm((DEPTH, G, H), 1.0)
    inp['w_glu_a'] = nrm((DEPTH, SSM_WIDTH, D_MODEL), SSM_WIDTH ** -0.5)
    inp['w_glu_b'] = nrm((DEPTH, SSM_WIDTH, D_MODEL), SSM_WIDTH ** -0.5)
    inp['w_pool'] = nrm((DEPTH, len(POOL_WINDOWS), POOL_GROUP, POOL_GROUP), POOL_GROUP ** -0.5)
    inp['pool_scale'] = gain((DEPTH, POOL_WIDTH))
    inp['w_pool_up'] = nrm((DEPTH, POOL_WIDTH, D_MODEL), POOL_WIDTH ** -0.5)
    inp['w_out'] = nrm((DEPTH, D_MODEL, D_MODEL), D_MODEL ** -0.5)
    inp['g_ffn2'] = gain((DEPTH, D_MODEL))
    inp['w_ffn2_gate'] = nrm((DEPTH, D_MODEL, D_FF), D_MODEL ** -0.5)
    inp['w_ffn2_up'] = nrm((DEPTH, D_MODEL, D_FF), D_MODEL ** -0.5)
    inp['w_ffn2_down'] = nrm((DEPTH, D_FF, D_MODEL), D_FF ** -0.5)
    inp['g_ple'] = gain((DEPTH, D_MODEL))
    inp['w_ple'] = nrm((DEPTH, PLE_DIM, D_MODEL), PLE_DIM ** -0.5)
    inp['w_ple_gate'] = nrm((DEPTH, D_MODEL, D_MODEL), D_MODEL ** -0.5)
    inp['g_final'] = gain((D_MODEL,))
    return inp


def reference(x_prompt, x_sample, state_ssm_re, state_ssm_im, state_pool, p_prompt, p_sample,
              g_ffn1, w_ffn1_gate, w_ffn1_up, w_ffn1_down, g_mix, w_in,
              ssm_a_re, ssm_a_im, ssm_log_dt, ssm_b_re, ssm_b_im, ssm_c_re, ssm_c_im, ssm_d,
              w_glu_a, w_glu_b, w_pool, pool_scale, w_pool_up, w_out,
              g_ffn2, w_ffn2_gate, w_ffn2_up, w_ffn2_down, g_ple, w_ple, w_ple_gate, g_final):
    prm = dict(g_ffn1=g_ffn1, w_ffn1_gate=w_ffn1_gate, w_ffn1_up=w_ffn1_up, w_ffn1_down=w_ffn1_down,
               g_mix=g_mix, w_in=w_in, ssm_a_re=ssm_a_re, ssm_a_im=ssm_a_im, ssm_log_dt=ssm_log_dt,
               ssm_b_re=ssm_b_re, ssm_b_im=ssm_b_im, ssm_c_re=ssm_c_re, ssm_c_im=ssm_c_im, ssm_d=ssm_d,
               w_glu_a=w_glu_a, w_glu_b=w_glu_b, w_pool=w_pool, pool_scale=pool_scale,
               w_pool_up=w_pool_up, w_out=w_out, g_ffn2=g_ffn2, w_ffn2_gate=w_ffn2_gate,
               w_ffn2_up=w_ffn2_up, w_ffn2_down=w_ffn2_down, g_ple=g_ple, w_ple=w_ple,
               w_ple_gate=w_ple_gate, g_final=g_final)
    zero_ssm = jnp.zeros((DEPTH, BATCH, SSM_GROUPS, SSM_STATE), x_prompt.dtype)
    zero_pool = jnp.zeros((DEPTH, BATCH, POOL_BUF, POOL_WIDTH), x_prompt.dtype)
    y_prompt, ssm_re_prompt, ssm_im_prompt, pool_prompt = _trunk(
        x_prompt, p_prompt, zero_ssm, zero_ssm, zero_pool, 0, prm)
    y_sample, ssm_re_sample, ssm_im_sample, pool_sample = _trunk(
        x_sample, p_sample, state_ssm_re, state_ssm_im, state_pool, PAST_LEN, prm)
    return (y_prompt, y_sample, ssm_re_prompt, ssm_im_prompt, pool_prompt,
            ssm_re_sample, ssm_im_sample, pool_sample)
```

```python
import functools
import math

import numpy as np
import jax
import jax.numpy as jnp
from jax import lax
from jax.experimental import pallas as pl
from jax.experimental.pallas import tpu as pltpu

F32 = jnp.float32
BF16 = jnp.bfloat16
RMS_EPS = 1e-6
POOL_WINDOWS = (2, 4, 8, 16)
SSM_GROUP = 16
SSM_CHUNK = 16
GELU_C = math.sqrt(2.0 / math.pi)
VMEM_LIMIT = 56 * 1024 * 1024


def _cparams(sem):
    return pltpu.CompilerParams(dimension_semantics=sem, vmem_limit_bytes=VMEM_LIMIT)


def _rms_bf16(x, g):
    inv = lax.rsqrt(jnp.mean(x * x, axis=-1, keepdims=True) + RMS_EPS)
    return (x * inv * g).astype(BF16)


def _dot(a, b):
    return jnp.dot(a, b, preferred_element_type=F32)


def _ffn_body(x_ref, g_ref, wg_ref, wu_ref, wd_ref, o_ref, xn_ref):
    j = pl.program_id(1)

    @pl.when(j == 0)
    def _():
        xn_ref[...] = _rms_bf16(x_ref[...], g_ref[...])

    xn = xn_ref[...]
    a = _dot(xn, wg_ref[...])
    b = _dot(xn, wu_ref[...])
    mid = (a * jax.nn.sigmoid(a) * b).astype(BF16)
    part = _dot(mid, wd_ref[...])

    @pl.when(j == 0)
    def _():
        o_ref[...] = part

    @pl.when(j > 0)
    def _():
        o_ref[...] += part

    @pl.when(j == pl.num_programs(1) - 1)
    def _():
        o_ref[...] = x_ref[...] + 0.5 * o_ref[...]


def _ffn(x, g, wg, wu, wd, layer, tm, tf):
    m, d = x.shape
    f = wg.shape[-1]
    return pl.pallas_call(
        _ffn_body,
        grid=(m // tm, f // tf),
        in_specs=[
            pl.BlockSpec((tm, d), lambda i, j: (i, 0), pipeline_mode=pl.Buffered(1)),
            pl.BlockSpec((None, 1, d), lambda i, j: (layer, 0, 0)),
            pl.BlockSpec((None, d, tf), lambda i, j: (layer, 0, j)),
            pl.BlockSpec((None, d, tf), lambda i, j: (layer, 0, j)),
            pl.BlockSpec((None, tf, d), lambda i, j: (layer, j, 0)),
        ],
        out_specs=pl.BlockSpec((tm, d), lambda i, j: (i, 0), pipeline_mode=pl.Buffered(1)),
        out_shape=jax.ShapeDtypeStruct((m, d), F32),
        scratch_shapes=[pltpu.VMEM((tm, d), BF16)],
        compiler_params=_cparams(("parallel", "arbitrary")),
        name="ffn",
    )(x, g, wg, wu, wd)


def _inproj_body(x_ref, g_ref, w_ref, o_ref, xn_ref, *, gate_from):
    j = pl.program_id(1)

    @pl.when(j == 0)
    def _():
        xn_ref[...] = _rms_bf16(x_ref[...], g_ref[...])

    r = _dot(xn_ref[...], w_ref[...])

    @pl.when(j < gate_from)
    def _():
        o_ref[...] = r

    @pl.when(j >= gate_from)
    def _():
        o_ref[...] = jax.nn.sigmoid(r)


def _inproj(x, g, w, layer, tm, tn, n_plain):
    m, d = x.shape
    n = w.shape[-1]
    return pl.pallas_call(
        functools.partial(_inproj_body, gate_from=n_plain // tn),
        grid=(m // tm, n // tn),
        in_specs=[
            pl.BlockSpec((tm, d), lambda i, j: (i, 0)),
            pl.BlockSpec((None, 1, d), lambda i, j: (layer, 0, 0)),
            pl.BlockSpec((None, d, tn), lambda i, j: (layer, 0, j)),
        ],
        out_specs=pl.BlockSpec((tm, tn), lambda i, j: (i, j)),
        out_shape=jax.ShapeDtypeStruct((m, n), F32),
        scratch_shapes=[pltpu.VMEM((tm, d), BF16)],
        compiler_params=_cparams(("parallel", "arbitrary")),
        name="inproj",
    )(x, g, w)


def _ssm_weights_body(ar_r, ai_r, ar_c, ai_c, ldt, bre, bim, cre, cim, e_col, e_row, e_scan,
                      tt_o, wor_o, woi_o, wsr_o, wsi_o, scr_o, sci_o, *, chunk):
    dt = jnp.exp(ldt[...])
    arr, air = ar_r[...] * dt, ai_r[...] * dt
    arc, aic = ar_c[...] * dt, ai_c[...] * dt
    tau_c = e_col[...]
    tau_r = e_row[...]

    def lam_pow(re, im, e):
        mag = jnp.exp(re * e)
        ang = im * e
        return mag * jnp.cos(ang), mag * jnp.sin(ang)

    lr, li = lam_pow(arc, aic, 1.0)
    a_re, a_im = ar_c[...], ai_c[...]
    den = a_re * a_re + a_im * a_im
    num_re = lr - 1.0
    k_re = (num_re * a_re + li * a_im) / den
    k_im = (li * a_re - num_re * a_im) / den
    b_re, b_im = bre[...], bim[...]
    kb_re = k_re * b_re - k_im * b_im
    kb_im = k_re * b_im + k_im * b_re
    c_re, c_im = cre[...], cim[...]

    pr, pi = lam_pow(arr, air, tau_c)
    l_re = c_re * pr - c_im * pi
    l_im = c_re * pi + c_im * pr
    qr, qi = lam_pow(arc, aic, -tau_r)
    r_re = qr * kb_re - qi * kb_im
    r_im = qr * kb_im + qi * kb_re
    hi = lax.Precision.HIGHEST
    kmat = (jnp.dot(l_re, r_re, precision=hi, preferred_element_type=F32)
            - jnp.dot(l_im, r_im, precision=hi, preferred_element_type=F32))
    tt_o[...] = jnp.where(tau_c >= tau_r, kmat, 0.0).astype(BF16)

    p1r, p1i = lam_pow(arr, air, tau_c + 1.0)
    wor_o[...] = (c_re * p1r - c_im * p1i).astype(BF16)
    woi_o[...] = (-(c_re * p1i + c_im * p1r)).astype(BF16)

    q2r, q2i = lam_pow(arc, aic, (chunk - 1.0) - tau_r)
    wsr_o[...] = (q2r * kb_re - q2i * kb_im).astype(BF16)
    wsi_o[...] = (q2r * kb_im + q2i * kb_re).astype(BF16)

    sr, si = lam_pow(arc, aic, e_scan[...])
    scr_o[...] = sr
    sci_o[...] = si


def _ssm_weights(a_re, a_im, log_dt, b_re, b_im, c_re, c_im, chunk):
    dg, p = a_re.shape
    h = b_re.shape[-1]
    th = chunk * h
    tau = np.repeat(np.arange(chunk, dtype=np.float32), h)
    e_col = jnp.asarray(tau.reshape(th, 1))
    e_row = jnp.asarray(tau.reshape(1, th))
    scan = np.zeros((1, 128), np.float32)
    scan[0, :7] = chunk * 2.0 ** np.arange(7)
    scan[0, 7] = chunk / 2
    e_scan = jnp.asarray(scan)
    row = lambda x: x.reshape(dg, 1, p)
    col = lambda x: x.reshape(dg, p, 1)
    b_t = lambda x: jnp.tile(x, (1, 1, chunk))
    c_t = lambda x: jnp.tile(x, (1, chunk, 1))
    per_g = lambda *s: pl.BlockSpec((None,) + s, lambda g: (g,) + (0,) * len(s))
    const = lambda *s: pl.BlockSpec(s, lambda g: (0,) * len(s))
    return pl.pallas_call(
        functools.partial(_ssm_weights_body, chunk=float(chunk)),
        grid=(dg,),
        in_specs=[per_g(1, p), per_g(1, p), per_g(p, 1), per_g(p, 1), per_g(1, 1),
                  per_g(p, th), per_g(p, th), per_g(th, p), per_g(th, p),
                  const(th, 1), const(1, th), const(1, 128)],
        out_specs=[per_g(th, th), per_g(th, p), per_g(th, p), per_g(p, th), per_g(p, th),
                   per_g(p, 128), per_g(p, 128)],
        out_shape=[jax.ShapeDtypeStruct((dg, th, th), BF16),
                   jax.ShapeDtypeStruct((dg, th, p), BF16),
                   jax.ShapeDtypeStruct((dg, th, p), BF16),
                   jax.ShapeDtypeStruct((dg, p, th), BF16),
                   jax.ShapeDtypeStruct((dg, p, th), BF16),
                   jax.ShapeDtypeStruct((dg, p, 128), F32),
                   jax.ShapeDtypeStruct((dg, p, 128), F32)],
        compiler_params=_cparams(("parallel",)),
        name="ssm_weights",
    )(row(a_re), row(a_im), col(a_re), col(a_im), log_dt.reshape(dg, 1, 1),
      b_t(b_re), b_t(b_im), c_t(c_re), c_t(c_im), e_col, e_row, e_scan)


def _ssm_body(*refs, steps, chunks, gpb, has_init):
    if has_init:
        (ut_ref, d_ref, tt_ref, wor_ref, woi_ref, wsr_ref, wsi_ref, scr_ref, sci_ref,
         s0r_ref, s0i_ref, yt_ref, sfr_ref, sfi_ref) = refs
    else:
        (ut_ref, d_ref, tt_ref, wor_ref, woi_ref, wsr_ref, wsi_ref, scr_ref, sci_ref,
         yt_ref, sfr_ref, sfi_ref) = refs
    h = SSM_GROUP
    th = steps * h
    w = ut_ref.shape[1] // steps
    p = scr_ref.shape[1]
    ws_off = wsr_ref.shape[2] - th
    for gl in range(gpb):
        rows = slice(gl * h, (gl + 1) * h)
        u = jnp.concatenate([ut_ref[rows, t * w:(t + 1) * w] for t in range(steps)], axis=0)
        ub = u.astype(BF16)
        y = _dot(tt_ref[gl, :th, :th], ub)
        xr = _dot(wsr_ref[gl, :, ws_off:], ub)
        xi = _dot(wsi_ref[gl, :, ws_off:], ub)
        if chunks > 1:
            c_idx = lax.broadcasted_iota(jnp.int32, (p, w), 1) & (chunks - 1)
            sr, si = xr, xi
            k = 0
            while (1 << k) < chunks:
                sh = 1 << k
                rr = pltpu.roll(sr, sh, axis=1)
                ri = pltpu.roll(si, sh, axis=1)
                mr = scr_ref[gl, :, k:k + 1]
                mi = sci_ref[gl, :, k:k + 1]
                keep = c_idx >= sh
                sr = sr + jnp.where(keep, mr * rr - mi * ri, 0.0)
                si = si + jnp.where(keep, mr * ri + mi * rr, 0.0)
                k += 1
            first = c_idx >= 1
            pr = jnp.where(first, pltpu.roll(sr, 1, axis=1), 0.0)
            pi = jnp.where(first, pltpu.roll(si, 1, axis=1), 0.0)
            for n in range(w // chunks):
                last = n * chunks + chunks - 1
                sfr_ref[gl, :, n:n + 1] = sr[:, last:last + 1]
                sfi_ref[gl, :, n:n + 1] = si[:, last:last + 1]
        else:
            pr, pi = s0r_ref[gl], s0i_ref[gl]
            lr = scr_ref[gl, :, 7:8]
            li = sci_ref[gl, :, 7:8]
            sfr_ref[gl] = lr * pr - li * pi + xr
            sfi_ref[gl] = lr * pi + li * pr + xi
        y = y + _dot(wor_ref[gl, :th, :], pr.astype(BF16)) + _dot(woi_ref[gl, :th, :], pi.astype(BF16))
        y = y + d_ref[gl, :th, :] * u
        y = 0.5 * y * (1.0 + jnp.tanh(GELU_C * (y + 0.044715 * (y * y * y))))
        yb = y.astype(BF16)
        for t in range(steps):
            yt_ref[rows, t * w:(t + 1) * w] = yb[t * h:(t + 1) * h, :]


def _ssm(ut, d_t, ops, layer, n_groups, steps, chunks, s0=None, gpb=8):
    tt, wor, woi, wsr, wsi, scr, sci = ops
    c, lanes = ut.shape
    h = SSM_GROUP
    w = lanes // steps
    nseq = w // chunks
    p = scr.shape[1]
    thf = tt.shape[1]
    base = layer * (n_groups // gpb)
    wblk = lambda *s: pl.BlockSpec((gpb,) + s, lambda g: (base + g,) + (0,) * len(s))
    in_specs = [pl.BlockSpec((gpb * h, lanes), lambda g: (g, 0)),
                wblk(thf, 1), wblk(thf, thf), wblk(thf, p), wblk(thf, p), wblk(p, thf), wblk(p, thf),
                wblk(p, 128), wblk(p, 128)]
    args = [ut, d_t, tt, wor, woi, wsr, wsi, scr, sci]
    if s0 is not None:
        in_specs += [pl.BlockSpec((gpb, p, w), lambda g: (g, 0, 0))] * 2
        args += list(s0)
    sf_spec = pl.BlockSpec((gpb, p, nseq), lambda g: (g, 0, 0))
    return pl.pallas_call(
        functools.partial(_ssm_body, steps=steps, chunks=chunks, gpb=gpb, has_init=s0 is not None),
        grid=(n_groups // gpb,),
        in_specs=in_specs,
        out_specs=[pl.BlockSpec((gpb * h, lanes), lambda g: (g, 0)), sf_spec, sf_spec],
        out_shape=[jax.ShapeDtypeStruct((c, lanes), BF16),
                   jax.ShapeDtypeStruct((n_groups, p, nseq), F32),
                   jax.ShapeDtypeStruct((n_groups, p, nseq), F32)],
        compiler_params=_cparams(("parallel",)),
        name="ssm_chunks" if s0 is None else "ssm_step",
    )(*args)


def _pool_seq_body(u_ref, w_ref, sc_ref, o_ref, z_ref):
    l, c = u_ref.shape
    pad = z_ref.shape[0] - l
    cg = c // len(POOL_WINDOWS)
    z_ref[:pad, :] = jnp.zeros((pad, c), F32)
    z_ref[pad:, :] = u_ref[...]
    pos1 = (lax.broadcasted_iota(jnp.int32, (l, 1), 0) + 1).astype(F32)
    for gi, win in enumerate(POOL_WINDOWS):
        cols = slice(gi * cg, (gi + 1) * cg)
        cur = z_ref[pad:, cols]
        tot = cur
        for k in range(1, win):
            tot = tot + z_ref[pad - k:pad - k + l, cols]
        mean = tot / jnp.minimum(pos1, float(win))
        mixed = _dot((mean - cur).astype(BF16), w_ref[gi])
        o_ref[:, cols] = (mixed * sc_ref[:, cols]).astype(BF16)


def _pool_seq(z, w_pool, scale, layer, nseq, seqlen, width, col_blk):
    return pl.pallas_call(
        _pool_seq_body,
        grid=(nseq,),
        in_specs=[pl.BlockSpec((seqlen, width), lambda n: (n, col_blk)),
                  pl.BlockSpec((None,) + w_pool.shape[1:], lambda n: (layer, 0, 0, 0)),
                  pl.BlockSpec((None, 1, width), lambda n: (layer, 0, 0))],
        out_specs=pl.BlockSpec((seqlen, width), lambda n: (n, 0)),
        out_shape=jax.ShapeDtypeStruct((nseq * seqlen, width), BF16),
        scratch_shapes=[pltpu.VMEM((seqlen + 16, width), F32)],
        compiler_params=_cparams(("parallel",)),
        name="pool_seq",
    )(z, w_pool, scale)


def _pool_step_body(u_ref, prev_ref, w_ref, sc_ref, o_ref):
    n, steps, c = u_ref.shape
    buf = prev_ref.shape[1]
    cg = c // len(POOL_WINDOWS)

    def row(j, cols):
        return prev_ref[:, j, cols] if j < buf else u_ref[:, j - buf, cols]

    for gi, win in enumerate(POOL_WINDOWS):
        cols = slice(gi * cg, (gi + 1) * cg)
        for t in range(steps):
            cur = row(buf + t, cols)
            tot = cur
            for k in range(1, win):
                tot = tot + row(buf + t - k, cols)
            mixed = _dot((tot / float(win) - cur).astype(BF16), w_ref[gi])
            o_ref[t, :, cols] = (mixed * sc_ref[:, cols]).astype(BF16)


def _pool_step(u, prev, w_pool, scale, layer):
    n, steps, c = u.shape
    return pl.pallas_call(
        _pool_step_body,
        grid=(1,),
        in_specs=[pl.BlockSpec(u.shape, lambda i: (0, 0, 0)),
                  pl.BlockSpec((None,) + prev.shape[1:], lambda i: (layer, 0, 0, 0)),
                  pl.BlockSpec((None,) + w_pool.shape[1:], lambda i: (layer, 0, 0, 0)),
                  pl.BlockSpec((None, 1, c), lambda i: (layer, 0, 0))],
        out_specs=pl.BlockSpec((steps, n, c), lambda i: (0, 0, 0)),
        out_shape=jax.ShapeDtypeStruct((steps, n, c), BF16),
        compiler_params=_cparams(("arbitrary",)),
        name="pool_step",
    )(u, prev, w_pool, scale)


def _mix_body(ga_ref, yb_ref, sa_ref, sb_ref, wa_ref, wb_ref, wp_ref, o_ref):
    ga = ga_ref[...]
    br_a = _dot(ga, wa_ref[...]) * jax.nn.sigmoid(_dot(ga, wb_ref[...]))
    br_b = _dot(yb_ref[...], wp_ref[...])
    o_ref[...] = (sa_ref[...] * br_a + sb_ref[...] * br_b).astype(BF16)


def _mix(ga, yb, z, wa, wb, wp, layer, tm, tn, gate_col):
    m, k = ga.shape
    n = wa.shape[-1]
    gblk = gate_col // tn
    wspec = pl.BlockSpec((None, k, tn), lambda i, j: (layer, 0, j))
    return pl.pallas_call(
        _mix_body,
        grid=(m // tm, n // tn),
        in_specs=[pl.BlockSpec((tm, k), lambda i, j: (i, 0)),
                  pl.BlockSpec((tm, k), lambda i, j: (i, 0)),
                  pl.BlockSpec((tm, tn), lambda i, j: (i, gblk + j)),
                  pl.BlockSpec((tm, tn), lambda i, j: (i, gblk + n // tn + j)),
                  wspec, wspec, wspec],
        out_specs=pl.BlockSpec((tm, tn), lambda i, j: (i, j)),
        out_shape=jax.ShapeDtypeStruct((m, n), BF16),
        compiler_params=_cparams(("parallel", "arbitrary")),
        name="mix",
    )(ga, yb, z, z, wa, wb, wp)


def _resmm_body(a_ref, w_ref, h_ref, o_ref):
    o_ref[...] = h_ref[...] + _dot(a_ref[...], w_ref[...])


def _resmm(a, w, hres, layer, tm, tn):
    m, k = a.shape
    n = w.shape[-1]
    return pl.pallas_call(
        _resmm_body,
        grid=(m // tm, n // tn),
        in_specs=[pl.BlockSpec((tm, k), lambda i, j: (i, 0)),
                  pl.BlockSpec((None, k, tn), lambda i, j: (layer, 0, j)),
                  pl.BlockSpec((tm, tn), lambda i, j: (i, j))],
        out_specs=pl.BlockSpec((tm, tn), lambda i, j: (i, j)),
        out_shape=jax.ShapeDtypeStruct((m, n), F32),
        compiler_params=_cparams(("parallel", "arbitrary")),
        name="resmm",
    )(a, w, hres)


def _ple_body(x_ref, xc_ref, g_ref, p_ref, wp_ref, wg_ref, o_ref, xn_ref):
    @pl.when(pl.program_id(1) == 0)
    def _():
        xn_ref[...] = _rms_bf16(x_ref[...], g_ref[...])

    gate = jax.nn.sigmoid(_dot(xn_ref[...], wg_ref[...]))
    o_ref[...] = xc_ref[...] + _dot(p_ref[...].astype(BF16), wp_ref[...]) * gate


def _ple(x, g, p, wp, wg, layer, tm, tn):
    m, d = x.shape
    pd = p.shape[-1]
    return pl.pallas_call(
        _ple_body,
        grid=(m // tm, d // tn),
        in_specs=[pl.BlockSpec((tm, d), lambda i, j: (i, 0)),
                  pl.BlockSpec((tm, tn), lambda i, j: (i, j)),
                  pl.BlockSpec((None, 1, d), lambda i, j: (layer, 0, 0)),
                  pl.BlockSpec((tm, pd), lambda i, j: (i, 0)),
                  pl.BlockSpec((None, pd, tn), lambda i, j: (layer, 0, j)),
                  pl.BlockSpec((None, d, tn), lambda i, j: (layer, 0, j))],
        out_specs=pl.BlockSpec((tm, tn), lambda i, j: (i, j)),
        out_shape=jax.ShapeDtypeStruct((m, d), F32),
        scratch_shapes=[pltpu.VMEM((tm, d), BF16)],
        compiler_params=_cparams(("parallel", "arbitrary")),
        name="ple",
    )(x, x, g, p, wp, wg)


def _final_norm_body(x_ref, g_ref, o_ref):
    x = x_ref[...]
    inv = lax.rsqrt(jnp.mean(x * x, axis=-1, keepdims=True) + RMS_EPS)
    o_ref[...] = x * inv * g_ref[...]


def _final_norm(x, g, tm):
    m, d = x.shape
    return pl.pallas_call(
        _final_norm_body,
        grid=(m // tm,),
        in_specs=[pl.BlockSpec((tm, d), lambda i: (i, 0)), pl.BlockSpec((1, d), lambda i: (0, 0))],
        out_specs=pl.BlockSpec((tm, d), lambda i: (i, 0)),
        out_shape=jax.ShapeDtypeStruct((m, d), F32),
        compiler_params=_cparams(("parallel",)),
        name="final_norm",
    )(x, g)


def _pick_tile(n, pref):
    t = min(pref, n)
    while n % t:
        t //= 2
    return t


def kernel(x_prompt, x_sample, state_ssm_re, state_ssm_im, state_pool, p_prompt, p_sample, g_ffn1, w_ffn1_gate, w_ffn1_up, w_ffn1_down, g_mix, w_in, ssm_a_re, ssm_a_im, ssm_log_dt, ssm_b_re, ssm_b_im, ssm_c_re, ssm_c_im, ssm_d, w_glu_a, w_glu_b, w_pool, pool_scale, w_pool_up, w_out, g_ffn2, w_ffn2_gate, w_ffn2_up, w_ffn2_down, g_ple, w_ple, w_ple_gate, g_final):
    nb, seq, d = x_prompt.shape
    ns, dseq, _ = x_sample.shape
    depth, n_groups, p_state = ssm_a_re.shape
    h = ssm_b_re.shape[-1]
    sw = n_groups * h
    pw = pool_scale.shape[-1]
    buf = state_pool.shape[2]
    chunk = SSM_CHUNK
    assert h == SSM_GROUP and dseq * 2 == chunk and seq % chunk == 0 and buf == max(POOL_WINDOWS) - 1
    n_chunks = seq // chunk
    assert n_chunks & (n_chunks - 1) == 0 and n_chunks <= 128
    mp, ms = nb * seq, ns * dseq
    m = mp + ms
    tm = _pick_tile(m, 1024)

    bf = lambda a: a.astype(BF16)
    g3 = lambda a: a.reshape(depth, 1, -1)
    wg1, wu1, wd1 = bf(w_ffn1_gate), bf(w_ffn1_up), bf(w_ffn1_down)
    wg2, wu2, wd2 = bf(w_ffn2_gate), bf(w_ffn2_up), bf(w_ffn2_down)
    win, wga, wgb, wpu, wo = bf(w_in), bf(w_glu_a), bf(w_glu_b), bf(w_pool_up), bf(w_out)
    wpl, wpg, wpool = bf(w_ple), bf(w_ple_gate), bf(w_pool)
    gf1, gmx, gf2, gpl = g3(g_ffn1), g3(g_mix), g3(g_ffn2), g3(g_ple)
    pscale = g3(pool_scale)

    flat = lambda a: a.reshape((depth * n_groups,) + a.shape[2:])
    ops = _ssm_weights(flat(ssm_a_re), flat(ssm_a_im), flat(ssm_log_dt), flat(ssm_b_re), flat(ssm_b_im),
                       flat(ssm_c_re), flat(ssm_c_im), chunk)
    d_t = jnp.tile(flat(ssm_d), (1, chunk)).reshape(depth * n_groups, chunk * h, 1)

    hcur = jnp.concatenate([x_prompt.reshape(mp, d), x_sample.reshape(ms, d)], axis=0)
    tf = _pick_tile(w_ffn1_gate.shape[-1], 512)
    tn_in = _pick_tile(sw + pw, 1024)
    tn = _pick_tile(d, 512)
    gpb = min(8, n_groups)
    new_re_p, new_im_p, new_pool_p, new_re_s, new_im_s, new_pool_s = [], [], [], [], [], []
    for i in range(depth):
        h1 = _ffn(hcur, gf1, wg1, wu1, wd1, i, tm, tf)
        z = _inproj(h1, gmx, win, i, tm, tn_in, sw + pw)

        ua_p = z[:mp, :sw].reshape(nb, n_chunks, chunk, sw).transpose(3, 2, 0, 1).reshape(sw, mp)
        ua_s = z[mp:, :sw].reshape(ns, dseq, sw).transpose(2, 1, 0).reshape(sw, ms)
        yt_p, sr_p, si_p = _ssm(ua_p, d_t, ops, i, n_groups, chunk, n_chunks, gpb=gpb)
        s0 = (state_ssm_re[i].transpose(1, 2, 0), state_ssm_im[i].transpose(1, 2, 0))
        yt_s, sr_s, si_s = _ssm(ua_s, d_t, ops, i, n_groups, dseq, 1, s0=s0, gpb=gpb)
        ga = jnp.concatenate([
            yt_p.reshape(sw, chunk, nb, n_chunks).transpose(2, 3, 1, 0).reshape(mp, sw),
            yt_s.reshape(sw, dseq, ns).transpose(2, 1, 0).reshape(ms, sw)], axis=0)
        new_re_p.append(sr_p.transpose(2, 0, 1))
        new_im_p.append(si_p.transpose(2, 0, 1))
        new_re_s.append(sr_s.transpose(2, 0, 1))
        new_im_s.append(si_s.transpose(2, 0, 1))

        ub_s = z[mp:, sw:sw + pw].reshape(ns, dseq, pw)
        yb_p = _pool_seq(z, wpool, pscale, i, nb, seq, pw, sw // pw)
        yb_s = _pool_step(ub_s, state_pool, wpool, pscale, i)
        yb = jnp.concatenate([yb_p, yb_s.transpose(1, 0, 2).reshape(ms, pw)], axis=0)
        new_pool_p.append(z[:mp, sw:sw + pw].reshape(nb, seq, pw)[:, seq - buf:, :])
        new_pool_s.append(jnp.concatenate([state_pool[i], ub_s], axis=1)[:, dseq:, :])

        merged = _mix(ga, yb, z, wga, wgb, wpu, i, tm, tn, sw + pw)
        h2 = _resmm(merged, wo, h1, i, tm, tn)
        h3 = _ffn(h2, gf2, wg2, wu2, wd2, i, tm, tf)
        p_i = jnp.concatenate([p_prompt[i].reshape(mp, -1), p_sample[i].reshape(ms, -1)], axis=0)
        hcur = _ple(h3, gpl, p_i, wpl, wpg, i, tm, tn)

    y = _final_norm(hcur, g_final.reshape(1, d), tm)
    return (y[:mp].reshape(nb, seq, d), y[mp:].reshape(ns, dseq, d),
            jnp.stack(new_re_p), jnp.stack(new_im_p), jnp.stack(new_pool_p),
            jnp.stack(new_re_s), jnp.stack(new_im_s), jnp.stack(new_pool_s))
```

```python
import functools
import math

import numpy as np
import jax
import jax.numpy as jnp
from jax import lax
from jax.experimental import pallas as pl
from jax.experimental.pallas import tpu as pltpu

F32 = jnp.float32
BF16 = jnp.bfloat16
RMS_EPS = 1e-6
POOL_WINDOWS = (2, 4, 8, 16)
SSM_GROUP = 16
SSM_CHUNK = 16
GELU_C = math.sqrt(2.0 / math.pi)
VMEM_LIMIT = 56 * 1024 * 1024
MXU_COLS = 256


def _cparams(sem):
    return pltpu.CompilerParams(dimension_semantics=sem, vmem_limit_bytes=VMEM_LIMIT)


def _rms_bf16(x, g):
    inv = lax.rsqrt(jnp.mean(x * x, axis=-1, keepdims=True) + RMS_EPS)
    return (x * inv * g).astype(BF16)


def _dot(a, b):
    return jnp.dot(a, b, preferred_element_type=F32)


def _col_chunks(width, chunk):
    chunk = min(chunk, width)
    return [slice(c0, c0 + chunk) for c0 in range(0, width, chunk)]


def _ffn_body(x_ref, g_ref, wg_ref, wu_ref, wd_ref, o_ref, xn_ref):
    j = pl.program_id(1)

    @pl.when(j == 0)
    def _():
        xn_ref[...] = _rms_bf16(x_ref[...], g_ref[...])
        o_ref[...] = jnp.zeros_like(o_ref)

    xn = xn_ref[...]
    a = _dot(xn, wg_ref[...])
    b = _dot(xn, wu_ref[...])
    mid = (a * jax.nn.sigmoid(a) * b).astype(BF16)
    for cs in _col_chunks(o_ref.shape[1], 2 * MXU_COLS):
        o_ref[:, cs] += _dot(mid, wd_ref[:, cs])

    @pl.when(j == pl.num_programs(1) - 1)
    def _():
        o_ref[...] = x_ref[...] + 0.5 * o_ref[...]


def _ffn(x, g, wg, wu, wd, layer, tm, tf):
    m, d = x.shape
    f = wg.shape[-1]
    return pl.pallas_call(
        _ffn_body,
        grid=(m // tm, f // tf),
        in_specs=[
            pl.BlockSpec((tm, d), lambda i, j: (i, 0), pipeline_mode=pl.Buffered(1)),
            pl.BlockSpec((None, 1, d), lambda i, j: (layer, 0, 0)),
            pl.BlockSpec((None, d, tf), lambda i, j: (layer, 0, j)),
            pl.BlockSpec((None, d, tf), lambda i, j: (layer, 0, j)),
            pl.BlockSpec((None, tf, d), lambda i, j: (layer, j, 0)),
        ],
        out_specs=pl.BlockSpec((tm, d), lambda i, j: (i, 0), pipeline_mode=pl.Buffered(1)),
        out_shape=jax.ShapeDtypeStruct((m, d), F32),
        scratch_shapes=[pltpu.VMEM((tm, d), BF16)],
        compiler_params=_cparams(("parallel", "arbitrary")),
        name="ffn",
    )(x, g, wg, wu, wd)


def _inproj_body(x_ref, g_ref, w_ref, o_ref, xn_ref, *, gate_from):
    j = pl.program_id(1)

    @pl.when(j == 0)
    def _():
        xn_ref[...] = _rms_bf16(x_ref[...], g_ref[...])

    is_gate = j >= gate_from
    xn = xn_ref[...]
    for cs in _col_chunks(o_ref.shape[1], MXU_COLS):
        r = _dot(xn, w_ref[:, cs])
        o_ref[:, cs] = jnp.where(is_gate, jax.nn.sigmoid(r), r)


def _inproj(x, g, w, layer, tm, tn, n_plain):
    m, d = x.shape
    n = w.shape[-1]
    return pl.pallas_call(
        functools.partial(_inproj_body, gate_from=n_plain // tn),
        grid=(m // tm, n // tn),
        in_specs=[
            pl.BlockSpec((tm, d), lambda i, j: (i, 0)),
            pl.BlockSpec((None, 1, d), lambda i, j: (layer, 0, 0)),
            pl.BlockSpec((None, d, tn), lambda i, j: (layer, 0, j)),
        ],
        out_specs=pl.BlockSpec((tm, tn), lambda i, j: (i, j)),
        out_shape=jax.ShapeDtypeStruct((m, n), F32),
        scratch_shapes=[pltpu.VMEM((tm, d), BF16)],
        compiler_params=_cparams(("parallel", "arbitrary")),
        name="inproj",
    )(x, g, w)


def _cmul(ar, ai, br, bi):
    return ar * br - ai * bi, ar * bi + ai * br


def _ssm_weights_body(ar_ref, ai_ref, ldt_ref, btr_ref, bti_ref, ctr_ref, cti_ref, e_col, e_row,
                      tt_o, wor_o, woi_o, wstr_o, wsti_o, scr_o, sci_o, *, chunk, gblk):
    h = SSM_GROUP
    causal = e_col[...] >= e_row[...]
    nt = (((1,), (1,)), ((), ()))
    hi = lax.Precision.HIGHEST

    def rows(pows):
        width = pows[0][0].shape[1]
        return tuple(jnp.concatenate([jnp.broadcast_to(x[k], (h, width)) for x in pows], axis=0)
                     for k in (0, 1))

    for gl in range(gblk):
        dt = jnp.exp(ldt_ref[gl])
        a_re, a_im = ar_ref[gl], ai_ref[gl]
        mag = jnp.exp(a_re * dt)
        ang = a_im * dt
        lr, li = mag * jnp.cos(ang), mag * jnp.sin(ang)
        den = a_re * a_re + a_im * a_im
        num_re = lr - 1.0
        k_re = (num_re * a_re + li * a_im) / den
        k_im = (li * a_re - num_re * a_im) / den
        inv = 1.0 / (lr * lr + li * li)
        nr, ni = lr * inv, -li * inv
        pw = [(jnp.ones_like(lr), jnp.zeros_like(lr))]
        npw = list(pw)
        for _ in range(chunk):
            pw.append(_cmul(*pw[-1], lr, li))
            npw.append(_cmul(*npw[-1], nr, ni))
        kb = _cmul(k_re, k_im, btr_ref[gl], bti_ref[gl])
        c = (ctr_ref[gl], cti_ref[gl])

        l_re, l_im = _cmul(*c, *rows(pw[:chunk]))
        r_re, r_im = _cmul(*rows(npw[:chunk]), *kb)
        kmat = (lax.dot_general(l_re, r_re, nt, precision=hi, preferred_element_type=F32)
                - lax.dot_general(l_im, r_im, nt, precision=hi, preferred_element_type=F32))
        tt_o[gl] = jnp.where(causal, kmat, 0.0).astype(BF16)

        e_re, e_im = _cmul(*c, *rows(pw[1:chunk + 1]))
        wor_o[gl] = e_re.astype(BF16)
        woi_o[gl] = (-e_im).astype(BF16)

        s_re, s_im = _cmul(*rows(pw[chunk - 1::-1]), *kb)
        wstr_o[gl] = s_re.astype(BF16)
        wsti_o[gl] = s_im.astype(BF16)

        sc = [pw[chunk]]
        for _ in range(6):
            sc.append(_cmul(*sc[-1], *sc[-1]))
        sc.append(pw[chunk // 2])
        scr_o[gl] = jnp.concatenate([x[0] for x in sc], axis=0)
        sci_o[gl] = jnp.concatenate([x[1] for x in sc], axis=0)


def _ssm_weights(a_re, a_im, log_dt, b_re, b_im, c_re, c_im, chunk):
    dg, p = a_re.shape
    h = b_re.shape[-1]
    th = chunk * h
    tau = np.repeat(np.arange(chunk, dtype=np.float32), h)
    e_col = jnp.asarray(tau.reshape(th, 1))
    e_row = jnp.asarray(tau.reshape(1, th))
    gblk = min(16, dg)
    row = lambda x: x.reshape(dg, 1, p)
    bt_t = lambda x: jnp.tile(jnp.swapaxes(x, 1, 2), (1, chunk, 1))
    c_t = lambda x: jnp.tile(x, (1, chunk, 1))
    per_g = lambda *s: pl.BlockSpec((gblk,) + s, lambda g: (g,) + (0,) * len(s))
    const = lambda *s: pl.BlockSpec(s, lambda g: (0,) * len(s))
    tt, wor, woi, wstr, wsti, scr, sci = pl.pallas_call(
        functools.partial(_ssm_weights_body, chunk=chunk, gblk=gblk),
        grid=(dg // gblk,),
        in_specs=[per_g(1, p), per_g(1, p), per_g(1, 1),
                  per_g(th, p), per_g(th, p), per_g(th, p), per_g(th, p),
                  const(th, 1), const(1, th)],
        out_specs=[per_g(th, th), per_g(th, p), per_g(th, p), per_g(th, p), per_g(th, p),
                   per_g(8, p), per_g(8, p)],
        out_shape=[jax.ShapeDtypeStruct((dg, th, th), BF16),
                   jax.ShapeDtypeStruct((dg, th, p), BF16),
                   jax.ShapeDtypeStruct((dg, th, p), BF16),
                   jax.ShapeDtypeStruct((dg, th, p), BF16),
                   jax.ShapeDtypeStruct((dg, th, p), BF16),
                   jax.ShapeDtypeStruct((dg, 8, p), F32),
                   jax.ShapeDtypeStruct((dg, 8, p), F32)],
        compiler_params=_cparams(("parallel",)),
        name="ssm_weights",
    )(row(a_re), row(a_im), log_dt.reshape(dg, 1, 1),
      bt_t(b_re), bt_t(b_im), c_t(c_re), c_t(c_im), e_col, e_row)
    sw = lambda x: jnp.swapaxes(x, 1, 2)
    return tt, wor, woi, sw(wstr), sw(wsti), sw(scr), sw(sci)


def _ssm_body(*refs, steps, chunks, gpb, has_init):
    if has_init:
        (ut_ref, d_ref, tt_ref, wor_ref, woi_ref, wsr_ref, wsi_ref, scr_ref, sci_ref,
         s0r_ref, s0i_ref, yt_ref, sfr_ref, sfi_ref) = refs
    else:
        (ut_ref, d_ref, tt_ref, wor_ref, woi_ref, wsr_ref, wsi_ref, scr_ref, sci_ref,
         yt_ref, sfr_ref, sfi_ref) = refs
    h = SSM_GROUP
    th = steps * h
    w = ut_ref.shape[1] // steps
    p = scr_ref.shape[1]
    ws_off = wsr_ref.shape[2] - th
    for gl in range(gpb):
        rows = slice(gl * h, (gl + 1) * h)
        u = jnp.concatenate([ut_ref[rows, t * w:(t + 1) * w] for t in range(steps)], axis=0)
        ub = u.astype(BF16)
        y = _dot(tt_ref[gl, :th, :th], ub)
        xr = _dot(wsr_ref[gl, :, ws_off:], ub)
        xi = _dot(wsi_ref[gl, :, ws_off:], ub)
        if chunks > 1:
            c_idx = lax.broadcasted_iota(jnp.int32, (p, w), 1) & (chunks - 1)
            sr, si = xr, xi
            k = 0
            while (1 << k) < chunks:
                sh = 1 << k
                rr = pltpu.roll(sr, sh, axis=1)
                ri = pltpu.roll(si, sh, axis=1)
                mr = scr_ref[gl, :, k:k + 1]
                mi = sci_ref[gl, :, k:k + 1]
                keep = c_idx >= sh
                sr = sr + jnp.where(keep, mr * rr - mi * ri, 0.0)
                si = si + jnp.where(keep, mr * ri + mi * rr, 0.0)
                k += 1
            first = c_idx >= 1
            pr = jnp.where(first, pltpu.roll(sr, 1, axis=1), 0.0)
            pi = jnp.where(first, pltpu.roll(si, 1, axis=1), 0.0)
            for n in range(w // chunks):
                last = n * chunks + chunks - 1
                sfr_ref[gl, :, n:n + 1] = sr[:, last:last + 1]
                sfi_ref[gl, :, n:n + 1] = si[:, last:last + 1]
        else:
            pr, pi = s0r_ref[gl], s0i_ref[gl]
            lr = scr_ref[gl, :, 7:8]
            li = sci_ref[gl, :, 7:8]
            sfr_ref[gl] = lr * pr - li * pi + xr
            sfi_ref[gl] = lr * pi + li * pr + xi
        y = y + _dot(wor_ref[gl, :th, :], pr.astype(BF16)) + _dot(woi_ref[gl, :th, :], pi.astype(BF16))
        y = y + d_ref[gl, :th, :] * u
        y = 0.5 * y * (1.0 + jnp.tanh(GELU_C * (y + 0.044715 * (y * y * y))))
        yb = y.astype(BF16)
        for t in range(steps):
            yt_ref[rows, t * w:(t + 1) * w] = yb[t * h:(t + 1) * h, :]


def _ssm(ut, d_t, ops, layer, n_groups, steps, chunks, s0=None, gpb=8):
    tt, wor, woi, wsr, wsi, scr, sci = ops
    c, lanes = ut.shape
    h = SSM_GROUP
    w = lanes // steps
    nseq = w // chunks
    p = scr.shape[1]
    thf = tt.shape[1]
    base = layer * (n_groups // gpb)
    wblk = lambda *s: pl.BlockSpec((gpb,) + s, lambda g: (base + g,) + (0,) * len(s))
    in_specs = [pl.BlockSpec((gpb * h, lanes), lambda g: (g, 0)),
                wblk(thf, 1), wblk(thf, thf), wblk(thf, p), wblk(thf, p), wblk(p, thf), wblk(p, thf),
                wblk(p, scr.shape[2]), wblk(p, scr.shape[2])]
    args = [ut, d_t, tt, wor, woi, wsr, wsi, scr, sci]
    if s0 is not None:
        in_specs += [pl.BlockSpec((gpb, p, w), lambda g: (g, 0, 0))] * 2
        args += list(s0)
    sf_spec = pl.BlockSpec((gpb, p, nseq), lambda g: (g, 0, 0))
    return pl.pallas_call(
        functools.partial(_ssm_body, steps=steps, chunks=chunks, gpb=gpb, has_init=s0 is not None),
        grid=(n_groups // gpb,),
        in_specs=in_specs,
        out_specs=[pl.BlockSpec((gpb * h, lanes), lambda g: (g, 0)), sf_spec, sf_spec],
        out_shape=[jax.ShapeDtypeStruct((c, lanes), BF16),
                   jax.ShapeDtypeStruct((n_groups, p, nseq), F32),
                   jax.ShapeDtypeStruct((n_groups, p, nseq), F32)],
        compiler_params=_cparams(("parallel",)),
        name="ssm_chunks" if s0 is None else "ssm_step",
    )(*args)


def _pool_seq_body(u_ref, w_ref, sc_ref, o_ref, z_ref):
    l, c = u_ref.shape
    pad = z_ref.shape[0] - l
    cg = c // len(POOL_WINDOWS)
    z_ref[:pad, :] = jnp.zeros((pad, c), F32)
    z_ref[pad:, :] = u_ref[...]
    pos1 = (lax.broadcasted_iota(jnp.int32, (l, 1), 0) + 1).astype(F32)
    for gi, win in enumerate(POOL_WINDOWS):
        cols = slice(gi * cg, (gi + 1) * cg)
        cur = z_ref[pad:, cols]
        tot = cur
        for k in range(1, win):
            tot = tot + z_ref[pad - k:pad - k + l, cols]
        mean = tot / jnp.minimum(pos1, float(win))
        mixed = _dot((mean - cur).astype(BF16), w_ref[gi])
        o_ref[:, cols] = (mixed * sc_ref[:, cols]).astype(BF16)


def _pool_seq(z, w_pool, scale, layer, nseq, seqlen, width, col_blk):
    return pl.pallas_call(
        _pool_seq_body,
        grid=(nseq,),
        in_specs=[pl.BlockSpec((seqlen, width), lambda n: (n, col_blk)),
                  pl.BlockSpec((None,) + w_pool.shape[1:], lambda n: (layer, 0, 0, 0)),
                  pl.BlockSpec((None, 1, width), lambda n: (layer, 0, 0))],
        out_specs=pl.BlockSpec((seqlen, width), lambda n: (n, 0)),
        out_shape=jax.ShapeDtypeStruct((nseq * seqlen, width), BF16),
        scratch_shapes=[pltpu.VMEM((seqlen + 16, width), F32)],
        compiler_params=_cparams(("parallel",)),
        name="pool_seq",
    )(z, w_pool, scale)


def _pool_step_body(u_ref, prev_ref, w_ref, sc_ref, o_ref):
    n, steps, c = u_ref.shape
    buf = prev_ref.shape[1]
    cg = c // len(POOL_WINDOWS)

    def row(j, cols):
        return prev_ref[:, j, cols] if j < buf else u_ref[:, j - buf, cols]

    for gi, win in enumerate(POOL_WINDOWS):
        cols = slice(gi * cg, (gi + 1) * cg)
        for t in range(steps):
            cur = row(buf + t, cols)
            tot = cur
            for k in range(1, win):
                tot = tot + row(buf + t - k, cols)
            mixed = _dot((tot / float(win) - cur).astype(BF16), w_ref[gi])
            o_ref[t, :, cols] = (mixed * sc_ref[:, cols]).astype(BF16)


def _pool_step(u, prev, w_pool, scale, layer):
    n, steps, c = u.shape
    return pl.pallas_call(
        _pool_step_body,
        grid=(1,),
        in_specs=[pl.BlockSpec(u.shape, lambda i: (0, 0, 0)),
                  pl.BlockSpec((None,) + prev.shape[1:], lambda i: (layer, 0, 0, 0)),
                  pl.BlockSpec((None,) + w_pool.shape[1:], lambda i: (layer, 0, 0, 0)),
                  pl.BlockSpec((None, 1, c), lambda i: (layer, 0, 0))],
        out_specs=pl.BlockSpec((steps, n, c), lambda i: (0, 0, 0)),
        out_shape=jax.ShapeDtypeStruct((steps, n, c), BF16),
        compiler_params=_cparams(("arbitrary",)),
        name="pool_step",
    )(u, prev, w_pool, scale)


def _mix_body(ga_ref, yb_ref, sa_ref, sb_ref, wa_ref, wb_ref, wp_ref, o_ref):
    ga = ga_ref[...]
    br_a = _dot(ga, wa_ref[...]) * jax.nn.sigmoid(_dot(ga, wb_ref[...]))
    br_b = _dot(yb_ref[...], wp_ref[...])
    o_ref[...] = (sa_ref[...] * br_a + sb_ref[...] * br_b).astype(BF16)


def _mix(ga, yb, z, wa, wb, wp, layer, tm, tn, gate_col):
    m, k = ga.shape
    n = wa.shape[-1]
    gblk = gate_col // tn
    wspec = pl.BlockSpec((None, k, tn), lambda i, j: (layer, 0, j))
    return pl.pallas_call(
        _mix_body,
        grid=(m // tm, n // tn),
        in_specs=[pl.BlockSpec((tm, k), lambda i, j: (i, 0)),
                  pl.BlockSpec((tm, k), lambda i, j: (i, 0)),
                  pl.BlockSpec((tm, tn), lambda i, j: (i, gblk + j)),
                  pl.BlockSpec((tm, tn), lambda i, j: (i, gblk + n // tn + j)),
                  wspec, wspec, wspec],
        out_specs=pl.BlockSpec((tm, tn), lambda i, j: (i, j)),
        out_shape=jax.ShapeDtypeStruct((m, n), BF16),
        compiler_params=_cparams(("parallel", "arbitrary")),
        name="mix",
    )(ga, yb, z, z, wa, wb, wp)


def _resmm_body(a_ref, w_ref, h_ref, o_ref):
    o_ref[...] = h_ref[...] + _dot(a_ref[...], w_ref[...])


def _resmm(a, w, hres, layer, tm, tn):
    m, k = a.shape
    n = w.shape[-1]
    return pl.pallas_call(
        _resmm_body,
        grid=(m // tm, n // tn),
        in_specs=[pl.BlockSpec((tm, k), lambda i, j: (i, 0)),
                  pl.BlockSpec((None, k, tn), lambda i, j: (layer, 0, j)),
                  pl.BlockSpec((tm, tn), lambda i, j: (i, j))],
        out_specs=pl.BlockSpec((tm, tn), lambda i, j: (i, j)),
        out_shape=jax.ShapeDtypeStruct((m, n), F32),
        compiler_params=_cparams(("parallel", "arbitrary")),
        name="resmm",
    )(a, w, hres)


def _ple_body(x_ref, xc_ref, g_ref, p_ref, wp_ref, wg_ref, o_ref, xn_ref):
    @pl.when(pl.program_id(1) == 0)
    def _():
        xn_ref[...] = _rms_bf16(x_ref[...], g_ref[...])

    gate = jax.nn.sigmoid(_dot(xn_ref[...], wg_ref[...]))
    o_ref[...] = xc_ref[...] + _dot(p_ref[...].astype(BF16), wp_ref[...]) * gate


def _ple(x, g, p, wp, wg, layer, tm, tn):
    m, d = x.shape
    pd = p.shape[-1]
    return pl.pallas_call(
        _ple_body,
        grid=(m // tm, d // tn),
        in_specs=[pl.BlockSpec((tm, d), lambda i, j: (i, 0)),
                  pl.BlockSpec((tm, tn), lambda i, j: (i, j)),
                  pl.BlockSpec((None, 1, d), lambda i, j: (layer, 0, 0)),
                  pl.BlockSpec((None, tm, pd), lambda i, j: (layer, i, 0)),
                  pl.BlockSpec((None, pd, tn), lambda i, j: (layer, 0, j)),
                  pl.BlockSpec((None, d, tn), lambda i, j: (layer, 0, j))],
        out_specs=pl.BlockSpec((tm, tn), lambda i, j: (i, j)),
        out_shape=jax.ShapeDtypeStruct((m, d), F32),
        scratch_shapes=[pltpu.VMEM((tm, d), BF16)],
        compiler_params=_cparams(("parallel", "arbitrary")),
        name="ple",
    )(x, x, g, p, wp, wg)


def _final_norm_body(x_ref, g_ref, o_ref):
    x = x_ref[...]
    inv = lax.rsqrt(jnp.mean(x * x, axis=-1, keepdims=True) + RMS_EPS)
    o_ref[...] = x * inv * g_ref[...]


def _final_norm(x, g, tm):
    m, d = x.shape
    return pl.pallas_call(
        _final_norm_body,
        grid=(m // tm,),
        in_specs=[pl.BlockSpec((tm, d), lambda i: (i, 0)), pl.BlockSpec((1, d), lambda i: (0, 0))],
        out_specs=pl.BlockSpec((tm, d), lambda i: (i, 0)),
        out_shape=jax.ShapeDtypeStruct((m, d), F32),
        compiler_params=_cparams(("parallel",)),
        name="final_norm",
    )(x, g)


def _pick_tile(n, pref):
    t = min(pref, n)
    while n % t:
        t //= 2
    return t


def kernel(x_prompt, x_sample, state_ssm_re, state_ssm_im, state_pool, p_prompt, p_sample, g_ffn1, w_ffn1_gate, w_ffn1_up, w_ffn1_down, g_mix, w_in, ssm_a_re, ssm_a_im, ssm_log_dt, ssm_b_re, ssm_b_im, ssm_c_re, ssm_c_im, ssm_d, w_glu_a, w_glu_b, w_pool, pool_scale, w_pool_up, w_out, g_ffn2, w_ffn2_gate, w_ffn2_up, w_ffn2_down, g_ple, w_ple, w_ple_gate, g_final):
    nb, seq, d = x_prompt.shape
    ns, dseq, _ = x_sample.shape
    depth, n_groups, p_state = ssm_a_re.shape
    h = ssm_b_re.shape[-1]
    sw = n_groups * h
    pw = pool_scale.shape[-1]
    buf = state_pool.shape[2]
    chunk = SSM_CHUNK
    assert h == SSM_GROUP and dseq * 2 == chunk and seq % chunk == 0 and buf == max(POOL_WINDOWS) - 1
    n_chunks = seq // chunk
    assert n_chunks & (n_chunks - 1) == 0 and n_chunks <= 128
    mp, ms = nb * seq, ns * dseq
    m = mp + ms
    tm = _pick_tile(m, 1024)

    bf = lambda a: a.astype(BF16)
    g3 = lambda a: a.reshape(depth, 1, -1)
    wg1, wu1, wd1 = bf(w_ffn1_gate), bf(w_ffn1_up), bf(w_ffn1_down)
    wg2, wu2, wd2 = bf(w_ffn2_gate), bf(w_ffn2_up), bf(w_ffn2_down)
    win, wga, wgb, wpu, wo = bf(w_in), bf(w_glu_a), bf(w_glu_b), bf(w_pool_up), bf(w_out)
    wpl, wpg, wpool = bf(w_ple), bf(w_ple_gate), bf(w_pool)
    gf1, gmx, gf2, gpl = g3(g_ffn1), g3(g_mix), g3(g_ffn2), g3(g_ple)
    pscale = g3(pool_scale)

    flat = lambda a: a.reshape((depth * n_groups,) + a.shape[2:])
    ops = _ssm_weights(flat(ssm_a_re), flat(ssm_a_im), flat(ssm_log_dt), flat(ssm_b_re), flat(ssm_b_im),
                       flat(ssm_c_re), flat(ssm_c_im), chunk)
    d_t = jnp.tile(flat(ssm_d), (1, chunk)).reshape(depth * n_groups, chunk * h, 1)

    hcur = jnp.concatenate([x_prompt.reshape(mp, d), x_sample.reshape(ms, d)], axis=0)
    p_all = jnp.concatenate([p_prompt.reshape(depth, mp, -1), p_sample.reshape(depth, ms, -1)], axis=1)
    tf = _pick_tile(w_ffn1_gate.shape[-1], 512)
    tn_in = _pick_tile(sw + pw, 1024)
    tn = _pick_tile(d, 512)
    gpb = min(8, n_groups)
    new_re_p, new_im_p, new_pool_p, new_re_s, new_im_s, new_pool_s = [], [], [], [], [], []
    for i in range(depth):
        h1 = _ffn(hcur, gf1, wg1, wu1, wd1, i, tm, tf)
        z = _inproj(h1, gmx, win, i, tm, tn_in, sw + pw)

        ua_p = z[:mp, :sw].reshape(nb, n_chunks, chunk, sw).transpose(3, 2, 0, 1).reshape(sw, mp)
        ua_s = z[mp:, :sw].reshape(ns, dseq, sw).transpose(2, 1, 0).reshape(sw, ms)
        yt_p, sr_p, si_p = _ssm(ua_p, d_t, ops, i, n_groups, chunk, n_chunks, gpb=gpb)
        s0 = (state_ssm_re[i].transpose(1, 2, 0), state_ssm_im[i].transpose(1, 2, 0))
        yt_s, sr_s, si_s = _ssm(ua_s, d_t, ops, i, n_groups, dseq, 1, s0=s0, gpb=gpb)
        ga = jnp.concatenate([
            yt_p.reshape(sw, chunk, nb, n_chunks).transpose(2, 3, 1, 0).reshape(mp, sw),
            yt_s.reshape(sw, dseq, ns).transpose(2, 1, 0).reshape(ms, sw)], axis=0)
        new_re_p.append(sr_p.transpose(2, 0, 1))
        new_im_p.append(si_p.transpose(2, 0, 1))
        new_re_s.append(sr_s.transpose(2, 0, 1))
        new_im_s.append(si_s.transpose(2, 0, 1))

        ub_s = z[mp:, sw:sw + pw].reshape(ns, dseq, pw)
        yb_p = _pool_seq(z, wpool, pscale, i, nb, seq, pw, sw // pw)
        yb_s = _pool_step(ub_s, state_pool, wpool, pscale, i)
        yb = jnp.concatenate([yb_p, yb_s.transpose(1, 0, 2).reshape(ms, pw)], axis=0)
        new_pool_p.append(z[:mp, sw:sw + pw].reshape(nb, seq, pw)[:, seq - buf:, :])
        new_pool_s.append(jnp.concatenate([state_pool[i], ub_s], axis=1)[:, dseq:, :])

        merged = _mix(ga, yb, z, wga, wgb, wpu, i, tm, tn, sw + pw)
        h2 = _resmm(merged, wo, h1, i, tm, tn)
        h3 = _ffn(h2, gf2, wg2, wu2, wd2, i, tm, tf)
        hcur = _ple(h3, gpl, p_all, wpl, wpg, i, tm, tn)

    y = _final_norm(hcur, g_final.reshape(1, d), tm)
    return (y[:mp].reshape(nb, seq, d), y[mp:].reshape(ns, dseq, d),
            jnp.stack(new_re_p), jnp.stack(new_im_p), jnp.stack(new_pool_p),
            jnp.stack(new_re_s), jnp.stack(new_im_s), jnp.stack(new_pool_s))
```

```python
import functools
import math

import numpy as np
import jax
import jax.numpy as jnp
from jax import lax
from jax.experimental import pallas as pl
from jax.experimental.pallas import tpu as pltpu

F32 = jnp.float32
BF16 = jnp.bfloat16
RMS_EPS = 1e-6
POOL_WINDOWS = (2, 4, 8, 16)
SSM_GROUP = 16
SSM_CHUNK = 16
GELU_C = math.sqrt(2.0 / math.pi)
VMEM_LIMIT = 60 * 1024 * 1024
MXU_COLS = 256


def _cparams(sem):
    return pltpu.CompilerParams(dimension_semantics=sem, vmem_limit_bytes=VMEM_LIMIT)


def _rms_bf16(x, g):
    inv = lax.rsqrt(jnp.mean(x * x, axis=-1, keepdims=True) + RMS_EPS)
    return (x * inv * g).astype(BF16)


def _dot(a, b):
    return jnp.dot(a, b, preferred_element_type=F32)


def _col_chunks(width, chunk):
    chunk = min(chunk, width)
    return [slice(c0, c0 + chunk) for c0 in range(0, width, chunk)]


def _ffn_body(x_ref, g_ref, wg_ref, wu_ref, wd_ref, o_ref, xn_ref):
    j = pl.program_id(1)

    @pl.when(j == 0)
    def _():
        xn_ref[...] = _rms_bf16(x_ref[...], g_ref[...])
        o_ref[...] = jnp.zeros_like(o_ref)

    xn = xn_ref[...]
    mids = []
    for cs in _col_chunks(wg_ref.shape[1], MXU_COLS):
        a = _dot(xn, wg_ref[:, cs].astype(BF16))
        b = _dot(xn, wu_ref[:, cs].astype(BF16))
        mids.append((a * jax.nn.sigmoid(a) * b).astype(BF16))
    mid = jnp.concatenate(mids, axis=1)
    for cs in _col_chunks(o_ref.shape[1], 2 * MXU_COLS):
        o_ref[:, cs] += _dot(mid, wd_ref[:, cs].astype(BF16))

    @pl.when(j == pl.num_programs(1) - 1)
    def _():
        o_ref[...] = x_ref[...] + 0.5 * o_ref[...]


def _ffn(x, g, wg, wu, wd, layer, tm, tf):
    m, d = x.shape
    f = wg.shape[-1]
    return pl.pallas_call(
        _ffn_body,
        grid=(m // tm, f // tf),
        in_specs=[
            pl.BlockSpec((tm, d), lambda i, j: (i, 0), pipeline_mode=pl.Buffered(1)),
            pl.BlockSpec((None, 1, d), lambda i, j: (layer, 0, 0)),
            pl.BlockSpec((None, d, tf), lambda i, j: (layer, 0, j)),
            pl.BlockSpec((None, d, tf), lambda i, j: (layer, 0, j)),
            pl.BlockSpec((None, tf, d), lambda i, j: (layer, j, 0)),
        ],
        out_specs=pl.BlockSpec((tm, d), lambda i, j: (i, 0), pipeline_mode=pl.Buffered(1)),
        out_shape=jax.ShapeDtypeStruct((m, d), F32),
        scratch_shapes=[pltpu.VMEM((tm, d), BF16)],
        compiler_params=_cparams(("parallel", "arbitrary")),
        name="ffn",
    )(x, g, wg, wu, wd)


def _inproj_body(x_ref, g_ref, w_ref, o_ref, xn_ref, *, gate_from):
    j = pl.program_id(1)

    @pl.when(j == 0)
    def _():
        xn_ref[...] = _rms_bf16(x_ref[...], g_ref[...])

    is_gate = j >= gate_from
    xn = xn_ref[...]
    for cs in _col_chunks(o_ref.shape[1], MXU_COLS):
        r = _dot(xn, w_ref[:, cs].astype(BF16))
        o_ref[:, cs] = jnp.where(is_gate, jax.nn.sigmoid(r), r)


def _inproj(x, g, w, layer, tm, tn, n_plain):
    m, d = x.shape
    n = w.shape[-1]
    return pl.pallas_call(
        functools.partial(_inproj_body, gate_from=n_plain // tn),
        grid=(m // tm, n // tn),
        in_specs=[
            pl.BlockSpec((tm, d), lambda i, j: (i, 0)),
            pl.BlockSpec((None, 1, d), lambda i, j: (layer, 0, 0)),
            pl.BlockSpec((None, d, tn), lambda i, j: (layer, 0, j)),
        ],
        out_specs=pl.BlockSpec((tm, tn), lambda i, j: (i, j)),
        out_shape=jax.ShapeDtypeStruct((m, n), F32),
        scratch_shapes=[pltpu.VMEM((tm, d), BF16)],
        compiler_params=_cparams(("parallel", "arbitrary")),
        name="inproj",
    )(x, g, w)


def _cmul(ar, ai, br, bi):
    return ar * br - ai * bi, ar * bi + ai * br


def _ssm_weights_body(ar_ref, ai_ref, ldt_ref, btr_ref, bti_ref, ctr_ref, cti_ref, e_col, e_row,
                      tt_o, wor_o, woi_o, wstr_o, wsti_o, scr_o, sci_o, *, chunk, gblk):
    h = SSM_GROUP
    causal = e_col[...] >= e_row[...]
    nt = (((1,), (1,)), ((), ()))
    hi = lax.Precision.HIGHEST

    def rows(pows):
        width = pows[0][0].shape[1]
        return tuple(jnp.concatenate([jnp.broadcast_to(x[k], (h, width)) for x in pows], axis=0)
                     for k in (0, 1))

    for gl in range(gblk):
        dt = jnp.exp(ldt_ref[gl])
        a_re, a_im = ar_ref[gl], ai_ref[gl]
        mag = jnp.exp(a_re * dt)
        ang = a_im * dt
        lr, li = mag * jnp.cos(ang), mag * jnp.sin(ang)
        den = a_re * a_re + a_im * a_im
        num_re = lr - 1.0
        k_re = (num_re * a_re + li * a_im) / den
        k_im = (li * a_re - num_re * a_im) / den
        inv = 1.0 / (lr * lr + li * li)
        nr, ni = lr * inv, -li * inv
        pw = [(jnp.ones_like(lr), jnp.zeros_like(lr))]
        npw = list(pw)
        for _ in range(chunk):
            pw.append(_cmul(*pw[-1], lr, li))
            npw.append(_cmul(*npw[-1], nr, ni))
        kb = _cmul(k_re, k_im, btr_ref[gl], bti_ref[gl])
        c = (ctr_ref[gl], cti_ref[gl])

        l_re, l_im = _cmul(*c, *rows(pw[:chunk]))
        r_re, r_im = _cmul(*rows(npw[:chunk]), *kb)
        kmat = (lax.dot_general(l_re, r_re, nt, precision=hi, preferred_element_type=F32)
                - lax.dot_general(l_im, r_im, nt, precision=hi, preferred_element_type=F32))
        tt_o[gl] = jnp.where(causal, kmat, 0.0).astype(BF16)

        e_re, e_im = _cmul(*c, *rows(pw[1:chunk + 1]))
        wor_o[gl] = e_re.astype(BF16)
        woi_o[gl] = (-e_im).astype(BF16)

        s_re, s_im = _cmul(*rows(pw[chunk - 1::-1]), *kb)
        wstr_o[gl] = s_re.astype(BF16)
        wsti_o[gl] = s_im.astype(BF16)

        sc = [pw[chunk]]
        for _ in range(6):
            sc.append(_cmul(*sc[-1], *sc[-1]))
        sc.append(pw[chunk // 2])
        scr_o[gl] = jnp.concatenate([x[0] for x in sc], axis=0)
        sci_o[gl] = jnp.concatenate([x[1] for x in sc], axis=0)


def _ssm_weights(a_re, a_im, log_dt, b_re, b_im, c_re, c_im, chunk):
    dg, p = a_re.shape
    h = b_re.shape[-1]
    th = chunk * h
    tau = np.repeat(np.arange(chunk, dtype=np.float32), h)
    e_col = jnp.asarray(tau.reshape(th, 1))
    e_row = jnp.asarray(tau.reshape(1, th))
    gblk = min(16, dg)
    row = lambda x: x.reshape(dg, 1, p)
    bt_t = lambda x: jnp.tile(jnp.swapaxes(x, 1, 2), (1, chunk, 1))
    c_t = lambda x: jnp.tile(x, (1, chunk, 1))
    per_g = lambda *s: pl.BlockSpec((gblk,) + s, lambda g: (g,) + (0,) * len(s))
    const = lambda *s: pl.BlockSpec(s, lambda g: (0,) * len(s))
    tt, wor, woi, wstr, wsti, scr, sci = pl.pallas_call(
        functools.partial(_ssm_weights_body, chunk=chunk, gblk=gblk),
        grid=(dg // gblk,),
        in_specs=[per_g(1, p), per_g(1, p), per_g(1, 1),
                  per_g(th, p), per_g(th, p), per_g(th, p), per_g(th, p),
                  const(th, 1), const(1, th)],
        out_specs=[per_g(th, th), per_g(th, p), per_g(th, p), per_g(th, p), per_g(th, p),
                   per_g(8, p), per_g(8, p)],
        out_shape=[jax.ShapeDtypeStruct((dg, th, th), BF16),
                   jax.ShapeDtypeStruct((dg, th, p), BF16),
                   jax.ShapeDtypeStruct((dg, th, p), BF16),
                   jax.ShapeDtypeStruct((dg, th, p), BF16),
                   jax.ShapeDtypeStruct((dg, th, p), BF16),
                   jax.ShapeDtypeStruct((dg, 8, p), F32),
                   jax.ShapeDtypeStruct((dg, 8, p), F32)],
        compiler_params=_cparams(("parallel",)),
        name="ssm_weights",
    )(row(a_re), row(a_im), log_dt.reshape(dg, 1, 1),
      bt_t(b_re), bt_t(b_im), c_t(c_re), c_t(c_im), e_col, e_row)
    sw = lambda x: jnp.swapaxes(x, 1, 2)
    return tt, wor, woi, sw(wstr), sw(wsti), sw(scr), sw(sci)


def _ssm_body(*refs, steps, chunks, gpb, has_init):
    if has_init:
        (ut_ref, d_ref, tt_ref, wor_ref, woi_ref, wsr_ref, wsi_ref, scr_ref, sci_ref,
         s0r_ref, s0i_ref, yt_ref, sfr_ref, sfi_ref) = refs
    else:
        (ut_ref, d_ref, tt_ref, wor_ref, woi_ref, wsr_ref, wsi_ref, scr_ref, sci_ref,
         yt_ref, sfr_ref, sfi_ref) = refs
    h = SSM_GROUP
    th = steps * h
    w = ut_ref.shape[1] // steps
    p = scr_ref.shape[1]
    ws_off = wsr_ref.shape[2] - th
    for gl in range(gpb):
        rows = slice(gl * h, (gl + 1) * h)
        u = jnp.concatenate([ut_ref[rows, t * w:(t + 1) * w] for t in range(steps)], axis=0)
        ub = u.astype(BF16)
        y = _dot(tt_ref[gl, :th, :th], ub)
        xr = _dot(wsr_ref[gl, :, ws_off:], ub)
        xi = _dot(wsi_ref[gl, :, ws_off:], ub)
        if chunks > 1:
            c_idx = lax.broadcasted_iota(jnp.int32, (p, w), 1) & (chunks - 1)
            sr, si = xr, xi
            k = 0
            while (1 << k) < chunks:
                sh = 1 << k
                rr = pltpu.roll(sr, sh, axis=1)
                ri = pltpu.roll(si, sh, axis=1)
                mr = scr_ref[gl, :, k:k + 1]
                mi = sci_ref[gl, :, k:k + 1]
                keep = c_idx >= sh
                sr = sr + jnp.where(keep, mr * rr - mi * ri, 0.0)
                si = si + jnp.where(keep, mr * ri + mi * rr, 0.0)
                k += 1
            first = c_idx >= 1
            pr = jnp.where(first, pltpu.roll(sr, 1, axis=1), 0.0)
            pi = jnp.where(first, pltpu.roll(si, 1, axis=1), 0.0)
            for n in range(w // chunks):
                last = n * chunks + chunks - 1
                sfr_ref[gl, :, n:n + 1] = sr[:, last:last + 1]
                sfi_ref[gl, :, n:n + 1] = si[:, last:last + 1]
        else:
            pr, pi = s0r_ref[gl], s0i_ref[gl]
            lr = scr_ref[gl, :, 7:8]
            li = sci_ref[gl, :, 7:8]
            sfr_ref[gl] = lr * pr - li * pi + xr
            sfi_ref[gl] = lr * pi + li * pr + xi
        y = y + _dot(wor_ref[gl, :th, :], pr.astype(BF16)) + _dot(woi_ref[gl, :th, :], pi.astype(BF16))
        y = y + d_ref[gl, :th, :] * u
        y = 0.5 * y * (1.0 + jnp.tanh(GELU_C * (y + 0.044715 * (y * y * y))))
        yb = y.astype(BF16)
        for t in range(steps):
            yt_ref[rows, t * w:(t + 1) * w] = yb[t * h:(t + 1) * h, :]


def _ssm(ut, d_t, ops, layer, n_groups, steps, chunks, s0=None, gpb=8):
    tt, wor, woi, wsr, wsi, scr, sci = ops
    c, lanes = ut.shape
    h = SSM_GROUP
    w = lanes // steps
    nseq = w // chunks
    p = scr.shape[1]
    thf = tt.shape[1]
    base = layer * (n_groups // gpb)
    wblk = lambda *s: pl.BlockSpec((gpb,) + s, lambda g: (base + g,) + (0,) * len(s))
    in_specs = [pl.BlockSpec((gpb * h, lanes), lambda g: (g, 0)),
                wblk(thf, 1), wblk(thf, thf), wblk(thf, p), wblk(thf, p), wblk(p, thf), wblk(p, thf),
                wblk(p, scr.shape[2]), wblk(p, scr.shape[2])]
    args = [ut, d_t, tt, wor, woi, wsr, wsi, scr, sci]
    if s0 is not None:
        in_specs += [pl.BlockSpec((gpb, p, w), lambda g: (g, 0, 0))] * 2
        args += list(s0)
    sf_spec = pl.BlockSpec((gpb, p, nseq), lambda g: (g, 0, 0))
    return pl.pallas_call(
        functools.partial(_ssm_body, steps=steps, chunks=chunks, gpb=gpb, has_init=s0 is not None),
        grid=(n_groups // gpb,),
        in_specs=in_specs,
        out_specs=[pl.BlockSpec((gpb * h, lanes), lambda g: (g, 0)), sf_spec, sf_spec],
        out_shape=[jax.ShapeDtypeStruct((c, lanes), BF16),
                   jax.ShapeDtypeStruct((n_groups, p, nseq), F32),
                   jax.ShapeDtypeStruct((n_groups, p, nseq), F32)],
        compiler_params=_cparams(("parallel",)),
        name="ssm_chunks" if s0 is None else "ssm_step",
    )(*args)


def _pool_seq_body(u_ref, w_ref, sc_ref, o_ref, z_ref):
    l, c = u_ref.shape
    pad = z_ref.shape[0] - l
    cg = c // len(POOL_WINDOWS)
    z_ref[:pad, :] = jnp.zeros((pad, c), F32)
    z_ref[pad:, :] = u_ref[...]
    pos1 = (lax.broadcasted_iota(jnp.int32, (l, 1), 0) + 1).astype(F32)
    for gi, win in enumerate(POOL_WINDOWS):
        cols = slice(gi * cg, (gi + 1) * cg)
        cur = z_ref[pad:, cols]
        tot = cur
        for k in range(1, win):
            tot = tot + z_ref[pad - k:pad - k + l, cols]
        mean = tot / jnp.minimum(pos1, float(win))
        mixed = _dot((mean - cur).astype(BF16), w_ref[gi].astype(BF16))
        o_ref[:, cols] = (mixed * sc_ref[:, cols]).astype(BF16)


def _pool_seq(z, w_pool, scale, layer, nseq, seqlen, width, col_blk):
    return pl.pallas_call(
        _pool_seq_body,
        grid=(nseq,),
        in_specs=[pl.BlockSpec((seqlen, width), lambda n: (n, col_blk)),
                  pl.BlockSpec((None,) + w_pool.shape[1:], lambda n: (layer, 0, 0, 0)),
                  pl.BlockSpec((None, 1, width), lambda n: (layer, 0, 0))],
        out_specs=pl.BlockSpec((seqlen, width), lambda n: (n, 0)),
        out_shape=jax.ShapeDtypeStruct((nseq * seqlen, width), BF16),
        scratch_shapes=[pltpu.VMEM((seqlen + 16, width), F32)],
        compiler_params=_cparams(("parallel",)),
        name="pool_seq",
    )(z, w_pool, scale)


def _pool_step_body(u_ref, prev_ref, w_ref, sc_ref, o_ref):
    n, steps, c = u_ref.shape
    buf = prev_ref.shape[1]
    cg = c // len(POOL_WINDOWS)

    def row(j, cols):
        return prev_ref[:, j, cols] if j < buf else u_ref[:, j - buf, cols]

    for gi, win in enumerate(POOL_WINDOWS):
        cols = slice(gi * cg, (gi + 1) * cg)
        for t in range(steps):
            cur = row(buf + t, cols)
            tot = cur
            for k in range(1, win):
                tot = tot + row(buf + t - k, cols)
            mixed = _dot((tot / float(win) - cur).astype(BF16), w_ref[gi].astype(BF16))
            o_ref[t, :, cols] = (mixed * sc_ref[:, cols]).astype(BF16)


def _pool_step(u, prev, w_pool, scale, layer):
    n, steps, c = u.shape
    return pl.pallas_call(
        _pool_step_body,
        grid=(1,),
        in_specs=[pl.BlockSpec(u.shape, lambda i: (0, 0, 0)),
                  pl.BlockSpec((None,) + prev.shape[1:], lambda i: (layer, 0, 0, 0)),
                  pl.BlockSpec((None,) + w_pool.shape[1:], lambda i: (layer, 0, 0, 0)),
                  pl.BlockSpec((None, 1, c), lambda i: (layer, 0, 0))],
        out_specs=pl.BlockSpec((steps, n, c), lambda i: (0, 0, 0)),
        out_shape=jax.ShapeDtypeStruct((steps, n, c), BF16),
        compiler_params=_cparams(("arbitrary",)),
        name="pool_step",
    )(u, prev, w_pool, scale)


def _mix_body(ga_ref, yb_ref, sa_ref, sb_ref, wa_ref, wb_ref, wp_ref, o_ref):
    ga = ga_ref[...]
    br_a = _dot(ga, wa_ref[...].astype(BF16)) * jax.nn.sigmoid(_dot(ga, wb_ref[...].astype(BF16)))
    br_b = _dot(yb_ref[...], wp_ref[...].astype(BF16))
    o_ref[...] = (sa_ref[...] * br_a + sb_ref[...] * br_b).astype(BF16)


def _mix(ga, yb, z, wa, wb, wp, layer, tm, tn, gate_col):
    m, k = ga.shape
    n = wa.shape[-1]
    gblk = gate_col // tn
    wspec = pl.BlockSpec((None, k, tn), lambda i, j: (layer, 0, j))
    return pl.pallas_call(
        _mix_body,
        grid=(m // tm, n // tn),
        in_specs=[pl.BlockSpec((tm, k), lambda i, j: (i, 0)),
                  pl.BlockSpec((tm, k), lambda i, j: (i, 0)),
                  pl.BlockSpec((tm, tn), lambda i, j: (i, gblk + j)),
                  pl.BlockSpec((tm, tn), lambda i, j: (i, gblk + n // tn + j)),
                  wspec, wspec, wspec],
        out_specs=pl.BlockSpec((tm, tn), lambda i, j: (i, j)),
        out_shape=jax.ShapeDtypeStruct((m, n), BF16),
        compiler_params=_cparams(("parallel", "arbitrary")),
        name="mix",
    )(ga, yb, z, z, wa, wb, wp)


def _resmm_body(a_ref, w_ref, h_ref, o_ref):
    o_ref[...] = h_ref[...] + _dot(a_ref[...], w_ref[...].astype(BF16))


def _resmm(a, w, hres, layer, tm, tn):
    m, k = a.shape
    n = w.shape[-1]
    return pl.pallas_call(
        _resmm_body,
        grid=(m // tm, n // tn),
        in_specs=[pl.BlockSpec((tm, k), lambda i, j: (i, 0)),
                  pl.BlockSpec((None, k, tn), lambda i, j: (layer, 0, j)),
                  pl.BlockSpec((tm, tn), lambda i, j: (i, j))],
        out_specs=pl.BlockSpec((tm, tn), lambda i, j: (i, j)),
        out_shape=jax.ShapeDtypeStruct((m, n), F32),
        compiler_params=_cparams(("parallel", "arbitrary")),
        name="resmm",
    )(a, w, hres)


def _ple_body(x_ref, xc_ref, g_ref, p_ref, wp_ref, wg_ref, o_ref, xn_ref):
    @pl.when(pl.program_id(1) == 0)
    def _():
        xn_ref[...] = _rms_bf16(x_ref[...], g_ref[...])

    gate = jax.nn.sigmoid(_dot(xn_ref[...], wg_ref[...].astype(BF16)))
    o_ref[...] = xc_ref[...] + _dot(p_ref[...].astype(BF16), wp_ref[...].astype(BF16)) * gate


def _ple(x, g, p, wp, wg, layer, tm, tn):
    m, d = x.shape
    pd = p.shape[-1]
    return pl.pallas_call(
        _ple_body,
        grid=(m // tm, d // tn),
        in_specs=[pl.BlockSpec((tm, d), lambda i, j: (i, 0)),
                  pl.BlockSpec((tm, tn), lambda i, j: (i, j)),
                  pl.BlockSpec((None, 1, d), lambda i, j: (layer, 0, 0)),
                  pl.BlockSpec((None, tm, pd), lambda i, j: (layer, i, 0)),
                  pl.BlockSpec((None, pd, tn), lambda i, j: (layer, 0, j)),
                  pl.BlockSpec((None, d, tn), lambda i, j: (layer, 0, j))],
        out_specs=pl.BlockSpec((tm, tn), lambda i, j: (i, j)),
        out_shape=jax.ShapeDtypeStruct((m, d), F32),
        scratch_shapes=[pltpu.VMEM((tm, d), BF16)],
        compiler_params=_cparams(("parallel", "arbitrary")),
        name="ple",
    )(x, x, g, p, wp, wg)


def _final_norm_body(x_ref, g_ref, op_ref, os_ref, *, p_tiles):
    x = x_ref[...]
    inv = lax.rsqrt(jnp.mean(x * x, axis=-1, keepdims=True) + RMS_EPS)
    y = x * inv * g_ref[...]
    i = pl.program_id(0)

    @pl.when(i < p_tiles)
    def _():
        op_ref[...] = y

    @pl.when(i >= p_tiles)
    def _():
        os_ref[...] = y


def _final_norm(x, g, mp, tm):
    m, d = x.shape
    p_tiles = mp // tm
    return pl.pallas_call(
        functools.partial(_final_norm_body, p_tiles=p_tiles),
        grid=(m // tm,),
        in_specs=[pl.BlockSpec((tm, d), lambda i: (i, 0)), pl.BlockSpec((1, d), lambda i: (0, 0))],
        out_specs=[pl.BlockSpec((tm, d), lambda i: (jnp.minimum(i, p_tiles - 1), 0)),
                   pl.BlockSpec((tm, d), lambda i: (jnp.maximum(i - p_tiles, 0), 0))],
        out_shape=[jax.ShapeDtypeStruct((mp, d), F32), jax.ShapeDtypeStruct((m - mp, d), F32)],
        compiler_params=_cparams(("arbitrary",)),
        name="final_norm",
    )(x, g)


def _pick_tile(n, pref):
    t = min(pref, n)
    while n % t:
        t //= 2
    return t


def kernel(x_prompt, x_sample, state_ssm_re, state_ssm_im, state_pool, p_prompt, p_sample, g_ffn1, w_ffn1_gate, w_ffn1_up, w_ffn1_down, g_mix, w_in, ssm_a_re, ssm_a_im, ssm_log_dt, ssm_b_re, ssm_b_im, ssm_c_re, ssm_c_im, ssm_d, w_glu_a, w_glu_b, w_pool, pool_scale, w_pool_up, w_out, g_ffn2, w_ffn2_gate, w_ffn2_up, w_ffn2_down, g_ple, w_ple, w_ple_gate, g_final):
    nb, seq, d = x_prompt.shape
    ns, dseq, _ = x_sample.shape
    depth, n_groups, p_state = ssm_a_re.shape
    h = ssm_b_re.shape[-1]
    sw = n_groups * h
    pw = pool_scale.shape[-1]
    buf = state_pool.shape[2]
    chunk = SSM_CHUNK
    assert h == SSM_GROUP and dseq * 2 == chunk and seq % chunk == 0 and buf == max(POOL_WINDOWS) - 1
    n_chunks = seq // chunk
    assert n_chunks & (n_chunks - 1) == 0 and n_chunks <= 128
    mp, ms = nb * seq, ns * dseq
    m = mp + ms
    tm = _pick_tile(m, 1024)

    g3 = lambda a: a.reshape(depth, 1, -1)
    wg1, wu1, wd1 = w_ffn1_gate, w_ffn1_up, w_ffn1_down
    wg2, wu2, wd2 = w_ffn2_gate, w_ffn2_up, w_ffn2_down
    win, wga, wgb, wpu, wo = w_in, w_glu_a, w_glu_b, w_pool_up, w_out
    wpl, wpg, wpool = w_ple, w_ple_gate, w_pool
    gf1, gmx, gf2, gpl = g3(g_ffn1), g3(g_mix), g3(g_ffn2), g3(g_ple)
    pscale = g3(pool_scale)

    flat = lambda a: a.reshape((depth * n_groups,) + a.shape[2:])
    ops = _ssm_weights(flat(ssm_a_re), flat(ssm_a_im), flat(ssm_log_dt), flat(ssm_b_re), flat(ssm_b_im),
                       flat(ssm_c_re), flat(ssm_c_im), chunk)
    d_t = jnp.tile(flat(ssm_d), (1, chunk)).reshape(depth * n_groups, chunk * h, 1)

    hcur = jnp.concatenate([x_prompt.reshape(mp, d), x_sample.reshape(ms, d)], axis=0)
    p_all = jnp.concatenate([p_prompt.reshape(depth, mp, -1), p_sample.reshape(depth, ms, -1)], axis=1)
    tf = _pick_tile(w_ffn1_gate.shape[-1], 512)
    tn_in = _pick_tile(sw + pw, 1024)
    tn = _pick_tile(d, 512)
    gpb = min(8, n_groups)
    new_re_p, new_im_p, new_pool_p, new_re_s, new_im_s, new_pool_s = [], [], [], [], [], []
    for i in range(depth):
        h1 = _ffn(hcur, gf1, wg1, wu1, wd1, i, tm, tf)
        z = _inproj(h1, gmx, win, i, tm, tn_in, sw + pw)

        ua_p = z[:mp, :sw].reshape(nb, n_chunks, chunk, sw).transpose(3, 2, 0, 1).reshape(sw, mp)
        ua_s = z[mp:, :sw].reshape(ns, dseq, sw).transpose(2, 1, 0).reshape(sw, ms)
        yt_p, sr_p, si_p = _ssm(ua_p, d_t, ops, i, n_groups, chunk, n_chunks, gpb=gpb)
        s0 = (state_ssm_re[i].transpose(1, 2, 0), state_ssm_im[i].transpose(1, 2, 0))
        yt_s, sr_s, si_s = _ssm(ua_s, d_t, ops, i, n_groups, dseq, 1, s0=s0, gpb=gpb)
        ga = jnp.concatenate([
            yt_p.reshape(sw, chunk, nb, n_chunks).transpose(2, 3, 1, 0).reshape(mp, sw),
            yt_s.reshape(sw, dseq, ns).transpose(2, 1, 0).reshape(ms, sw)], axis=0)
        new_re_p.append(sr_p.transpose(2, 0, 1))
        new_im_p.append(si_p.transpose(2, 0, 1))
        new_re_s.append(sr_s.transpose(2, 0, 1))
        new_im_s.append(si_s.transpose(2, 0, 1))

        ub_s = z[mp:, sw:sw + pw].reshape(ns, dseq, pw)
        yb_p = _pool_seq(z, wpool, pscale, i, nb, seq, pw, sw // pw)
        yb_s = _pool_step(ub_s, state_pool, wpool, pscale, i)
        yb = jnp.concatenate([yb_p, yb_s.transpose(1, 0, 2).reshape(ms, pw)], axis=0)
        new_pool_p.append(z[:mp, sw:sw + pw].reshape(nb, seq, pw)[:, seq - buf:, :])
        new_pool_s.append(jnp.concatenate([state_pool[i], ub_s], axis=1)[:, dseq:, :])

        merged = _mix(ga, yb, z, wga, wgb, wpu, i, tm, tn, sw + pw)
        h2 = _resmm(merged, wo, h1, i, tm, tn)
        h3 = _ffn(h2, gf2, wg2, wu2, wd2, i, tm, tf)
        hcur = _ple(h3, gpl, p_all, wpl, wpg, i, tm, tn)

    y_p, y_s = _final_norm(hcur, g_final.reshape(1, d), mp, _pick_tile(math.gcd(mp, ms), 1024))
    return (y_p.reshape(nb, seq, d), y_s.reshape(ns, dseq, d),
            jnp.stack(new_re_p), jnp.stack(new_im_p), jnp.stack(new_pool_p),
            jnp.stack(new_re_s), jnp.stack(new_im_s), jnp.stack(new_pool_s))
```

```python
import functools
import math

import numpy as np
import jax
import jax.numpy as jnp
from jax import lax
from jax.experimental import pallas as pl
from jax.experimental.pallas import tpu as pltpu

F32 = jnp.float32
BF16 = jnp.bfloat16
RMS_EPS = 1e-6
POOL_WINDOWS = (2, 4, 8, 16)
SSM_GROUP = 16
SSM_CHUNK = 16
GELU_C = math.sqrt(2.0 / math.pi)
VMEM_LIMIT = 60 * 1024 * 1024
MXU_COLS = 256


def _cparams(sem):
    return pltpu.CompilerParams(dimension_semantics=sem, vmem_limit_bytes=VMEM_LIMIT)


def _rms_bf16(x, g):
    inv = lax.rsqrt(jnp.mean(x * x, axis=-1, keepdims=True) + RMS_EPS)
    return (x * inv * g).astype(BF16)


def _dot(a, b):
    return jnp.dot(a, b, preferred_element_type=F32)


def _col_chunks(width, chunk):
    chunk = min(chunk, width)
    return [slice(c0, c0 + chunk) for c0 in range(0, width, chunk)]


def _ffn_body(x_ref, g_ref, wg_ref, wu_ref, wd_ref, o_ref, xn_ref):
    j = pl.program_id(1)

    @pl.when(j == 0)
    def _():
        xn_ref[...] = _rms_bf16(x_ref[...], g_ref[...])
        o_ref[...] = jnp.zeros_like(o_ref)

    xn = xn_ref[...]
    mids = []
    for cs in _col_chunks(wg_ref.shape[1], MXU_COLS):
        a = _dot(xn, wg_ref[:, cs].astype(BF16))
        b = _dot(xn, wu_ref[:, cs].astype(BF16))
        mids.append((a * jax.nn.sigmoid(a) * b).astype(BF16))
    mid = jnp.concatenate(mids, axis=1)
    for cs in _col_chunks(o_ref.shape[1], 2 * MXU_COLS):
        o_ref[:, cs] += _dot(mid, wd_ref[:, cs].astype(BF16))

    @pl.when(j == pl.num_programs(1) - 1)
    def _():
        o_ref[...] = x_ref[...] + 0.5 * o_ref[...]


def _ffn(x, g, wg, wu, wd, layer, tm, tf):
    m, d = x.shape
    f = wg.shape[-1]
    return pl.pallas_call(
        _ffn_body,
        grid=(m // tm, f // tf),
        in_specs=[
            pl.BlockSpec((tm, d), lambda i, j: (i, 0), pipeline_mode=pl.Buffered(1)),
            pl.BlockSpec((None, 1, d), lambda i, j: (layer, 0, 0)),
            pl.BlockSpec((None, d, tf), lambda i, j: (layer, 0, j)),
            pl.BlockSpec((None, d, tf), lambda i, j: (layer, 0, j)),
            pl.BlockSpec((None, tf, d), lambda i, j: (layer, j, 0)),
        ],
        out_specs=pl.BlockSpec((tm, d), lambda i, j: (i, 0), pipeline_mode=pl.Buffered(1)),
        out_shape=jax.ShapeDtypeStruct((m, d), F32),
        scratch_shapes=[pltpu.VMEM((tm, d), BF16)],
        compiler_params=_cparams(("parallel", "arbitrary")),
        name="ffn",
    )(x, g, wg, wu, wd)


def _inproj_body(x_ref, g_ref, w_ref, o_ref, xn_ref, *, gate_from):
    j = pl.program_id(1)

    @pl.when(j == 0)
    def _():
        xn_ref[...] = _rms_bf16(x_ref[...], g_ref[...])

    is_gate = j >= gate_from
    xn = xn_ref[...]
    for cs in _col_chunks(o_ref.shape[1], MXU_COLS):
        r = _dot(xn, w_ref[j, :, cs])
        o_ref[:, cs] = jnp.where(is_gate, jax.nn.sigmoid(r), r)


def _inproj(x, g, w, layer, tm, n_plain):
    m, d = x.shape
    _, nj, _, tn = w.shape
    return pl.pallas_call(
        functools.partial(_inproj_body, gate_from=n_plain // tn),
        grid=(m // tm, nj),
        in_specs=[
            pl.BlockSpec((tm, d), lambda i, j: (i, 0)),
            pl.BlockSpec((None, 1, d), lambda i, j: (layer, 0, 0)),
            pl.BlockSpec((None, nj, d, tn), lambda i, j: (layer, 0, 0, 0), pipeline_mode=pl.Buffered(1)),
        ],
        out_specs=pl.BlockSpec((tm, tn), lambda i, j: (i, j)),
        out_shape=jax.ShapeDtypeStruct((m, nj * tn), F32),
        scratch_shapes=[pltpu.VMEM((tm, d), BF16)],
        compiler_params=_cparams(("parallel", "arbitrary")),
        name="inproj",
    )(x, g, w)


def _cmul(ar, ai, br, bi):
    return ar * br - ai * bi, ar * bi + ai * br


def _ssm_weights_body(ar_ref, ai_ref, ldt_ref, btr_ref, bti_ref, ctr_ref, cti_ref, e_col, e_row,
                      tt_o, wor_o, woi_o, wstr_o, wsti_o, scr_o, sci_o, *, chunk, gblk):
    h = SSM_GROUP
    causal = e_col[...] >= e_row[...]
    nt = (((1,), (1,)), ((), ()))
    hi = lax.Precision.HIGHEST

    def rows(pows):
        width = pows[0][0].shape[1]
        return tuple(jnp.concatenate([jnp.broadcast_to(x[k], (h, width)) for x in pows], axis=0)
                     for k in (0, 1))

    for gl in range(gblk):
        dt = jnp.exp(ldt_ref[gl])
        a_re, a_im = ar_ref[gl], ai_ref[gl]
        mag = jnp.exp(a_re * dt)
        ang = a_im * dt
        lr, li = mag * jnp.cos(ang), mag * jnp.sin(ang)
        den = a_re * a_re + a_im * a_im
        num_re = lr - 1.0
        k_re = (num_re * a_re + li * a_im) / den
        k_im = (li * a_re - num_re * a_im) / den
        inv = 1.0 / (lr * lr + li * li)
        nr, ni = lr * inv, -li * inv
        pw = [(jnp.ones_like(lr), jnp.zeros_like(lr))]
        npw = list(pw)
        for _ in range(chunk):
            pw.append(_cmul(*pw[-1], lr, li))
            npw.append(_cmul(*npw[-1], nr, ni))
        kb = _cmul(k_re, k_im, btr_ref[gl], bti_ref[gl])
        c = (ctr_ref[gl], cti_ref[gl])

        l_re, l_im = _cmul(*c, *rows(pw[:chunk]))
        r_re, r_im = _cmul(*rows(npw[:chunk]), *kb)
        kmat = (lax.dot_general(l_re, r_re, nt, precision=hi, preferred_element_type=F32)
                - lax.dot_general(l_im, r_im, nt, precision=hi, preferred_element_type=F32))
        tt_o[gl] = jnp.where(causal, kmat, 0.0).astype(BF16)

        e_re, e_im = _cmul(*c, *rows(pw[1:chunk + 1]))
        wor_o[gl] = e_re.astype(BF16)
        woi_o[gl] = (-e_im).astype(BF16)

        s_re, s_im = _cmul(*rows(pw[chunk - 1::-1]), *kb)
        wstr_o[gl] = s_re.astype(BF16)
        wsti_o[gl] = s_im.astype(BF16)

        sc = [pw[chunk]]
        for _ in range(6):
            sc.append(_cmul(*sc[-1], *sc[-1]))
        sc.append(pw[chunk // 2])
        scr_o[gl] = jnp.concatenate([x[0] for x in sc], axis=0)
        sci_o[gl] = jnp.concatenate([x[1] for x in sc], axis=0)


def _ssm_weights(a_re, a_im, log_dt, b_re, b_im, c_re, c_im, chunk):
    dg, p = a_re.shape
    h = b_re.shape[-1]
    th = chunk * h
    tau = np.repeat(np.arange(chunk, dtype=np.float32), h)
    e_col = jnp.asarray(tau.reshape(th, 1))
    e_row = jnp.asarray(tau.reshape(1, th))
    gblk = min(16, dg)
    row = lambda x: x.reshape(dg, 1, p)
    bt_t = lambda x: jnp.tile(jnp.swapaxes(x, 1, 2), (1, chunk, 1))
    c_t = lambda x: jnp.tile(x, (1, chunk, 1))
    per_g = lambda *s: pl.BlockSpec((gblk,) + s, lambda g: (g,) + (0,) * len(s))
    const = lambda *s: pl.BlockSpec(s, lambda g: (0,) * len(s))
    tt, wor, woi, wstr, wsti, scr, sci = pl.pallas_call(
        functools.partial(_ssm_weights_body, chunk=chunk, gblk=gblk),
        grid=(dg // gblk,),
        in_specs=[per_g(1, p), per_g(1, p), per_g(1, 1),
                  per_g(th, p), per_g(th, p), per_g(th, p), per_g(th, p),
                  const(th, 1), const(1, th)],
        out_specs=[per_g(th, th), per_g(th, p), per_g(th, p), per_g(th, p), per_g(th, p),
                   per_g(8, p), per_g(8, p)],
        out_shape=[jax.ShapeDtypeStruct((dg, th, th), BF16),
                   jax.ShapeDtypeStruct((dg, th, p), BF16),
                   jax.ShapeDtypeStruct((dg, th, p), BF16),
                   jax.ShapeDtypeStruct((dg, th, p), BF16),
                   jax.ShapeDtypeStruct((dg, th, p), BF16),
                   jax.ShapeDtypeStruct((dg, 8, p), F32),
                   jax.ShapeDtypeStruct((dg, 8, p), F32)],
        compiler_params=_cparams(("parallel",)),
        name="ssm_weights",
    )(row(a_re), row(a_im), log_dt.reshape(dg, 1, 1),
      bt_t(b_re), bt_t(b_im), c_t(c_re), c_t(c_im), e_col, e_row)
    sw = lambda x: jnp.swapaxes(x, 1, 2)
    return tt, wor, woi, sw(wstr), sw(wsti), sw(scr), sw(sci)


def _ssm_body(*refs, steps, chunks, gpb, has_init):
    if has_init:
        (ut_ref, d_ref, tt_ref, wor_ref, woi_ref, wsr_ref, wsi_ref, scr_ref, sci_ref,
         s0r_ref, s0i_ref, yt_ref, sfr_ref, sfi_ref) = refs
    else:
        (ut_ref, d_ref, tt_ref, wor_ref, woi_ref, wsr_ref, wsi_ref, scr_ref, sci_ref,
         yt_ref, sfr_ref, sfi_ref) = refs
    h = SSM_GROUP
    th = steps * h
    w = ut_ref.shape[1] // steps
    p = scr_ref.shape[1]
    ws_off = wsr_ref.shape[2] - th
    for gl in range(gpb):
        rows = slice(gl * h, (gl + 1) * h)
        u = jnp.concatenate([ut_ref[rows, t * w:(t + 1) * w] for t in range(steps)], axis=0)
        ub = u.astype(BF16)
        y = _dot(tt_ref[gl, :th, :th], ub)
        xr = _dot(wsr_ref[gl, :, ws_off:], ub)
        xi = _dot(wsi_ref[gl, :, ws_off:], ub)
        if chunks > 1:
            c_idx = lax.broadcasted_iota(jnp.int32, (p, w), 1) & (chunks - 1)
            sr, si = xr, xi
            k = 0
            while (1 << k) < chunks:
                sh = 1 << k
                rr = pltpu.roll(sr, sh, axis=1)
                ri = pltpu.roll(si, sh, axis=1)
                mr = scr_ref[gl, :, k:k + 1]
                mi = sci_ref[gl, :, k:k + 1]
                keep = c_idx >= sh
                sr = sr + jnp.where(keep, mr * rr - mi * ri, 0.0)
                si = si + jnp.where(keep, mr * ri + mi * rr, 0.0)
                k += 1
            first = c_idx >= 1
            pr = jnp.where(first, pltpu.roll(sr, 1, axis=1), 0.0)
            pi = jnp.where(first, pltpu.roll(si, 1, axis=1), 0.0)
            for n in range(w // chunks):
                last = n * chunks + chunks - 1
                sfr_ref[gl, :, n:n + 1] = sr[:, last:last + 1]
                sfi_ref[gl, :, n:n + 1] = si[:, last:last + 1]
        else:
            pr, pi = s0r_ref[gl], s0i_ref[gl]
            lr = scr_ref[gl, :, 7:8]
            li = sci_ref[gl, :, 7:8]
            sfr_ref[gl] = lr * pr - li * pi + xr
            sfi_ref[gl] = lr * pi + li * pr + xi
        y = y + _dot(wor_ref[gl, :th, :], pr.astype(BF16)) + _dot(woi_ref[gl, :th, :], pi.astype(BF16))
        y = y + d_ref[gl, :th, :] * u
        y = 0.5 * y * (1.0 + jnp.tanh(GELU_C * (y + 0.044715 * (y * y * y))))
        yb = y.astype(BF16)
        for t in range(steps):
            yt_ref[rows, t * w:(t + 1) * w] = yb[t * h:(t + 1) * h, :]


def _ssm(ut, d_t, ops, layer, n_groups, steps, chunks, s0=None, gpb=8):
    tt, wor, woi, wsr, wsi, scr, sci = ops
    c, lanes = ut.shape
    h = SSM_GROUP
    w = lanes // steps
    nseq = w // chunks
    p = scr.shape[1]
    thf = tt.shape[1]
    base = layer * (n_groups // gpb)
    wblk = lambda *s: pl.BlockSpec((gpb,) + s, lambda g: (base + g,) + (0,) * len(s))
    in_specs = [pl.BlockSpec((gpb * h, lanes), lambda g: (g, 0)),
                wblk(thf, 1), wblk(thf, thf), wblk(thf, p), wblk(thf, p), wblk(p, thf), wblk(p, thf),
                wblk(p, scr.shape[2]), wblk(p, scr.shape[2])]
    args = [ut, d_t, tt, wor, woi, wsr, wsi, scr, sci]
    if s0 is not None:
        in_specs += [pl.BlockSpec((gpb, p, w), lambda g: (g, 0, 0))] * 2
        args += list(s0)
    sf_spec = pl.BlockSpec((gpb, p, nseq), lambda g: (g, 0, 0))
    return pl.pallas_call(
        functools.partial(_ssm_body, steps=steps, chunks=chunks, gpb=gpb, has_init=s0 is not None),
        grid=(n_groups // gpb,),
        in_specs=in_specs,
        out_specs=[pl.BlockSpec((gpb * h, lanes), lambda g: (g, 0)), sf_spec, sf_spec],
        out_shape=[jax.ShapeDtypeStruct((c, lanes), BF16),
                   jax.ShapeDtypeStruct((n_groups, p, nseq), F32),
                   jax.ShapeDtypeStruct((n_groups, p, nseq), F32)],
        compiler_params=_cparams(("parallel",)),
        name="ssm_chunks" if s0 is None else "ssm_step",
    )(*args)


def _pool_seq_body(u_ref, w_ref, sc_ref, o_ref, z_ref):
    l, c = u_ref.shape
    pad = z_ref.shape[0] - l
    cg = c // len(POOL_WINDOWS)
    z_ref[:pad, :] = jnp.zeros((pad, c), F32)
    z_ref[pad:, :] = u_ref[...]
    pos1 = (lax.broadcasted_iota(jnp.int32, (l, 1), 0) + 1).astype(F32)
    for gi, win in enumerate(POOL_WINDOWS):
        cols = slice(gi * cg, (gi + 1) * cg)
        cur = z_ref[pad:, cols]
        tot = cur
        for k in range(1, win):
            tot = tot + z_ref[pad - k:pad - k + l, cols]
        mean = tot / jnp.minimum(pos1, float(win))
        mixed = _dot((mean - cur).astype(BF16), w_ref[gi].astype(BF16))
        o_ref[:, cols] = (mixed * sc_ref[:, cols]).astype(BF16)


def _pool_seq(z, w_pool, scale, layer, nseq, seqlen, width, col_blk):
    return pl.pallas_call(
        _pool_seq_body,
        grid=(nseq,),
        in_specs=[pl.BlockSpec((seqlen, width), lambda n: (n, col_blk)),
                  pl.BlockSpec((None,) + w_pool.shape[1:], lambda n: (layer, 0, 0, 0)),
                  pl.BlockSpec((None, 1, width), lambda n: (layer, 0, 0))],
        out_specs=pl.BlockSpec((seqlen, width), lambda n: (n, 0)),
        out_shape=jax.ShapeDtypeStruct((nseq * seqlen, width), BF16),
        scratch_shapes=[pltpu.VMEM((seqlen + 16, width), F32)],
        compiler_params=_cparams(("parallel",)),
        name="pool_seq",
    )(z, w_pool, scale)


def _pool_step_body(u_ref, prev_ref, w_ref, sc_ref, o_ref):
    n, steps, c = u_ref.shape
    buf = prev_ref.shape[1]
    cg = c // len(POOL_WINDOWS)

    def row(j, cols):
        return prev_ref[:, j, cols] if j < buf else u_ref[:, j - buf, cols]

    for gi, win in enumerate(POOL_WINDOWS):
        cols = slice(gi * cg, (gi + 1) * cg)
        for t in range(steps):
            cur = row(buf + t, cols)
            tot = cur
            for k in range(1, win):
                tot = tot + row(buf + t - k, cols)
            mixed = _dot((tot / float(win) - cur).astype(BF16), w_ref[gi].astype(BF16))
            o_ref[t, :, cols] = (mixed * sc_ref[:, cols]).astype(BF16)


def _pool_step(u, prev, w_pool, scale, layer):
    n, steps, c = u.shape
    return pl.pallas_call(
        _pool_step_body,
        grid=(1,),
        in_specs=[pl.BlockSpec(u.shape, lambda i: (0, 0, 0)),
                  pl.BlockSpec((None,) + prev.shape[1:], lambda i: (layer, 0, 0, 0)),
                  pl.BlockSpec((None,) + w_pool.shape[1:], lambda i: (layer, 0, 0, 0)),
                  pl.BlockSpec((None, 1, c), lambda i: (layer, 0, 0))],
        out_specs=pl.BlockSpec((steps, n, c), lambda i: (0, 0, 0)),
        out_shape=jax.ShapeDtypeStruct((steps, n, c), BF16),
        compiler_params=_cparams(("arbitrary",)),
        name="pool_step",
    )(u, prev, w_pool, scale)


def _mix_body(ga_ref, yb_ref, sa_ref, sb_ref, wa_ref, wb_ref, wp_ref, o_ref):
    ga = ga_ref[...]
    yb = yb_ref[...]
    for cs in _col_chunks(o_ref.shape[1], MXU_COLS):
        br_a = _dot(ga, wa_ref[:, cs].astype(BF16)) * jax.nn.sigmoid(_dot(ga, wb_ref[:, cs].astype(BF16)))
        br_b = _dot(yb, wp_ref[:, cs].astype(BF16))
        o_ref[:, cs] = (sa_ref[:, cs] * br_a + sb_ref[:, cs] * br_b).astype(BF16)


def _resident(shape, layer):
    return pl.BlockSpec((None,) + shape, lambda i: (layer,) + (0,) * len(shape), pipeline_mode=pl.Buffered(1))


def _mix(ga, yb, z, wa, wb, wp, layer, tm, gate_col):
    m, k = ga.shape
    n = wa.shape[-1]
    gblk = gate_col // n
    return pl.pallas_call(
        _mix_body,
        grid=(m // tm,),
        in_specs=[pl.BlockSpec((tm, k), lambda i: (i, 0)),
                  pl.BlockSpec((tm, k), lambda i: (i, 0)),
                  pl.BlockSpec((tm, n), lambda i: (i, gblk)),
                  pl.BlockSpec((tm, n), lambda i: (i, gblk + 1)),
                  _resident((k, n), layer), _resident((k, n), layer), _resident((k, n), layer)],
        out_specs=pl.BlockSpec((tm, n), lambda i: (i, 0)),
        out_shape=jax.ShapeDtypeStruct((m, n), BF16),
        compiler_params=_cparams(("parallel",)),
        name="mix",
    )(ga, yb, z, z, wa, wb, wp)


def _resmm_body(a_ref, w_ref, h_ref, o_ref):
    a = a_ref[...]
    for cs in _col_chunks(o_ref.shape[1], MXU_COLS):
        o_ref[:, cs] = h_ref[:, cs] + _dot(a, w_ref[:, cs].astype(BF16))


def _resmm(a, w, hres, layer, tm):
    m, k = a.shape
    n = w.shape[-1]
    return pl.pallas_call(
        _resmm_body,
        grid=(m // tm,),
        in_specs=[pl.BlockSpec((tm, k), lambda i: (i, 0)),
                  _resident((k, n), layer),
                  pl.BlockSpec((tm, n), lambda i: (i, 0))],
        out_specs=pl.BlockSpec((tm, n), lambda i: (i, 0)),
        out_shape=jax.ShapeDtypeStruct((m, n), F32),
        compiler_params=_cparams(("parallel",)),
        name="resmm",
    )(a, w, hres)


def _ple_update(x_ref, g_ref, p_ref, wp_ref, wg_ref, dst_ref):
    xn = _rms_bf16(x_ref[...], g_ref[...])
    pb = p_ref[...].astype(BF16)
    for cs in _col_chunks(dst_ref.shape[1], MXU_COLS):
        gate = jax.nn.sigmoid(_dot(xn, wg_ref[:, cs].astype(BF16)))
        dst_ref[:, cs] = x_ref[:, cs] + _dot(pb, wp_ref[:, cs].astype(BF16)) * gate


def _ple_body(x_ref, g_ref, p_ref, wp_ref, wg_ref, o_ref):
    _ple_update(x_ref, g_ref, p_ref, wp_ref, wg_ref, o_ref)


def _ple_final_body(x_ref, g_ref, p_ref, wp_ref, wg_ref, gf_ref, op_ref, os_ref, h_ref, *, p_tiles):
    _ple_update(x_ref, g_ref, p_ref, wp_ref, wg_ref, h_ref)
    h = h_ref[...]
    y = h * lax.rsqrt(jnp.mean(h * h, axis=-1, keepdims=True) + RMS_EPS) * gf_ref[...]
    i = pl.program_id(0)

    @pl.when(i < p_tiles)
    def _():
        op_ref[...] = y

    @pl.when(i >= p_tiles)
    def _():
        os_ref[...] = y


def _ple(x, g, p, wp, wg, layer, tm, final=None):
    m, d = x.shape
    pd = p.shape[-1]
    in_specs = [pl.BlockSpec((tm, d), lambda i: (i, 0)),
                pl.BlockSpec((None, 1, d), lambda i: (layer, 0, 0)),
                pl.BlockSpec((None, tm, pd), lambda i: (layer, i, 0)),
                _resident((pd, d), layer), _resident((d, d), layer)]
    if final is None:
        return pl.pallas_call(
            _ple_body, grid=(m // tm,), in_specs=in_specs,
            out_specs=pl.BlockSpec((tm, d), lambda i: (i, 0)),
            out_shape=jax.ShapeDtypeStruct((m, d), F32),
            compiler_params=_cparams(("parallel",)),
            name="ple",
        )(x, g, p, wp, wg)
    g_final, mp = final
    p_tiles = mp // tm
    return pl.pallas_call(
        functools.partial(_ple_final_body, p_tiles=p_tiles),
        grid=(m // tm,),
        in_specs=in_specs + [pl.BlockSpec((1, d), lambda i: (0, 0))],
        out_specs=[pl.BlockSpec((tm, d), lambda i: (jnp.minimum(i, p_tiles - 1), 0)),
                   pl.BlockSpec((tm, d), lambda i: (jnp.maximum(i - p_tiles, 0), 0))],
        out_shape=[jax.ShapeDtypeStruct((mp, d), F32), jax.ShapeDtypeStruct((m - mp, d), F32)],
        scratch_shapes=[pltpu.VMEM((tm, d), F32)],
        compiler_params=_cparams(("arbitrary",)),
        name="ple_final",
    )(x, g, p, wp, wg, g_final)


def _pick_tile(n, pref):
    t = min(pref, n)
    while n % t:
        t //= 2
    return t


def kernel(x_prompt, x_sample, state_ssm_re, state_ssm_im, state_pool, p_prompt, p_sample, g_ffn1, w_ffn1_gate, w_ffn1_up, w_ffn1_down, g_mix, w_in, ssm_a_re, ssm_a_im, ssm_log_dt, ssm_b_re, ssm_b_im, ssm_c_re, ssm_c_im, ssm_d, w_glu_a, w_glu_b, w_pool, pool_scale, w_pool_up, w_out, g_ffn2, w_ffn2_gate, w_ffn2_up, w_ffn2_down, g_ple, w_ple, w_ple_gate, g_final):
    nb, seq, d = x_prompt.shape
    ns, dseq, _ = x_sample.shape
    depth, n_groups, p_state = ssm_a_re.shape
    h = ssm_b_re.shape[-1]
    sw = n_groups * h
    pw = pool_scale.shape[-1]
    buf = state_pool.shape[2]
    chunk = SSM_CHUNK
    assert h == SSM_GROUP and dseq * 2 == chunk and seq % chunk == 0 and buf == max(POOL_WINDOWS) - 1
    n_chunks = seq // chunk
    assert n_chunks & (n_chunks - 1) == 0 and n_chunks <= 128
    mp, ms = nb * seq, ns * dseq
    m = mp + ms
    tm = _pick_tile(m, 1024)

    g3 = lambda a: a.reshape(depth, 1, -1)
    wg1, wu1, wd1 = w_ffn1_gate, w_ffn1_up, w_ffn1_down
    wg2, wu2, wd2 = w_ffn2_gate, w_ffn2_up, w_ffn2_down
    wga, wgb, wpu, wo = w_glu_a, w_glu_b, w_pool_up, w_out
    wpl, wpg, wpool = w_ple, w_ple_gate, w_pool
    tn_in = _pick_tile(sw + pw, 2048)
    win = w_in.astype(BF16).reshape(depth, d, -1, tn_in).transpose(0, 2, 1, 3)
    gf1, gmx, gf2, gpl = g3(g_ffn1), g3(g_mix), g3(g_ffn2), g3(g_ple)
    pscale = g3(pool_scale)

    flat = lambda a: a.reshape((depth * n_groups,) + a.shape[2:])
    ops = _ssm_weights(flat(ssm_a_re), flat(ssm_a_im), flat(ssm_log_dt), flat(ssm_b_re), flat(ssm_b_im),
                       flat(ssm_c_re), flat(ssm_c_im), chunk)
    d_t = jnp.tile(flat(ssm_d), (1, chunk)).reshape(depth * n_groups, chunk * h, 1)

    hcur = jnp.concatenate([x_prompt.reshape(mp, d), x_sample.reshape(ms, d)], axis=0)
    p_all = jnp.concatenate([p_prompt.reshape(depth, mp, -1), p_sample.reshape(depth, ms, -1)], axis=1)
    tf = _pick_tile(w_ffn1_gate.shape[-1], 512)
    tm2 = _pick_tile(math.gcd(mp, ms), 512)
    gpb = min(8, n_groups)
    new_re_p, new_im_p, new_pool_p, new_re_s, new_im_s, new_pool_s = [], [], [], [], [], []
    for i in range(depth):
        h1 = _ffn(hcur, gf1, wg1, wu1, wd1, i, tm, tf)
        z = _inproj(h1, gmx, win, i, tm2, sw + pw)

        ua_p = z[:mp, :sw].reshape(nb, n_chunks, chunk, sw).transpose(3, 2, 0, 1).reshape(sw, mp)
        ua_s = z[mp:, :sw].reshape(ns, dseq, sw).transpose(2, 1, 0).reshape(sw, ms)
        yt_p, sr_p, si_p = _ssm(ua_p, d_t, ops, i, n_groups, chunk, n_chunks, gpb=gpb)
        s0 = (state_ssm_re[i].transpose(1, 2, 0), state_ssm_im[i].transpose(1, 2, 0))
        yt_s, sr_s, si_s = _ssm(ua_s, d_t, ops, i, n_groups, dseq, 1, s0=s0, gpb=gpb)
        ga = jnp.concatenate([
            yt_p.reshape(sw, chunk, nb, n_chunks).transpose(2, 3, 1, 0).reshape(mp, sw),
            yt_s.reshape(sw, dseq, ns).transpose(2, 1, 0).reshape(ms, sw)], axis=0)
        new_re_p.append(sr_p.transpose(2, 0, 1))
        new_im_p.append(si_p.transpose(2, 0, 1))
        new_re_s.append(sr_s.transpose(2, 0, 1))
        new_im_s.append(si_s.transpose(2, 0, 1))

        ub_s = z[mp:, sw:sw + pw].reshape(ns, dseq, pw)
        yb_p = _pool_seq(z, wpool, pscale, i, nb, seq, pw, sw // pw)
        yb_s = _pool_step(ub_s, state_pool, wpool, pscale, i)
        yb = jnp.concatenate([yb_p, yb_s.transpose(1, 0, 2).reshape(ms, pw)], axis=0)
        new_pool_p.append(z[:mp, sw:sw + pw].reshape(nb, seq, pw)[:, seq - buf:, :])
        new_pool_s.append(jnp.concatenate([state_pool[i], ub_s], axis=1)[:, dseq:, :])

        merged = _mix(ga, yb, z, wga, wgb, wpu, i, tm2, sw + pw)
        h2 = _resmm(merged, wo, h1, i, tm2)
        h3 = _ffn(h2, gf2, wg2, wu2, wd2, i, tm, tf)
        if i + 1 < depth:
            hcur = _ple(h3, gpl, p_all, wpl, wpg, i, tm2)
        else:
            y_p, y_s = _ple(h3, gpl, p_all, wpl, wpg, i, tm2, final=(g_final.reshape(1, d), mp))

    return (y_p.reshape(nb, seq, d), y_s.reshape(ns, dseq, d),
            jnp.stack(new_re_p), jnp.stack(new_im_p), jnp.stack(new_pool_p),
            jnp.stack(new_re_s), jnp.stack(new_im_s), jnp.stack(new_pool_s))
```

```python
import functools
import math

import numpy as np
import jax
import jax.numpy as jnp
from jax import lax
from jax.experimental import pallas as pl
from jax.experimental.pallas import tpu as pltpu

F32 = jnp.float32
BF16 = jnp.bfloat16
RMS_EPS = 1e-6
POOL_WINDOWS = (2, 4, 8, 16)
SSM_GROUP = 16
SSM_CHUNK = 16
GELU_C = math.sqrt(2.0 / math.pi)
VMEM_LIMIT = 60 * 1024 * 1024
MXU_COLS = 256


def _cparams(sem):
    return pltpu.CompilerParams(dimension_semantics=sem, vmem_limit_bytes=VMEM_LIMIT)


def _rms_bf16(x, g):
    inv = lax.rsqrt(jnp.mean(x * x, axis=-1, keepdims=True) + RMS_EPS)
    return (x * inv * g).astype(BF16)


def _dot(a, b):
    return jnp.dot(a, b, preferred_element_type=F32)


def _col_chunks(width, chunk):
    chunk = min(chunk, width)
    return [slice(c0, c0 + chunk) for c0 in range(0, width, chunk)]


def _ffn_body(x_ref, g_ref, wg_ref, wu_ref, wd_ref, o_ref, xn_ref):
    j = pl.program_id(1)

    @pl.when(j == 0)
    def _():
        xn_ref[...] = _rms_bf16(x_ref[...], g_ref[...])
        o_ref[...] = jnp.zeros_like(o_ref)

    xn = xn_ref[...]
    mids = []
    for cs in _col_chunks(wg_ref.shape[1], MXU_COLS):
        a = _dot(xn, wg_ref[:, cs].astype(BF16))
        b = _dot(xn, wu_ref[:, cs].astype(BF16))
        mids.append((a * jax.nn.sigmoid(a) * b).astype(BF16))
    mid = jnp.concatenate(mids, axis=1)
    for cs in _col_chunks(o_ref.shape[1], 2 * MXU_COLS):
        o_ref[:, cs] += _dot(mid, wd_ref[:, cs].astype(BF16))

    @pl.when(j == pl.num_programs(1) - 1)
    def _():
        o_ref[...] = x_ref[...] + 0.5 * o_ref[...]


def _ffn(x, g, wg, wu, wd, layer, tm, tf):
    m, d = x.shape
    f = wg.shape[-1]
    return pl.pallas_call(
        _ffn_body,
        grid=(m // tm, f // tf),
        in_specs=[
            pl.BlockSpec((tm, d), lambda i, j: (i, 0), pipeline_mode=pl.Buffered(1)),
            pl.BlockSpec((None, 1, d), lambda i, j: (layer, 0, 0)),
            pl.BlockSpec((None, d, tf), lambda i, j: (layer, 0, j)),
            pl.BlockSpec((None, d, tf), lambda i, j: (layer, 0, j)),
            pl.BlockSpec((None, tf, d), lambda i, j: (layer, j, 0)),
        ],
        out_specs=pl.BlockSpec((tm, d), lambda i, j: (i, 0), pipeline_mode=pl.Buffered(1)),
        out_shape=jax.ShapeDtypeStruct((m, d), F32),
        scratch_shapes=[pltpu.VMEM((tm, d), BF16)],
        compiler_params=_cparams(("parallel", "arbitrary")),
        name="ffn",
    )(x, g, wg, wu, wd)


def _inproj_body(x_ref, g_ref, w_ref, o_ref, *, gate_from):
    is_gate = pl.program_id(0) >= gate_from
    xn = _rms_bf16(x_ref[...], g_ref[...])
    for cs in _col_chunks(o_ref.shape[1], MXU_COLS):
        r = _dot(xn, w_ref[:, cs].astype(BF16))
        o_ref[:, cs] = jnp.where(is_gate, jax.nn.sigmoid(r), r)


def _inproj(x, g, w, layer, tm, tn, n_plain):
    m, d = x.shape
    n = w.shape[-1]
    return pl.pallas_call(
        functools.partial(_inproj_body, gate_from=n_plain // tn),
        grid=(n // tn, m // tm),
        in_specs=[
            pl.BlockSpec((tm, d), lambda j, i: (i, 0)),
            pl.BlockSpec((None, 1, d), lambda j, i: (layer, 0, 0)),
            pl.BlockSpec((None, d, tn), lambda j, i: (layer, 0, j), pipeline_mode=pl.Buffered(1)),
        ],
        out_specs=pl.BlockSpec((tm, tn), lambda j, i: (i, j)),
        out_shape=jax.ShapeDtypeStruct((m, n), F32),
        compiler_params=_cparams(("arbitrary", "arbitrary")),
        name="inproj",
    )(x, g, w)


def _cmul(ar, ai, br, bi):
    return ar * br - ai * bi, ar * bi + ai * br


def _ssm_weights_body(ar_ref, ai_ref, ldt_ref, btr_ref, bti_ref, ctr_ref, cti_ref, e_col, e_row,
                      tt_o, wor_o, woi_o, wstr_o, wsti_o, scr_o, sci_o, *, chunk, gblk):
    h = SSM_GROUP
    causal = e_col[...] >= e_row[...]
    nt = (((1,), (1,)), ((), ()))
    hi = lax.Precision.HIGHEST

    def rows(pows):
        width = pows[0][0].shape[1]
        return tuple(jnp.concatenate([jnp.broadcast_to(x[k], (h, width)) for x in pows], axis=0)
                     for k in (0, 1))

    for gl in range(gblk):
        dt = jnp.exp(ldt_ref[gl])
        a_re, a_im = ar_ref[gl], ai_ref[gl]
        mag = jnp.exp(a_re * dt)
        ang = a_im * dt
        lr, li = mag * jnp.cos(ang), mag * jnp.sin(ang)
        den = a_re * a_re + a_im * a_im
        num_re = lr - 1.0
        k_re = (num_re * a_re + li * a_im) / den
        k_im = (li * a_re - num_re * a_im) / den
        inv = 1.0 / (lr * lr + li * li)
        nr, ni = lr * inv, -li * inv
        pw = [(jnp.ones_like(lr), jnp.zeros_like(lr))]
        npw = list(pw)
        for _ in range(chunk):
            pw.append(_cmul(*pw[-1], lr, li))
            npw.append(_cmul(*npw[-1], nr, ni))
        kb = _cmul(k_re, k_im, btr_ref[gl], bti_ref[gl])
        c = (ctr_ref[gl], cti_ref[gl])

        l_re, l_im = _cmul(*c, *rows(pw[:chunk]))
        r_re, r_im = _cmul(*rows(npw[:chunk]), *kb)
        kmat = (lax.dot_general(l_re, r_re, nt, precision=hi, preferred_element_type=F32)
                - lax.dot_general(l_im, r_im, nt, precision=hi, preferred_element_type=F32))
        tt_o[gl] = jnp.where(causal, kmat, 0.0).astype(BF16)

        e_re, e_im = _cmul(*c, *rows(pw[1:chunk + 1]))
        wor_o[gl] = e_re.astype(BF16)
        woi_o[gl] = (-e_im).astype(BF16)

        s_re, s_im = _cmul(*rows(pw[chunk - 1::-1]), *kb)
        wstr_o[gl] = s_re.astype(BF16)
        wsti_o[gl] = s_im.astype(BF16)

        sc = [pw[chunk]]
        for _ in range(6):
            sc.append(_cmul(*sc[-1], *sc[-1]))
        sc.append(pw[chunk // 2])
        scr_o[gl] = jnp.concatenate([x[0] for x in sc], axis=0)
        sci_o[gl] = jnp.concatenate([x[1] for x in sc], axis=0)


def _ssm_weights(a_re, a_im, log_dt, b_re, b_im, c_re, c_im, chunk):
    dg, p = a_re.shape
    h = b_re.shape[-1]
    th = chunk * h
    tau = np.repeat(np.arange(chunk, dtype=np.float32), h)
    e_col = jnp.asarray(tau.reshape(th, 1))
    e_row = jnp.asarray(tau.reshape(1, th))
    gblk = min(16, dg)
    row = lambda x: x.reshape(dg, 1, p)
    bt_t = lambda x: jnp.tile(jnp.swapaxes(x, 1, 2), (1, chunk, 1))
    c_t = lambda x: jnp.tile(x, (1, chunk, 1))
    per_g = lambda *s: pl.BlockSpec((gblk,) + s, lambda g: (g,) + (0,) * len(s))
    const = lambda *s: pl.BlockSpec(s, lambda g: (0,) * len(s))
    tt, wor, woi, wstr, wsti, scr, sci = pl.pallas_call(
        functools.partial(_ssm_weights_body, chunk=chunk, gblk=gblk),
        grid=(dg // gblk,),
        in_specs=[per_g(1, p), per_g(1, p), per_g(1, 1),
                  per_g(th, p), per_g(th, p), per_g(th, p), per_g(th, p),
                  const(th, 1), const(1, th)],
        out_specs=[per_g(th, th), per_g(th, p), per_g(th, p), per_g(th, p), per_g(th, p),
                   per_g(8, p), per_g(8, p)],
        out_shape=[jax.ShapeDtypeStruct((dg, th, th), BF16),
                   jax.ShapeDtypeStruct((dg, th, p), BF16),
                   jax.ShapeDtypeStruct((dg, th, p), BF16),
                   jax.ShapeDtypeStruct((dg, th, p), BF16),
                   jax.ShapeDtypeStruct((dg, th, p), BF16),
                   jax.ShapeDtypeStruct((dg, 8, p), F32),
                   jax.ShapeDtypeStruct((dg, 8, p), F32)],
        compiler_params=_cparams(("parallel",)),
        name="ssm_weights",
    )(row(a_re), row(a_im), log_dt.reshape(dg, 1, 1),
      bt_t(b_re), bt_t(b_im), c_t(c_re), c_t(c_im), e_col, e_row)
    sw = lambda x: jnp.swapaxes(x, 1, 2)
    return tt, wor, woi, sw(wstr), sw(wsti), sw(scr), sw(sci)


def _ssm_body(*refs, steps, chunks, gpb, has_init):
    if has_init:
        (ut_ref, d_ref, tt_ref, wor_ref, woi_ref, wsr_ref, wsi_ref, scr_ref, sci_ref,
         s0r_ref, s0i_ref, yt_ref, sfr_ref, sfi_ref) = refs
    else:
        (ut_ref, d_ref, tt_ref, wor_ref, woi_ref, wsr_ref, wsi_ref, scr_ref, sci_ref,
         yt_ref, sfr_ref, sfi_ref) = refs
    h = SSM_GROUP
    th = steps * h
    w = ut_ref.shape[1] // steps
    p = scr_ref.shape[1]
    ws_off = wsr_ref.shape[2] - th
    for gl in range(gpb):
        rows = slice(gl * h, (gl + 1) * h)
        u = jnp.concatenate([ut_ref[rows, t * w:(t + 1) * w] for t in range(steps)], axis=0)
        ub = u.astype(BF16)
        y = _dot(tt_ref[gl, :th, :th], ub)
        xr = _dot(wsr_ref[gl, :, ws_off:], ub)
        xi = _dot(wsi_ref[gl, :, ws_off:], ub)
        if chunks > 1:
            c_idx = lax.broadcasted_iota(jnp.int32, (p, w), 1) & (chunks - 1)
            sr, si = xr, xi
            k = 0
            while (1 << k) < chunks:
                sh = 1 << k
                rr = pltpu.roll(sr, sh, axis=1)
                ri = pltpu.roll(si, sh, axis=1)
                mr = scr_ref[gl, :, k:k + 1]
                mi = sci_ref[gl, :, k:k + 1]
                keep = c_idx >= sh
                sr = sr + jnp.where(keep, mr * rr - mi * ri, 0.0)
                si = si + jnp.where(keep, mr * ri + mi * rr, 0.0)
                k += 1
            first = c_idx >= 1
            pr = jnp.where(first, pltpu.roll(sr, 1, axis=1), 0.0)
            pi = jnp.where(first, pltpu.roll(si, 1, axis=1), 0.0)
            for n in range(w // chunks):
                last = n * chunks + chunks - 1
                sfr_ref[gl, :, n:n + 1] = sr[:, last:last + 1]
                sfi_ref[gl, :, n:n + 1] = si[:, last:last + 1]
        else:
            pr, pi = s0r_ref[gl], s0i_ref[gl]
            lr = scr_ref[gl, :, 7:8]
            li = sci_ref[gl, :, 7:8]
            sfr_ref[gl] = lr * pr - li * pi + xr
            sfi_ref[gl] = lr * pi + li * pr + xi
        y = y + _dot(wor_ref[gl, :th, :], pr.astype(BF16)) + _dot(woi_ref[gl, :th, :], pi.astype(BF16))
        y = y + d_ref[gl, :th, :] * u
        y = 0.5 * y * (1.0 + jnp.tanh(GELU_C * (y + 0.044715 * (y * y * y))))
        yb = y.astype(BF16)
        for t in range(steps):
            yt_ref[rows, t * w:(t + 1) * w] = yb[t * h:(t + 1) * h, :]


def _ssm(ut, d_t, ops, layer, n_groups, steps, chunks, s0=None, gpb=8):
    tt, wor, woi, wsr, wsi, scr, sci = ops
    c, lanes = ut.shape
    h = SSM_GROUP
    w = lanes // steps
    nseq = w // chunks
    p = scr.shape[1]
    thf = tt.shape[1]
    base = layer * (n_groups // gpb)
    wblk = lambda *s: pl.BlockSpec((gpb,) + s, lambda g: (base + g,) + (0,) * len(s))
    in_specs = [pl.BlockSpec((gpb * h, lanes), lambda g: (g, 0)),
                wblk(thf, 1), wblk(thf, thf), wblk(thf, p), wblk(thf, p), wblk(p, thf), wblk(p, thf),
                wblk(p, scr.shape[2]), wblk(p, scr.shape[2])]
    args = [ut, d_t, tt, wor, woi, wsr, wsi, scr, sci]
    if s0 is not None:
        in_specs += [pl.BlockSpec((gpb, p, w), lambda g: (g, 0, 0))] * 2
        args += list(s0)
    sf_spec = pl.BlockSpec((gpb, p, nseq), lambda g: (g, 0, 0))
    return pl.pallas_call(
        functools.partial(_ssm_body, steps=steps, chunks=chunks, gpb=gpb, has_init=s0 is not None),
        grid=(n_groups // gpb,),
        in_specs=in_specs,
        out_specs=[pl.BlockSpec((gpb * h, lanes), lambda g: (g, 0)), sf_spec, sf_spec],
        out_shape=[jax.ShapeDtypeStruct((c, lanes), BF16),
                   jax.ShapeDtypeStruct((n_groups, p, nseq), F32),
                   jax.ShapeDtypeStruct((n_groups, p, nseq), F32)],
        compiler_params=_cparams(("parallel",)),
        name="ssm_chunks" if s0 is None else "ssm_step",
    )(*args)


def _pool_seq_body(u_ref, w_ref, sc_ref, o_ref, z_ref):
    l, c = u_ref.shape
    pad = z_ref.shape[0] - l
    cg = c // len(POOL_WINDOWS)
    z_ref[:pad, :] = jnp.zeros((pad, c), F32)
    z_ref[pad:, :] = u_ref[...]
    pos1 = (lax.broadcasted_iota(jnp.int32, (l, 1), 0) + 1).astype(F32)
    for gi, win in enumerate(POOL_WINDOWS):
        cols = slice(gi * cg, (gi + 1) * cg)
        cur = z_ref[pad:, cols]
        tot = cur
        for k in range(1, win):
            tot = tot + z_ref[pad - k:pad - k + l, cols]
        inv_cnt = 1.0 / jnp.minimum(pos1, float(win))
        mixed = _dot((tot * inv_cnt - cur).astype(BF16), w_ref[gi].astype(BF16))
        o_ref[:, cols] = (mixed * sc_ref[:, cols]).astype(BF16)


def _pool_seq(z, w_pool, scale, layer, nseq, seqlen, width, col_blk):
    return pl.pallas_call(
        _pool_seq_body,
        grid=(nseq,),
        in_specs=[pl.BlockSpec((seqlen, width), lambda n: (n, col_blk)),
                  pl.BlockSpec((None,) + w_pool.shape[1:], lambda n: (layer, 0, 0, 0)),
                  pl.BlockSpec((None, 1, width), lambda n: (layer, 0, 0))],
        out_specs=pl.BlockSpec((seqlen, width), lambda n: (n, 0)),
        out_shape=jax.ShapeDtypeStruct((nseq * seqlen, width), BF16),
        scratch_shapes=[pltpu.VMEM((seqlen + 16, width), F32)],
        compiler_params=_cparams(("parallel",)),
        name="pool_seq",
    )(z, w_pool, scale)


def _pool_step_body(u_ref, prev_ref, w_ref, sc_ref, o_ref):
    n, buf, c = prev_ref.shape
    steps = u_ref.shape[0] // n
    cg = c // len(POOL_WINDOWS)

    def row(j, cols):
        return prev_ref[:, j, cols] if j < buf else u_ref[(j - buf) * n:(j - buf + 1) * n, cols]

    for gi, win in enumerate(POOL_WINDOWS):
        cols = slice(gi * cg, (gi + 1) * cg)
        for t in range(steps):
            cur = row(buf + t, cols)
            tot = cur
            for k in range(1, win):
                tot = tot + row(buf + t - k, cols)
            mixed = _dot((tot * (1.0 / win) - cur).astype(BF16), w_ref[gi].astype(BF16))
            o_ref[t * n:(t + 1) * n, cols] = (mixed * sc_ref[:, cols]).astype(BF16)


def _pool_step(z, prev, w_pool, scale, layer, rows, col_blk):
    n, buf, c = prev.shape[1:]
    row_blk = z.shape[0] // rows - 1
    return pl.pallas_call(
        _pool_step_body,
        grid=(1,),
        in_specs=[pl.BlockSpec((rows, c), lambda i: (row_blk, col_blk)),
                  pl.BlockSpec((None, n, buf, c), lambda i: (layer, 0, 0, 0)),
                  pl.BlockSpec((None,) + w_pool.shape[1:], lambda i: (layer, 0, 0, 0)),
                  pl.BlockSpec((None, 1, c), lambda i: (layer, 0, 0))],
        out_specs=pl.BlockSpec((rows, c), lambda i: (0, 0)),
        out_shape=jax.ShapeDtypeStruct((rows, c), BF16),
        compiler_params=_cparams(("arbitrary",)),
        name="pool_step",
    )(z, prev, w_pool, scale)


def _mix_body(gap_ref, gas_ref, ybp_ref, ybs_ref, sa_ref, sb_ref, wa_ref, wb_ref, wp_ref, o_ref, *, p_tiles):
    first = pl.program_id(0) < p_tiles
    ga = jnp.where(first, gap_ref[...], gas_ref[...])
    yb = jnp.where(first, ybp_ref[...], ybs_ref[...])
    for cs in _col_chunks(o_ref.shape[1], MXU_COLS):
        br_a = _dot(ga, wa_ref[:, cs].astype(BF16)) * jax.nn.sigmoid(_dot(ga, wb_ref[:, cs].astype(BF16)))
        br_b = _dot(yb, wp_ref[:, cs].astype(BF16))
        o_ref[:, cs] = (sa_ref[:, cs] * br_a + sb_ref[:, cs] * br_b).astype(BF16)


def _resident(shape, layer):
    return pl.BlockSpec((None,) + shape, lambda i: (layer,) + (0,) * len(shape), pipeline_mode=pl.Buffered(1))


def _split_specs(tm, width, p_tiles):
    return [pl.BlockSpec((tm, width), lambda i: (jnp.minimum(i, p_tiles - 1), 0)),
            pl.BlockSpec((tm, width), lambda i: (jnp.maximum(i - p_tiles, 0), 0))]


def _mix(ga_p, ga_s, yb_p, yb_s, z, wa, wb, wp, layer, tm, gate_col):
    m = z.shape[0]
    k = ga_p.shape[1]
    n = wa.shape[-1]
    gblk = gate_col // n
    p_tiles = ga_p.shape[0] // tm
    return pl.pallas_call(
        functools.partial(_mix_body, p_tiles=p_tiles),
        grid=(m // tm,),
        in_specs=_split_specs(tm, k, p_tiles) + _split_specs(tm, k, p_tiles) + [
                  pl.BlockSpec((tm, n), lambda i: (i, gblk)),
                  pl.BlockSpec((tm, n), lambda i: (i, gblk + 1)),
                  _resident((k, n), layer), _resident((k, n), layer), _resident((k, n), layer)],
        out_specs=pl.BlockSpec((tm, n), lambda i: (i, 0)),
        out_shape=jax.ShapeDtypeStruct((m, n), BF16),
        compiler_params=_cparams(("parallel",)),
        name="mix",
    )(ga_p, ga_s, yb_p, yb_s, z, z, wa, wb, wp)


def _resmm_body(a_ref, w_ref, h_ref, o_ref):
    a = a_ref[...]
    for cs in _col_chunks(o_ref.shape[1], MXU_COLS):
        o_ref[:, cs] = h_ref[:, cs] + _dot(a, w_ref[:, cs].astype(BF16))


def _resmm(a, w, hres, layer, tm):
    m, k = a.shape
    n = w.shape[-1]
    return pl.pallas_call(
        _resmm_body,
        grid=(m // tm,),
        in_specs=[pl.BlockSpec((tm, k), lambda i: (i, 0)),
                  _resident((k, n), layer),
                  pl.BlockSpec((tm, n), lambda i: (i, 0))],
        out_specs=pl.BlockSpec((tm, n), lambda i: (i, 0)),
        out_shape=jax.ShapeDtypeStruct((m, n), F32),
        compiler_params=_cparams(("parallel",)),
        name="resmm",
    )(a, w, hres)


def _ple_update(x_ref, g_ref, pp_ref, ps_ref, wp_ref, wg_ref, dst_ref, p_tiles):
    xn = _rms_bf16(x_ref[...], g_ref[...])
    pb = jnp.where(pl.program_id(0) < p_tiles, pp_ref[...], ps_ref[...]).astype(BF16)
    for cs in _col_chunks(dst_ref.shape[1], MXU_COLS):
        gate = jax.nn.sigmoid(_dot(xn, wg_ref[:, cs].astype(BF16)))
        dst_ref[:, cs] = x_ref[:, cs] + _dot(pb, wp_ref[:, cs].astype(BF16)) * gate


def _ple_body(x_ref, g_ref, pp_ref, ps_ref, wp_ref, wg_ref, o_ref, *, p_tiles):
    _ple_update(x_ref, g_ref, pp_ref, ps_ref, wp_ref, wg_ref, o_ref, p_tiles)


def _ple_final_body(x_ref, g_ref, pp_ref, ps_ref, wp_ref, wg_ref, gf_ref, op_ref, os_ref, h_ref, *, p_tiles):
    _ple_update(x_ref, g_ref, pp_ref, ps_ref, wp_ref, wg_ref, h_ref, p_tiles)
    h = h_ref[...]
    y = h * lax.rsqrt(jnp.mean(h * h, axis=-1, keepdims=True) + RMS_EPS) * gf_ref[...]
    i = pl.program_id(0)

    @pl.when(i < p_tiles)
    def _():
        op_ref[...] = y

    @pl.when(i >= p_tiles)
    def _():
        os_ref[...] = y


def _ple(x, g, p_p, p_s, wp, wg, layer, tm, g_final=None):
    m, d = x.shape
    mp, pd = p_p.shape[1:]
    p_tiles = mp // tm
    in_specs = [pl.BlockSpec((tm, d), lambda i: (i, 0)),
                pl.BlockSpec((None, 1, d), lambda i: (layer, 0, 0)),
                pl.BlockSpec((None, tm, pd), lambda i: (layer, jnp.minimum(i, p_tiles - 1), 0)),
                pl.BlockSpec((None, tm, pd), lambda i: (layer, jnp.maximum(i - p_tiles, 0), 0)),
                _resident((pd, d), layer), _resident((d, d), layer)]
    if g_final is None:
        return pl.pallas_call(
            functools.partial(_ple_body, p_tiles=p_tiles), grid=(m // tm,), in_specs=in_specs,
            out_specs=pl.BlockSpec((tm, d), lambda i: (i, 0)),
            out_shape=jax.ShapeDtypeStruct((m, d), F32),
            compiler_params=_cparams(("parallel",)),
            name="ple",
        )(x, g, p_p, p_s, wp, wg)
    return pl.pallas_call(
        functools.partial(_ple_final_body, p_tiles=p_tiles),
        grid=(m // tm,),
        in_specs=in_specs + [pl.BlockSpec((1, d), lambda i: (0, 0))],
        out_specs=[pl.BlockSpec((tm, d), lambda i: (jnp.minimum(i, p_tiles - 1), 0)),
                   pl.BlockSpec((tm, d), lambda i: (jnp.maximum(i - p_tiles, 0), 0))],
        out_shape=[jax.ShapeDtypeStruct((mp, d), F32), jax.ShapeDtypeStruct((m - mp, d), F32)],
        scratch_shapes=[pltpu.VMEM((tm, d), F32)],
        compiler_params=_cparams(("arbitrary",)),
        name="ple_final",
    )(x, g, p_p, p_s, wp, wg, g_final)


def _pick_tile(n, pref):
    t = min(pref, n)
    while n % t:
        t //= 2
    return t


def kernel(x_prompt, x_sample, state_ssm_re, state_ssm_im, state_pool, p_prompt, p_sample, g_ffn1, w_ffn1_gate, w_ffn1_up, w_ffn1_down, g_mix, w_in, ssm_a_re, ssm_a_im, ssm_log_dt, ssm_b_re, ssm_b_im, ssm_c_re, ssm_c_im, ssm_d, w_glu_a, w_glu_b, w_pool, pool_scale, w_pool_up, w_out, g_ffn2, w_ffn2_gate, w_ffn2_up, w_ffn2_down, g_ple, w_ple, w_ple_gate, g_final):
    nb, seq, d = x_prompt.shape
    ns, dseq, _ = x_sample.shape
    depth, n_groups, p_state = ssm_a_re.shape
    h = ssm_b_re.shape[-1]
    sw = n_groups * h
    pw = pool_scale.shape[-1]
    buf = state_pool.shape[2]
    chunk = SSM_CHUNK
    assert h == SSM_GROUP and dseq * 2 == chunk and seq % chunk == 0 and buf == max(POOL_WINDOWS) - 1
    n_chunks = seq // chunk
    assert n_chunks & (n_chunks - 1) == 0 and n_chunks <= 128
    mp, ms = nb * seq, ns * dseq
    m = mp + ms
    tm = _pick_tile(m, 1024)

    g3 = lambda a: a.reshape(depth, 1, -1)
    wg1, wu1, wd1 = w_ffn1_gate, w_ffn1_up, w_ffn1_down
    wg2, wu2, wd2 = w_ffn2_gate, w_ffn2_up, w_ffn2_down
    wga, wgb, wpu, wo = w_glu_a, w_glu_b, w_pool_up, w_out
    wpl, wpg, wpool = w_ple, w_ple_gate, w_pool
    tn_in = _pick_tile(sw + pw, 2048)
    gf1, gmx, gf2, gpl = g3(g_ffn1), g3(g_mix), g3(g_ffn2), g3(g_ple)
    pscale = g3(pool_scale)

    flat = lambda a: a.reshape((depth * n_groups,) + a.shape[2:])
    ops = _ssm_weights(flat(ssm_a_re), flat(ssm_a_im), flat(ssm_log_dt), flat(ssm_b_re), flat(ssm_b_im),
                       flat(ssm_c_re), flat(ssm_c_im), chunk)
    d_t = jnp.tile(flat(ssm_d), (1, chunk)).reshape(depth * n_groups, chunk * h, 1)

    hcur = jnp.concatenate([x_prompt.reshape(mp, d), x_sample.transpose(1, 0, 2).reshape(ms, d)], axis=0)
    p_p = p_prompt.reshape(depth, mp, -1)
    p_s = p_sample.transpose(0, 2, 1, 3).reshape(depth, ms, -1)
    tf = _pick_tile(w_ffn1_gate.shape[-1], 512)
    tm2 = _pick_tile(math.gcd(mp, ms), 512)
    gpb = min(8, n_groups)
    new_re_p, new_im_p, new_pool_p, new_re_s, new_im_s, new_pool_s = [], [], [], [], [], []
    for i in range(depth):
        h1 = _ffn(hcur, gf1, wg1, wu1, wd1, i, tm, tf)
        z = _inproj(h1, gmx, w_in, i, tm2, tn_in, sw + pw)

        ua_p = z[:mp, :sw].reshape(nb, n_chunks, chunk, sw).transpose(3, 2, 0, 1).reshape(sw, mp)
        ua_s = z[mp:, :sw].T
        yt_p, sr_p, si_p = _ssm(ua_p, d_t, ops, i, n_groups, chunk, n_chunks, gpb=gpb)
        s0 = (state_ssm_re[i].transpose(1, 2, 0), state_ssm_im[i].transpose(1, 2, 0))
        yt_s, sr_s, si_s = _ssm(ua_s, d_t, ops, i, n_groups, dseq, 1, s0=s0, gpb=gpb)
        ga_p = yt_p.reshape(sw, chunk, nb, n_chunks).transpose(2, 3, 1, 0).reshape(mp, sw)
        ga_s = yt_s.T
        new_re_p.append(sr_p.transpose(2, 0, 1))
        new_im_p.append(si_p.transpose(2, 0, 1))
        new_re_s.append(sr_s.transpose(2, 0, 1))
        new_im_s.append(si_s.transpose(2, 0, 1))

        yb_p = _pool_seq(z, wpool, pscale, i, nb, seq, pw, sw // pw)
        yb_s = _pool_step(z, state_pool, wpool, pscale, i, ms, sw // pw)
        ub_s = z[mp:, sw:sw + pw].reshape(dseq, ns, pw).transpose(1, 0, 2)
        new_pool_p.append(z[:mp, sw:sw + pw].reshape(nb, seq, pw)[:, seq - buf:, :])
        new_pool_s.append(jnp.concatenate([state_pool[i][:, dseq:, :], ub_s], axis=1))

        merged = _mix(ga_p, ga_s, yb_p, yb_s, z, wga, wgb, wpu, i, tm2, sw + pw)
        h2 = _resmm(merged, wo, h1, i, tm2)
        h3 = _ffn(h2, gf2, wg2, wu2, wd2, i, tm, tf)
        if i + 1 < depth:
            hcur = _ple(h3, gpl, p_p, p_s, wpl, wpg, i, tm2)
        else:
            y_p, y_s = _ple(h3, gpl, p_p, p_s, wpl, wpg, i, tm2, g_final=g_final.reshape(1, d))

    return (y_p.reshape(nb, seq, d), y_s.reshape(dseq, ns, d).transpose(1, 0, 2),
            jnp.stack(new_re_p), jnp.stack(new_im_p), jnp.stack(new_pool_p),
            jnp.stack(new_re_s), jnp.stack(new_im_s), jnp.stack(new_pool_s))
```

```python
import functools
import math

import numpy as np
import jax
import jax.numpy as jnp
from jax import lax
from jax.experimental import pallas as pl
from jax.experimental.pallas import tpu as pltpu

F32 = jnp.float32
BF16 = jnp.bfloat16
RMS_EPS = 1e-6
POOL_WINDOWS = (2, 4, 8, 16)
SSM_GROUP = 16
SSM_CHUNK = 16
GELU_C = math.sqrt(2.0 / math.pi)
VMEM_LIMIT = 60 * 1024 * 1024
MXU_COLS = 256


def _cparams(sem):
    return pltpu.CompilerParams(dimension_semantics=sem, vmem_limit_bytes=VMEM_LIMIT)


def _rms_bf16(x, g):
    inv = lax.rsqrt(jnp.mean(x * x, axis=-1, keepdims=True) + RMS_EPS)
    return (x * inv * g).astype(BF16)


def _dot(a, b):
    return jnp.dot(a, b, preferred_element_type=F32)


def _col_chunks(width, chunk):
    chunk = min(chunk, width)
    return [slice(c0, c0 + chunk) for c0 in range(0, width, chunk)]


def _ffn_body(x_ref, g_ref, wg_ref, wu_ref, wd_ref, o_ref, xn_ref):
    j = pl.program_id(1)

    @pl.when(j == 0)
    def _():
        xn_ref[...] = _rms_bf16(x_ref[...], g_ref[...])
        o_ref[...] = jnp.zeros_like(o_ref)

    xn = xn_ref[...]
    mids = []
    for cs in _col_chunks(wg_ref.shape[1], MXU_COLS):
        a = _dot(xn, wg_ref[:, cs].astype(BF16))
        b = _dot(xn, wu_ref[:, cs].astype(BF16))
        mids.append((a * jax.nn.sigmoid(a) * b).astype(BF16))
    mid = jnp.concatenate(mids, axis=1)
    for cs in _col_chunks(o_ref.shape[1], 2 * MXU_COLS):
        o_ref[:, cs] += _dot(mid, wd_ref[:, cs].astype(BF16))

    @pl.when(j == pl.num_programs(1) - 1)
    def _():
        o_ref[...] = x_ref[...] + 0.5 * o_ref[...]


def _ffn(x, g, wg, wu, wd, layer, tm, tf):
    m, d = x.shape
    f = wg.shape[-1]
    return pl.pallas_call(
        _ffn_body,
        grid=(m // tm, f // tf),
        in_specs=[
            pl.BlockSpec((tm, d), lambda i, j: (i, 0), pipeline_mode=pl.Buffered(1)),
            pl.BlockSpec((None, 1, d), lambda i, j: (layer, 0, 0)),
            pl.BlockSpec((None, d, tf), lambda i, j: (layer, 0, j)),
            pl.BlockSpec((None, d, tf), lambda i, j: (layer, 0, j)),
            pl.BlockSpec((None, tf, d), lambda i, j: (layer, j, 0)),
        ],
        out_specs=pl.BlockSpec((tm, d), lambda i, j: (i, 0), pipeline_mode=pl.Buffered(1)),
        out_shape=jax.ShapeDtypeStruct((m, d), F32),
        scratch_shapes=[pltpu.VMEM((tm, d), BF16)],
        compiler_params=_cparams(("parallel", "arbitrary")),
        name="ffn",
    )(x, g, wg, wu, wd)


def _inproj_body(x_ref, g_ref, w_ref, o_ref, *, gate_from):
    is_gate = pl.program_id(0) >= gate_from
    xn = _rms_bf16(x_ref[...], g_ref[...])
    for cs in _col_chunks(o_ref.shape[1], MXU_COLS):
        r = _dot(xn, w_ref[:, cs].astype(BF16))
        o_ref[:, cs] = jnp.where(is_gate, jax.nn.sigmoid(r), r)


def _inproj(x, g, w, layer, tm, tn, n_plain):
    m, d = x.shape
    n = w.shape[-1]
    return pl.pallas_call(
        functools.partial(_inproj_body, gate_from=n_plain // tn),
        grid=(n // tn, m // tm),
        in_specs=[
            pl.BlockSpec((tm, d), lambda j, i: (i, 0)),
            pl.BlockSpec((None, 1, d), lambda j, i: (layer, 0, 0)),
            pl.BlockSpec((None, d, tn), lambda j, i: (layer, 0, j), pipeline_mode=pl.Buffered(1)),
        ],
        out_specs=pl.BlockSpec((tm, tn), lambda j, i: (i, j)),
        out_shape=jax.ShapeDtypeStruct((m, n), F32),
        compiler_params=_cparams(("arbitrary", "arbitrary")),
        name="inproj",
    )(x, g, w)


def _cmul(ar, ai, br, bi):
    return ar * br - ai * bi, ar * bi + ai * br


def _ssm_weights_body(ar_ref, ai_ref, ldt_ref, btr_ref, bti_ref, ctr_ref, cti_ref, e_col, e_row,
                      tt_o, wor_o, woi_o, wstr_o, wsti_o, scr_o, sci_o, *, chunk, gblk):
    h = SSM_GROUP
    causal = e_col[...] >= e_row[...]
    nt = (((1,), (1,)), ((), ()))
    hi = lax.Precision.HIGHEST

    def rows(pows):
        width = pows[0][0].shape[1]
        return tuple(jnp.concatenate([jnp.broadcast_to(x[k], (h, width)) for x in pows], axis=0)
                     for k in (0, 1))

    for gl in range(gblk):
        dt = jnp.exp(ldt_ref[gl])
        a_re, a_im = ar_ref[gl], ai_ref[gl]
        mag = jnp.exp(a_re * dt)
        ang = a_im * dt
        lr, li = mag * jnp.cos(ang), mag * jnp.sin(ang)
        den = a_re * a_re + a_im * a_im
        num_re = lr - 1.0
        k_re = (num_re * a_re + li * a_im) / den
        k_im = (li * a_re - num_re * a_im) / den
        inv = 1.0 / (lr * lr + li * li)
        nr, ni = lr * inv, -li * inv
        pw = [(jnp.ones_like(lr), jnp.zeros_like(lr))]
        npw = list(pw)
        for _ in range(chunk):
            pw.append(_cmul(*pw[-1], lr, li))
            npw.append(_cmul(*npw[-1], nr, ni))
        kb = _cmul(k_re, k_im, btr_ref[gl], bti_ref[gl])
        c = (ctr_ref[gl], cti_ref[gl])

        l_re, l_im = _cmul(*c, *rows(pw[:chunk]))
        r_re, r_im = _cmul(*rows(npw[:chunk]), *kb)
        kmat = (lax.dot_general(l_re, r_re, nt, precision=hi, preferred_element_type=F32)
                - lax.dot_general(l_im, r_im, nt, precision=hi, preferred_element_type=F32))
        tt_o[gl] = jnp.where(causal, kmat, 0.0).astype(BF16)

        e_re, e_im = _cmul(*c, *rows(pw[1:chunk + 1]))
        wor_o[gl] = e_re.astype(BF16)
        woi_o[gl] = (-e_im).astype(BF16)

        s_re, s_im = _cmul(*rows(pw[chunk - 1::-1]), *kb)
        wstr_o[gl] = s_re.astype(BF16)
        wsti_o[gl] = s_im.astype(BF16)

        sc = [pw[chunk]]
        for _ in range(6):
            sc.append(_cmul(*sc[-1], *sc[-1]))
        sc.append(pw[chunk // 2])
        scr_o[gl] = jnp.concatenate([x[0] for x in sc], axis=0)
        sci_o[gl] = jnp.concatenate([x[1] for x in sc], axis=0)


def _ssm_weights(a_re, a_im, log_dt, b_re, b_im, c_re, c_im, chunk):
    dg, p = a_re.shape
    h = b_re.shape[-1]
    th = chunk * h
    tau = np.repeat(np.arange(chunk, dtype=np.float32), h)
    e_col = jnp.asarray(tau.reshape(th, 1))
    e_row = jnp.asarray(tau.reshape(1, th))
    gblk = min(16, dg)
    row = lambda x: x.reshape(dg, 1, p)
    bt_t = lambda x: jnp.tile(jnp.swapaxes(x, 1, 2), (1, chunk, 1))
    c_t = lambda x: jnp.tile(x, (1, chunk, 1))
    per_g = lambda *s: pl.BlockSpec((gblk,) + s, lambda g: (g,) + (0,) * len(s))
    const = lambda *s: pl.BlockSpec(s, lambda g: (0,) * len(s))
    tt, wor, woi, wstr, wsti, scr, sci = pl.pallas_call(
        functools.partial(_ssm_weights_body, chunk=chunk, gblk=gblk),
        grid=(dg // gblk,),
        in_specs=[per_g(1, p), per_g(1, p), per_g(1, 1),
                  per_g(th, p), per_g(th, p), per_g(th, p), per_g(th, p),
                  const(th, 1), const(1, th)],
        out_specs=[per_g(th, th), per_g(th, p), per_g(th, p), per_g(th, p), per_g(th, p),
                   per_g(8, p), per_g(8, p)],
        out_shape=[jax.ShapeDtypeStruct((dg, th, th), BF16),
                   jax.ShapeDtypeStruct((dg, th, p), BF16),
                   jax.ShapeDtypeStruct((dg, th, p), BF16),
                   jax.ShapeDtypeStruct((dg, th, p), BF16),
                   jax.ShapeDtypeStruct((dg, th, p), BF16),
                   jax.ShapeDtypeStruct((dg, 8, p), F32),
                   jax.ShapeDtypeStruct((dg, 8, p), F32)],
        compiler_params=_cparams(("parallel",)),
        name="ssm_weights",
    )(row(a_re), row(a_im), log_dt.reshape(dg, 1, 1),
      bt_t(b_re), bt_t(b_im), c_t(c_re), c_t(c_im), e_col, e_row)
    sw = lambda x: jnp.swapaxes(x, 1, 2)
    return tt, wor, woi, sw(wstr), sw(wsti), sw(scr), sw(sci)


def _ssm_body(*refs, steps, chunks, gpb, has_init):
    if has_init:
        (u_ref, d_ref, tt_ref, wor_ref, woi_ref, wsr_ref, wsi_ref, scr_ref, sci_ref,
         s0r_ref, s0i_ref, y_ref, sfr_ref, sfi_ref, ys_ref) = refs
    else:
        (u_ref, d_ref, tt_ref, wor_ref, woi_ref, wsr_ref, wsi_ref, scr_ref, sci_ref,
         y_ref, sfr_ref, sfi_ref, ys_ref) = refs
    h = SSM_GROUP
    th = steps * h
    w = u_ref.shape[0] // steps
    p = scr_ref.shape[1]
    ws_off = wsr_ref.shape[2] - th

    def step_rows(t):
        return pl.ds(t, w, stride=steps) if chunks > 1 else pl.ds(t * w, w)

    slabs = [u_ref[step_rows(t), :].T for t in range(steps)]
    for gl in range(gpb):
        rows = slice(gl * h, (gl + 1) * h)
        u = jnp.concatenate([s[rows, :] for s in slabs], axis=0)
        ub = u.astype(BF16)
        y = _dot(tt_ref[gl, :th, :th], ub)
        xr = _dot(wsr_ref[gl, :, ws_off:], ub)
        xi = _dot(wsi_ref[gl, :, ws_off:], ub)
        if chunks > 1:
            c_idx = lax.broadcasted_iota(jnp.int32, (p, w), 1) & (chunks - 1)
            sr, si = xr, xi
            k = 0
            while (1 << k) < chunks:
                sh = 1 << k
                rr = pltpu.roll(sr, sh, axis=1)
                ri = pltpu.roll(si, sh, axis=1)
                mr = scr_ref[gl, :, k:k + 1]
                mi = sci_ref[gl, :, k:k + 1]
                keep = c_idx >= sh
                sr = sr + jnp.where(keep, mr * rr - mi * ri, 0.0)
                si = si + jnp.where(keep, mr * ri + mi * rr, 0.0)
                k += 1
            first = c_idx >= 1
            pr = jnp.where(first, pltpu.roll(sr, 1, axis=1), 0.0)
            pi = jnp.where(first, pltpu.roll(si, 1, axis=1), 0.0)
            for n in range(w // chunks):
                last = n * chunks + chunks - 1
                sfr_ref[gl, :, n:n + 1] = sr[:, last:last + 1]
                sfi_ref[gl, :, n:n + 1] = si[:, last:last + 1]
        else:
            pr, pi = s0r_ref[gl], s0i_ref[gl]
            lr = scr_ref[gl, :, 7:8]
            li = sci_ref[gl, :, 7:8]
            sfr_ref[gl] = lr * pr - li * pi + xr
            sfi_ref[gl] = lr * pi + li * pr + xi
        y = y + _dot(wor_ref[gl, :th, :], pr.astype(BF16)) + _dot(woi_ref[gl, :th, :], pi.astype(BF16))
        y = y + d_ref[gl, :th, :] * u
        y = 0.5 * y * (1.0 + jnp.tanh(GELU_C * (y + 0.044715 * (y * y * y))))
        for t in range(steps):
            ys_ref[t, rows, :] = y[t * h:(t + 1) * h, :]
    for t in range(steps):
        y_ref[step_rows(t), :] = ys_ref[t].T


def _ssm(z, row_blk, rows, d_t, ops, layer, n_groups, steps, chunks, s0=None, gpb=8):
    tt, wor, woi, wsr, wsi, scr, sci = ops
    h = SSM_GROUP
    w = rows // steps
    nseq = w // chunks
    p = scr.shape[1]
    thf = tt.shape[1]
    base = layer * (n_groups // gpb)
    wblk = lambda *s: pl.BlockSpec((gpb,) + s, lambda g: (base + g,) + (0,) * len(s))
    in_specs = [pl.BlockSpec((rows, gpb * h), lambda g: (row_blk, g)),
                wblk(thf, 1), wblk(thf, thf), wblk(thf, p), wblk(thf, p), wblk(p, thf), wblk(p, thf),
                wblk(p, scr.shape[2]), wblk(p, scr.shape[2])]
    args = [z, d_t, tt, wor, woi, wsr, wsi, scr, sci]
    if s0 is not None:
        in_specs += [pl.BlockSpec((gpb, p, w), lambda g: (g, 0, 0))] * 2
        args += list(s0)
    sf_spec = pl.BlockSpec((gpb, p, nseq), lambda g: (g, 0, 0))
    return pl.pallas_call(
        functools.partial(_ssm_body, steps=steps, chunks=chunks, gpb=gpb, has_init=s0 is not None),
        grid=(n_groups // gpb,),
        in_specs=in_specs,
        out_specs=[pl.BlockSpec((rows, gpb * h), lambda g: (0, g)), sf_spec, sf_spec],
        out_shape=[jax.ShapeDtypeStruct((rows, n_groups * h), F32),
                   jax.ShapeDtypeStruct((n_groups, p, nseq), F32),
                   jax.ShapeDtypeStruct((n_groups, p, nseq), F32)],
        scratch_shapes=[pltpu.VMEM((steps, gpb * h, w), F32)],
        compiler_params=_cparams(("parallel",)),
        name="ssm_chunks" if s0 is None else "ssm_step",
    )(*args)


def _pool_seq_body(u_ref, w_ref, sc_ref, o_ref, z_ref):
    l, c = u_ref.shape
    pad = z_ref.shape[0] - l
    cg = c // len(POOL_WINDOWS)
    z_ref[:pad, :] = jnp.zeros((pad, c), F32)
    z_ref[pad:, :] = u_ref[...]
    pos1 = (lax.broadcasted_iota(jnp.int32, (l, 1), 0) + 1).astype(F32)
    for gi, win in enumerate(POOL_WINDOWS):
        cols = slice(gi * cg, (gi + 1) * cg)
        cur = z_ref[pad:, cols]
        tot = cur
        for k in range(1, win):
            tot = tot + z_ref[pad - k:pad - k + l, cols]
        inv_cnt = 1.0 / jnp.minimum(pos1, float(win))
        mixed = _dot((tot * inv_cnt - cur).astype(BF16), w_ref[gi].astype(BF16))
        o_ref[:, cols] = (mixed * sc_ref[:, cols]).astype(BF16)


def _pool_seq(z, w_pool, scale, layer, nseq, seqlen, width, col_blk):
    return pl.pallas_call(
        _pool_seq_body,
        grid=(nseq,),
        in_specs=[pl.BlockSpec((seqlen, width), lambda n: (n, col_blk)),
                  pl.BlockSpec((None,) + w_pool.shape[1:], lambda n: (layer, 0, 0, 0)),
                  pl.BlockSpec((None, 1, width), lambda n: (layer, 0, 0))],
        out_specs=pl.BlockSpec((seqlen, width), lambda n: (n, 0)),
        out_shape=jax.ShapeDtypeStruct((nseq * seqlen, width), BF16),
        scratch_shapes=[pltpu.VMEM((seqlen + 16, width), F32)],
        compiler_params=_cparams(("parallel",)),
        name="pool_seq",
    )(z, w_pool, scale)


def _pool_step_body(u_ref, prev_ref, w_ref, sc_ref, o_ref):
    n, buf, c = prev_ref.shape
    steps = u_ref.shape[0] // n
    cg = c // len(POOL_WINDOWS)

    def row(j, cols):
        return prev_ref[:, j, cols] if j < buf else u_ref[(j - buf) * n:(j - buf + 1) * n, cols]

    for gi, win in enumerate(POOL_WINDOWS):
        cols = slice(gi * cg, (gi + 1) * cg)
        for t in range(steps):
            cur = row(buf + t, cols)
            tot = cur
            for k in range(1, win):
                tot = tot + row(buf + t - k, cols)
            mixed = _dot((tot * (1.0 / win) - cur).astype(BF16), w_ref[gi].astype(BF16))
            o_ref[t * n:(t + 1) * n, cols] = (mixed * sc_ref[:, cols]).astype(BF16)


def _pool_step(z, prev, w_pool, scale, layer, rows, col_blk):
    n, buf, c = prev.shape[1:]
    row_blk = z.shape[0] // rows - 1
    return pl.pallas_call(
        _pool_step_body,
        grid=(1,),
        in_specs=[pl.BlockSpec((rows, c), lambda i: (row_blk, col_blk)),
                  pl.BlockSpec((None, n, buf, c), lambda i: (layer, 0, 0, 0)),
                  pl.BlockSpec((None,) + w_pool.shape[1:], lambda i: (layer, 0, 0, 0)),
                  pl.BlockSpec((None, 1, c), lambda i: (layer, 0, 0))],
        out_specs=pl.BlockSpec((rows, c), lambda i: (0, 0)),
        out_shape=jax.ShapeDtypeStruct((rows, c), BF16),
        compiler_params=_cparams(("arbitrary",)),
        name="pool_step",
    )(z, prev, w_pool, scale)


def _mix_body(gap_ref, gas_ref, ybp_ref, ybs_ref, sa_ref, sb_ref, wa_ref, wb_ref, wp_ref, o_ref, *, p_tiles):
    first = pl.program_id(0) < p_tiles
    ga = jnp.where(first, gap_ref[...], gas_ref[...]).astype(BF16)
    yb =jnp.where(first, ybp_ref[...], ybs_ref[...])
    for cs in _col_chunks(o_ref.shape[1], MXU_COLS):
        br_a = _dot(ga, wa_ref[:, cs].astype(BF16)) * jax.nn.sigmoid(_dot(ga, wb_ref[:, cs].astype(BF16)))
        br_b = _dot(yb, wp_ref[:, cs].astype(BF16))
        o_ref[:, cs] = (sa_ref[:, cs] * br_a + sb_ref[:, cs] * br_b).astype(BF16)


def _resident(shape, layer):
    return pl.BlockSpec((None,) + shape, lambda i: (layer,) + (0,) * len(shape), pipeline_mode=pl.Buffered(1))


def _split_specs(tm, width, p_tiles):
    return [pl.BlockSpec((tm, width), lambda i: (jnp.minimum(i, p_tiles - 1), 0)),
            pl.BlockSpec((tm, width), lambda i: (jnp.maximum(i - p_tiles, 0), 0))]


def _mix(ga_p, ga_s, yb_p, yb_s, z, wa, wb, wp, layer, tm, gate_col):
    m = z.shape[0]
    k = ga_p.shape[1]
    n = wa.shape[-1]
    gblk = gate_col // n
    p_tiles = ga_p.shape[0] // tm
    return pl.pallas_call(
        functools.partial(_mix_body, p_tiles=p_tiles),
        grid=(m // tm,),
        in_specs=_split_specs(tm, k, p_tiles) + _split_specs(tm, k, p_tiles) + [
                  pl.BlockSpec((tm, n), lambda i: (i, gblk)),
                  pl.BlockSpec((tm, n), lambda i: (i, gblk + 1)),
                  _resident((k, n), layer), _resident((k, n), layer), _resident((k, n), layer)],
        out_specs=pl.BlockSpec((tm, n), lambda i: (i, 0)),
        out_shape=jax.ShapeDtypeStruct((m, n), BF16),
        compiler_params=_cparams(("parallel",)),
        name="mix",
    )(ga_p, ga_s, yb_p, yb_s, z, z, wa, wb, wp)


def _resmm_body(a_ref, w_ref, h_ref, o_ref):
    a = a_ref[...]
    for cs in _col_chunks(o_ref.shape[1], MXU_COLS):
        o_ref[:, cs] = h_ref[:, cs] + _dot(a, w_ref[:, cs].astype(BF16))


def _resmm(a, w, hres, layer, tm):
    m, k = a.shape
    n = w.shape[-1]
    return pl.pallas_call(
        _resmm_body,
        grid=(m // tm,),
        in_specs=[pl.BlockSpec((tm, k), lambda i: (i, 0)),
                  _resident((k, n), layer),
                  pl.BlockSpec((tm, n), lambda i: (i, 0))],
        out_specs=pl.BlockSpec((tm, n), lambda i: (i, 0)),
        out_shape=jax.ShapeDtypeStruct((m, n), F32),
        compiler_params=_cparams(("parallel",)),
        name="resmm",
    )(a, w, hres)


def _ple_update(x_ref, g_ref, pp_ref, ps_ref, wp_ref, wg_ref, dst_ref, p_tiles):
    xn = _rms_bf16(x_ref[...], g_ref[...])
    pb = jnp.where(pl.program_id(0) < p_tiles, pp_ref[...], ps_ref[...]).astype(BF16)
    for cs in _col_chunks(dst_ref.shape[1], MXU_COLS):
        gate = jax.nn.sigmoid(_dot(xn, wg_ref[:, cs].astype(BF16)))
        dst_ref[:, cs] = x_ref[:, cs] + _dot(pb, wp_ref[:, cs].astype(BF16)) * gate


def _ple_body(x_ref, g_ref, pp_ref, ps_ref, wp_ref, wg_ref, o_ref, *, p_tiles):
    _ple_update(x_ref, g_ref, pp_ref, ps_ref, wp_ref, wg_ref, o_ref, p_tiles)


def _ple_final_body(x_ref, g_ref, pp_ref, ps_ref, wp_ref, wg_ref, gf_ref, op_ref, os_ref, h_ref, *, p_tiles):
    _ple_update(x_ref, g_ref, pp_ref, ps_ref, wp_ref, wg_ref, h_ref, p_tiles)
    h = h_ref[...]
    y = h * lax.rsqrt(jnp.mean(h * h, axis=-1, keepdims=True) + RMS_EPS) * gf_ref[...]
    i = pl.program_id(0)

    @pl.when(i < p_tiles)
    def _():
        op_ref[...] = y

    @pl.when(i >= p_tiles)
    def _():
        os_ref[...] = y


def _ple(x, g, p_p, p_s, wp, wg, layer, tm, g_final=None):
    m, d = x.shape
    mp, pd = p_p.shape[1:]
    p_tiles = mp // tm
    in_specs = [pl.BlockSpec((tm, d), lambda i: (i, 0)),
                pl.BlockSpec((None, 1, d), lambda i: (layer, 0, 0)),
                pl.BlockSpec((None, tm, pd), lambda i: (layer, jnp.minimum(i, p_tiles - 1), 0)),
                pl.BlockSpec((None, tm, pd), lambda i: (layer, jnp.maximum(i - p_tiles, 0), 0)),
                _resident((pd, d), layer), _resident((d, d), layer)]
    if g_final is None:
        return pl.pallas_call(
            functools.partial(_ple_body, p_tiles=p_tiles), grid=(m // tm,), in_specs=in_specs,
            out_specs=pl.BlockSpec((tm, d), lambda i: (i, 0)),
            out_shape=jax.ShapeDtypeStruct((m, d), F32),
            compiler_params=_cparams(("parallel",)),
            name="ple",
        )(x, g, p_p, p_s, wp, wg)
    return pl.pallas_call(
        functools.partial(_ple_final_body, p_tiles=p_tiles),
        grid=(m // tm,),
        in_specs=in_specs + [pl.BlockSpec((1, d), lambda i: (0, 0))],
        out_specs=[pl.BlockSpec((tm, d), lambda i: (jnp.minimum(i, p_tiles - 1), 0)),
                   pl.BlockSpec((tm, d), lambda i: (jnp.maximum(i - p_tiles, 0), 0))],
        out_shape=[jax.ShapeDtypeStruct((mp, d), F32), jax.ShapeDtypeStruct((m - mp, d), F32)],
        scratch_shapes=[pltpu.VMEM((tm, d), F32)],
        compiler_params=_cparams(("arbitrary",)),
        name="ple_final",
    )(x, g, p_p, p_s, wp, wg, g_final)


def _pick_tile(n, pref):
    t = min(pref, n)
    while n % t:
        t //= 2
    return t


def kernel(x_prompt, x_sample, state_ssm_re, state_ssm_im, state_pool, p_prompt, p_sample, g_ffn1, w_ffn1_gate, w_ffn1_up, w_ffn1_down, g_mix, w_in, ssm_a_re, ssm_a_im, ssm_log_dt, ssm_b_re, ssm_b_im, ssm_c_re, ssm_c_im, ssm_d, w_glu_a, w_glu_b, w_pool, pool_scale, w_pool_up, w_out, g_ffn2, w_ffn2_gate, w_ffn2_up, w_ffn2_down, g_ple, w_ple, w_ple_gate, g_final):
    nb, seq, d = x_prompt.shape
    ns, dseq, _ = x_sample.shape
    depth, n_groups, p_state = ssm_a_re.shape
    h = ssm_b_re.shape[-1]
    sw = n_groups * h
    pw = pool_scale.shape[-1]
    buf = state_pool.shape[2]
    chunk = SSM_CHUNK
    assert h == SSM_GROUP and dseq * 2 == chunk and seq % chunk == 0 and buf == max(POOL_WINDOWS) - 1
    n_chunks = seq // chunk
    assert n_chunks & (n_chunks - 1) == 0 and n_chunks <= 128
    mp, ms = nb * seq, ns * dseq
    m = mp + ms
    tm = _pick_tile(m, 1024)

    g3 = lambda a: a.reshape(depth, 1, -1)
    wg1, wu1, wd1 = w_ffn1_gate, w_ffn1_up, w_ffn1_down
    wg2, wu2, wd2 = w_ffn2_gate, w_ffn2_up, w_ffn2_down
    wga, wgb, wpu, wo = w_glu_a, w_glu_b, w_pool_up, w_out
    wpl, wpg, wpool = w_ple, w_ple_gate, w_pool
    tn_in = _pick_tile(sw + pw, 2048)
    gf1, gmx, gf2, gpl = g3(g_ffn1), g3(g_mix), g3(g_ffn2), g3(g_ple)
    pscale = g3(pool_scale)

    flat = lambda a: a.reshape((depth * n_groups,) + a.shape[2:])
    ops = _ssm_weights(flat(ssm_a_re), flat(ssm_a_im), flat(ssm_log_dt), flat(ssm_b_re), flat(ssm_b_im),
                       flat(ssm_c_re), flat(ssm_c_im), chunk)
    d_t = jnp.tile(flat(ssm_d), (1, chunk)).reshape(depth * n_groups, chunk * h, 1)

    hcur = jnp.concatenate([x_prompt.reshape(mp, d), x_sample.transpose(1, 0, 2).reshape(ms, d)], axis=0)
    p_p = p_prompt.reshape(depth, mp, -1)
    p_s = p_sample.transpose(0, 2, 1, 3).reshape(depth, ms, -1)
    tf = _pick_tile(w_ffn1_gate.shape[-1], 512)
    tm2 = _pick_tile(math.gcd(mp, ms), 512)
    gpb = min(8, n_groups)
    new_re_p, new_im_p, new_pool_p, new_re_s, new_im_s, new_pool_s = [], [], [], [], [], []
    for i in range(depth):
        h1 = _ffn(hcur, gf1, wg1, wu1, wd1, i, tm, tf)
        z = _inproj(h1, gmx, w_in, i, tm2, tn_in, sw + pw)

        ga_p, sr_p, si_p = _ssm(z, 0, mp, d_t, ops, i, n_groups, chunk, n_chunks, gpb=gpb)
        s0 = (state_ssm_re[i].transpose(1, 2, 0), state_ssm_im[i].transpose(1, 2, 0))
        ga_s, sr_s, si_s = _ssm(z, mp // ms, ms, d_t, ops, i, n_groups, dseq, 1, s0=s0, gpb=gpb)
        new_re_p.append(sr_p.transpose(2, 0, 1))
        new_im_p.append(si_p.transpose(2, 0, 1))
        new_re_s.append(sr_s.transpose(2, 0, 1))
        new_im_s.append(si_s.transpose(2, 0, 1))

        yb_p = _pool_seq(z, wpool, pscale, i, nb, seq, pw, sw // pw)
        yb_s = _pool_step(z, state_pool, wpool, pscale, i, ms, sw // pw)
        ub_s = z[mp:, sw:sw + pw].reshape(dseq, ns, pw).transpose(1, 0, 2)
        new_pool_p.append(z[:mp, sw:sw + pw].reshape(nb, seq, pw)[:, seq - buf:, :])
        new_pool_s.append(jnp.concatenate([state_pool[i][:, dseq:, :], ub_s], axis=1))

        merged = _mix(ga_p, ga_s, yb_p, yb_s, z, wga, wgb, wpu, i, tm2, sw + pw)
        h2 = _resmm(merged, wo, h1, i, tm2)
        h3 = _ffn(h2, gf2, wg2, wu2, wd2, i, tm, tf)
        if i + 1 < depth:
            hcur = _ple(h3, gpl, p_p, p_s, wpl, wpg, i, tm2)
        else:
            y_p, y_s = _ple(h3, gpl, p_p, p_s, wpl, wpg, i, tm2, g_final=g_final.reshape(1, d))

    return (y_p.reshape(nb, seq, d), y_s.reshape(dseq, ns, d).transpose(1, 0, 2),
            jnp.stack(new_re_p), jnp.stack(new_im_p), jnp.stack(new_pool_p),
            jnp.stack(new_re_s), jnp.stack(new_im_s), jnp.stack(new_pool_s))
```

```python
import functools
import math

import numpy as np
import jax
import jax.numpy as jnp
from jax import lax
from jax.experimental import pallas as pl
from jax.experimental.pallas import tpu as pltpu

F32 = jnp.float32
BF16 = jnp.bfloat16
RMS_EPS = 1e-6
POOL_WINDOWS = (2, 4, 8, 16)
SSM_GROUP = 16
SSM_CHUNK = 16
GELU_C = math.sqrt(2.0 / math.pi)
VMEM_LIMIT = 60 * 1024 * 1024
MXU_COLS = 256


def _cparams(sem):
    return pltpu.CompilerParams(dimension_semantics=sem, vmem_limit_bytes=VMEM_LIMIT)


def _rms_bf16(x, g):
    inv = lax.rsqrt(jnp.mean(x * x, axis=-1, keepdims=True) + RMS_EPS)
    return (x * inv * g).astype(BF16)


def _dot(a, b):
    return jnp.dot(a, b, preferred_element_type=F32)


def _col_chunks(width, chunk):
    chunk = min(chunk, width)
    return [slice(c0, c0 + chunk) for c0 in range(0, width, chunk)]


def _ffn_body(xn_ref, h_ref, wg_ref, wu_ref, wd_ref, o_ref, *, res_chunks):
    j = pl.program_id(1)
    nj = pl.num_programs(1)

    @pl.when(j == 0)
    def _():
        o_ref[...] = jnp.zeros_like(o_ref)

    xn = xn_ref[...]
    mids = []
    for cs in _col_chunks(wg_ref.shape[1], MXU_COLS):
        a = _dot(xn, wg_ref[:, cs].astype(BF16))
        b = _dot(xn, wu_ref[:, cs].astype(BF16))
        mids.append((a * jax.nn.sigmoid(a) * b).astype(BF16))
    mid = jnp.concatenate(mids, axis=1)
    for cs in _col_chunks(o_ref.shape[1], 2 * MXU_COLS):
        o_ref[:, cs] += _dot(mid, wd_ref[:, cs].astype(BF16))

    cw = o_ref.shape[1] // res_chunks
    for c in range(res_chunks):
        @pl.when(j == nj - res_chunks + c)
        def _(c=c):
            o_ref[:, c * cw:(c + 1) * cw] += 2.0 * h_ref[...]

    @pl.when(j == nj - 1)
    def _():
        o_ref[...] = 0.5 * o_ref[...]


def _ffn(xn, hres, wg, wu, wd, layer, tm, tf):
    m, d = xn.shape
    f = wg.shape[-1]
    nj = f // tf
    res_chunks = min(4, nj)
    return pl.pallas_call(
        functools.partial(_ffn_body, res_chunks=res_chunks),
        grid=(m // tm, nj),
        in_specs=[
            pl.BlockSpec((tm, d), lambda i, j: (i, 0)),
            pl.BlockSpec((tm, d // res_chunks),
                         lambda i, j: (i, jnp.clip(j - (nj - res_chunks), 0, res_chunks - 1))),
            pl.BlockSpec((None, d, tf), lambda i, j: (layer, 0, j)),
            pl.BlockSpec((None, d, tf), lambda i, j: (layer, 0, j)),
            pl.BlockSpec((None, tf, d), lambda i, j: (layer, j, 0)),
        ],
        out_specs=pl.BlockSpec((tm, d), lambda i, j: (i, 0), pipeline_mode=pl.Buffered(1)),
        out_shape=jax.ShapeDtypeStruct((m, d), F32),
        compiler_params=_cparams(("parallel", "arbitrary")),
        name="ffn",
    )(xn, hres, wg, wu, wd)


def _inproj_body(x_ref, g_ref, w_ref, o_ref, *, gate_from):
    is_gate = pl.program_id(0) >= gate_from
    xn = _rms_bf16(x_ref[...], g_ref[...])
    for cs in _col_chunks(o_ref.shape[1], MXU_COLS):
        r = _dot(xn, w_ref[:, cs].astype(BF16))
        o_ref[:, cs] = jnp.where(is_gate, jax.nn.sigmoid(r), r)


def _inproj(x, g, w, layer, tm, tn, n_plain):
    m, d = x.shape
    n = w.shape[-1]
    return pl.pallas_call(
        functools.partial(_inproj_body, gate_from=n_plain // tn),
        grid=(n // tn, m // tm),
        in_specs=[
            pl.BlockSpec((tm, d), lambda j, i: (i, 0)),
            pl.BlockSpec((None, 1, d), lambda j, i: (layer, 0, 0)),
            pl.BlockSpec((None, d, tn), lambda j, i: (layer, 0, j), pipeline_mode=pl.Buffered(1)),
        ],
        out_specs=pl.BlockSpec((tm, tn), lambda j, i: (i, j)),
        out_shape=jax.ShapeDtypeStruct((m, n), F32),
        compiler_params=_cparams(("arbitrary", "arbitrary")),
        name="inproj",
    )(x, g, w)


def _cmul(ar, ai, br, bi):
    return ar * br - ai * bi, ar * bi + ai * br


def _ssm_weights_body(ar_ref, ai_ref, ldt_ref, btr_ref, bti_ref, ctr_ref, cti_ref, e_col, e_row,
                      tt_o, wor_o, woi_o, wstr_o, wsti_o, scr_o, sci_o, *, chunk, gblk):
    h = SSM_GROUP
    causal = e_col[...] >= e_row[...]
    nt = (((1,), (1,)), ((), ()))
    hi = lax.Precision.HIGHEST

    def rows(pows):
        width = pows[0][0].shape[1]
        return tuple(jnp.concatenate([jnp.broadcast_to(x[k], (h, width)) for x in pows], axis=0)
                     for k in (0, 1))

    for gl in range(gblk):
        dt = jnp.exp(ldt_ref[gl])
        a_re, a_im = ar_ref[gl], ai_ref[gl]
        mag = jnp.exp(a_re * dt)
        ang = a_im * dt
        lr, li = mag * jnp.cos(ang), mag * jnp.sin(ang)
        den = a_re * a_re + a_im * a_im
        num_re = lr - 1.0
        k_re = (num_re * a_re + li * a_im) / den
        k_im = (li * a_re - num_re * a_im) / den
        inv = 1.0 / (lr * lr + li * li)
        nr, ni = lr * inv, -li * inv
        pw = [(jnp.ones_like(lr), jnp.zeros_like(lr))]
        npw = list(pw)
        for _ in range(chunk):
            pw.append(_cmul(*pw[-1], lr, li))
            npw.append(_cmul(*npw[-1], nr, ni))
        tile = lambda x: jnp.concatenate([x] * chunk, axis=0)
        kb = _cmul(k_re, k_im, tile(btr_ref[gl]), tile(bti_ref[gl]))
        c = (tile(ctr_ref[gl]), tile(cti_ref[gl]))

        l_re, l_im = _cmul(*c, *rows(pw[:chunk]))
        r_re, r_im = _cmul(*rows(npw[:chunk]), *kb)
        kmat = (lax.dot_general(l_re, r_re, nt, precision=hi, preferred_element_type=F32)
                - lax.dot_general(l_im, r_im, nt, precision=hi, preferred_element_type=F32))
        tt_o[gl] = jnp.where(causal, kmat, 0.0).astype(BF16)

        e_re, e_im = _cmul(*c, *rows(pw[1:chunk + 1]))
        wor_o[gl] = e_re.astype(BF16)
        woi_o[gl] = (-e_im).astype(BF16)

        s_re, s_im = _cmul(*rows(pw[chunk - 1::-1]), *kb)
        wstr_o[gl] = s_re.astype(BF16)
        wsti_o[gl] = s_im.astype(BF16)

        sc = [pw[chunk]]
        for _ in range(6):
            sc.append(_cmul(*sc[-1], *sc[-1]))
        sc.append(pw[chunk // 2])
        scr_o[gl] = jnp.concatenate([x[0] for x in sc], axis=0)
        sci_o[gl] = jnp.concatenate([x[1] for x in sc], axis=0)


def _ssm_weights(a_re, a_im, log_dt, b_re, b_im, c_re, c_im, chunk):
    dg, p = a_re.shape
    h = b_re.shape[-1]
    th = chunk * h
    tau = np.repeat(np.arange(chunk, dtype=np.float32), h)
    e_col = jnp.asarray(tau.reshape(th, 1))
    e_row = jnp.asarray(tau.reshape(1, th))
    gblk = min(16, dg)
    row = lambda x: x.reshape(dg, 1, p)
    bt = lambda x: jnp.swapaxes(x, 1, 2)
    per_g = lambda *s: pl.BlockSpec((gblk,) + s, lambda g: (g,) + (0,) * len(s))
    const = lambda *s: pl.BlockSpec(s, lambda g: (0,) * len(s))
    tt, wor, woi, wstr, wsti, scr, sci = pl.pallas_call(
        functools.partial(_ssm_weights_body, chunk=chunk, gblk=gblk),
        grid=(dg // gblk,),
        in_specs=[per_g(1, p), per_g(1, p), per_g(1, 1),
                  per_g(h, p), per_g(h, p), per_g(h, p), per_g(h, p),
                  const(th, 1), const(1, th)],
        out_specs=[per_g(th, th), per_g(th, p), per_g(th, p), per_g(th, p), per_g(th, p),
                   per_g(8, p), per_g(8, p)],
        out_shape=[jax.ShapeDtypeStruct((dg, th, th), BF16),
                   jax.ShapeDtypeStruct((dg, th, p), BF16),
                   jax.ShapeDtypeStruct((dg, th, p), BF16),
                   jax.ShapeDtypeStruct((dg, th, p), BF16),
                   jax.ShapeDtypeStruct((dg, th, p), BF16),
                   jax.ShapeDtypeStruct((dg, 8, p), F32),
                   jax.ShapeDtypeStruct((dg, 8, p), F32)],
        compiler_params=_cparams(("parallel",)),
        name="ssm_weights",
    )(row(a_re), row(a_im), log_dt.reshape(dg, 1, 1),
      bt(b_re), bt(b_im), c_re, c_im, e_col, e_row)
    sw = lambda x: jnp.swapaxes(x, 1, 2)
    return tt, wor, woi, sw(wstr), sw(wsti), sw(scr), sw(sci)


def _ssm_body(*refs, steps, chunks, gpb, has_init):
    if has_init:
        (u_ref, d_ref, tt_ref, wor_ref, woi_ref, wsr_ref, wsi_ref, scr_ref, sci_ref,
         s0r_ref, s0i_ref, y_ref, sfr_ref, sfi_ref, ys_ref) = refs
    else:
        (u_ref, d_ref, tt_ref, wor_ref, woi_ref, wsr_ref, wsi_ref, scr_ref, sci_ref,
         y_ref, sfr_ref, sfi_ref, ys_ref) = refs
    h = SSM_GROUP
    th = steps * h
    w = u_ref.shape[0] // steps
    p = scr_ref.shape[1]
    ws_off = wsr_ref.shape[2] - th

    def step_rows(t):
        return pl.ds(t, w, stride=steps) if chunks > 1 else pl.ds(t * w, w)

    slabs = [u_ref[step_rows(t), :].T for t in range(steps)]
    for gl in range(gpb):
        rows = slice(gl * h, (gl + 1) * h)
        u = jnp.concatenate([s[rows, :] for s in slabs], axis=0)
        ub = u.astype(BF16)
        y = _dot(tt_ref[gl, :th, :th], ub)
        xr = _dot(wsr_ref[gl, :, ws_off:], ub)
        xi = _dot(wsi_ref[gl, :, ws_off:], ub)
        if chunks > 1:
            c_idx = lax.broadcasted_iota(jnp.int32, (p, w), 1) & (chunks - 1)
            sr, si = xr, xi
            k = 0
            while (1 << k) < chunks:
                sh = 1 << k
                rr = pltpu.roll(sr, sh, axis=1)
                ri = pltpu.roll(si, sh, axis=1)
                mr = scr_ref[gl, :, k:k + 1]
                mi = sci_ref[gl, :, k:k + 1]
                keep = c_idx >= sh
                sr = sr + jnp.where(keep, mr * rr - mi * ri, 0.0)
                si = si + jnp.where(keep, mr * ri + mi * rr, 0.0)
                k += 1
            first = c_idx >= 1
            pr = jnp.where(first, pltpu.roll(sr, 1, axis=1), 0.0)
            pi = jnp.where(first, pltpu.roll(si, 1, axis=1), 0.0)
            for n in range(w // chunks):
                last = n * chunks + chunks - 1
                sfr_ref[gl, :, n:n + 1] = sr[:, last:last + 1]
                sfi_ref[gl, :, n:n + 1] = si[:, last:last + 1]
        else:
            pr, pi = s0r_ref[gl], s0i_ref[gl]
            lr = scr_ref[gl, :, 7:8]
            li = sci_ref[gl, :, 7:8]
            sfr_ref[gl] = lr * pr - li * pi + xr
            sfi_ref[gl] = lr * pi + li * pr + xi
        y = y + _dot(wor_ref[gl, :th, :], pr.astype(BF16)) + _dot(woi_ref[gl, :th, :], pi.astype(BF16))
        y = y + d_ref[gl, :th, :] * u
        y = 0.5 * y * (1.0 + jnp.tanh(GELU_C * (y + 0.044715 * (y * y * y))))
        for t in range(steps):
            ys_ref[t, rows, :] = y[t * h:(t + 1) * h, :]
    for t in range(steps):
        y_ref[step_rows(t), :] = ys_ref[t].T


def _ssm(z, row_blk, rows, d_t, ops, layer, n_groups, steps, chunks, s0=None, gpb=8):
    tt, wor, woi, wsr, wsi, scr, sci = ops
    h = SSM_GROUP
    w = rows // steps
    nseq = w // chunks
    p = scr.shape[1]
    thf = tt.shape[1]
    base = layer * (n_groups // gpb)
    wblk = lambda *s: pl.BlockSpec((gpb,) + s, lambda g: (base + g,) + (0,) * len(s))
    in_specs = [pl.BlockSpec((rows, gpb * h), lambda g: (row_blk, g)),
                wblk(thf, 1), wblk(thf, thf), wblk(thf, p), wblk(thf, p), wblk(p, thf), wblk(p, thf),
                wblk(p, scr.shape[2]), wblk(p, scr.shape[2])]
    args = [z, d_t, tt, wor, woi, wsr, wsi, scr, sci]
    if s0 is not None:
        in_specs += [pl.BlockSpec((gpb, p, w), lambda g: (g, 0, 0))] * 2
        args += list(s0)
    sf_spec = pl.BlockSpec((gpb, p, nseq), lambda g: (g, 0, 0))
    return pl.pallas_call(
        functools.partial(_ssm_body, steps=steps, chunks=chunks, gpb=gpb, has_init=s0 is not None),
        grid=(n_groups // gpb,),
        in_specs=in_specs,
        out_specs=[pl.BlockSpec((rows, gpb * h), lambda g: (0, g)), sf_spec, sf_spec],
        out_shape=[jax.ShapeDtypeStruct((rows, n_groups * h), F32),
                   jax.ShapeDtypeStruct((n_groups, p, nseq), F32),
                   jax.ShapeDtypeStruct((n_groups, p, nseq), F32)],
        scratch_shapes=[pltpu.VMEM((steps, gpb * h, w), F32)],
        compiler_params=_cparams(("parallel",)),
        name="ssm_chunks" if s0 is None else "ssm_step",
    )(*args)


def _pool_seq_body(u_ref, w_ref, sc_ref, o_ref, z_ref):
    l, c = u_ref.shape
    pad = z_ref.shape[0] - l
    cg = c // len(POOL_WINDOWS)
    z_ref[:pad, :] = jnp.zeros((pad, c), F32)
    z_ref[pad:, :] = u_ref[...]
    pos1 = (lax.broadcasted_iota(jnp.int32, (l, 1), 0) + 1).astype(F32)
    for gi, win in enumerate(POOL_WINDOWS):
        cols = slice(gi * cg, (gi + 1) * cg)
        cur = z_ref[pad:, cols]
        tot = cur
        for k in range(1, win):
            tot = tot + z_ref[pad - k:pad - k + l, cols]
        inv_cnt = 1.0 / jnp.minimum(pos1, float(win))
        mixed = _dot((tot * inv_cnt - cur).astype(BF16), w_ref[gi].astype(BF16))
        o_ref[:, cols] = (mixed * sc_ref[:, cols]).astype(BF16)


def _pool_seq(z, w_pool, scale, layer, nseq, seqlen, width, col_blk):
    return pl.pallas_call(
        _pool_seq_body,
        grid=(nseq,),
        in_specs=[pl.BlockSpec((seqlen, width), lambda n: (n, col_blk)),
                  pl.BlockSpec((None,) + w_pool.shape[1:], lambda n: (layer, 0, 0, 0)),
                  pl.BlockSpec((None, 1, width), lambda n: (layer, 0, 0))],
        out_specs=pl.BlockSpec((seqlen, width), lambda n: (n, 0)),
        out_shape=jax.ShapeDtypeStruct((nseq * seqlen, width), BF16),
        scratch_shapes=[pltpu.VMEM((seqlen + 16, width), F32)],
        compiler_params=_cparams(("parallel",)),
        name="pool_seq",
    )(z, w_pool, scale)


def _pool_step_body(u_ref, prev_ref, w_ref, sc_ref, o_ref):
    n, buf, c = prev_ref.shape
    steps = u_ref.shape[0] // n
    cg = c // len(POOL_WINDOWS)

    def row(j, cols):
        return prev_ref[:, j, cols] if j < buf else u_ref[(j - buf) * n:(j - buf + 1) * n, cols]

    for gi, win in enumerate(POOL_WINDOWS):
        cols = slice(gi * cg, (gi + 1) * cg)
        for t in range(steps):
            cur = row(buf + t, cols)
            tot = cur
            for k in range(1, win):
                tot = tot + row(buf + t - k, cols)
            mixed = _dot((tot * (1.0 / win) - cur).astype(BF16), w_ref[gi].astype(BF16))
            o_ref[t * n:(t + 1) * n, cols] = (mixed * sc_ref[:, cols]).astype(BF16)


def _pool_step(z, prev, w_pool, scale, layer, rows, col_blk):
    n, buf, c = prev.shape[1:]
    row_blk = z.shape[0] // rows - 1
    return pl.pallas_call(
        _pool_step_body,
        grid=(1,),
        in_specs=[pl.BlockSpec((rows, c), lambda i: (row_blk, col_blk)),
                  pl.BlockSpec((None, n, buf, c), lambda i: (layer, 0, 0, 0)),
                  pl.BlockSpec((None,) + w_pool.shape[1:], lambda i: (layer, 0, 0, 0)),
                  pl.BlockSpec((None, 1, c), lambda i: (layer, 0, 0))],
        out_specs=pl.BlockSpec((rows, c), lambda i: (0, 0)),
        out_shape=jax.ShapeDtypeStruct((rows, c), BF16),
        compiler_params=_cparams(("arbitrary",)),
        name="pool_step",
    )(z, prev, w_pool, scale)


def _mix_body(gap_ref, gas_ref, ybp_ref, ybs_ref, sa_ref, sb_ref, wa_ref, wb_ref, wp_ref, o_ref, *, p_tiles):
    first = pl.program_id(0) < p_tiles
    ga = jnp.where(first, gap_ref[...], gas_ref[...]).astype(BF16)
    yb =jnp.where(first, ybp_ref[...], ybs_ref[...])
    for cs in _col_chunks(o_ref.shape[1], MXU_COLS):
        br_a = _dot(ga, wa_ref[:, cs].astype(BF16)) * jax.nn.sigmoid(_dot(ga, wb_ref[:, cs].astype(BF16)))
        br_b = _dot(yb, wp_ref[:, cs].astype(BF16))
        o_ref[:, cs] = (sa_ref[:, cs] * br_a + sb_ref[:, cs] * br_b).astype(BF16)


def _resident(shape, layer):
    return pl.BlockSpec((None,) + shape, lambda i: (layer,) + (0,) * len(shape), pipeline_mode=pl.Buffered(1))


def _split_specs(tm, width, p_tiles):
    return [pl.BlockSpec((tm, width), lambda i: (jnp.minimum(i, p_tiles - 1), 0)),
            pl.BlockSpec((tm, width), lambda i: (jnp.maximum(i - p_tiles, 0), 0))]


def _mix(ga_p, ga_s, yb_p, yb_s, z, wa, wb, wp, layer, tm, gate_col):
    m = z.shape[0]
    k = ga_p.shape[1]
    n = wa.shape[-1]
    gblk = gate_col // n
    p_tiles = ga_p.shape[0] // tm
    return pl.pallas_call(
        functools.partial(_mix_body, p_tiles=p_tiles),
        grid=(m // tm,),
        in_specs=_split_specs(tm, k, p_tiles) + _split_specs(tm, k, p_tiles) + [
                  pl.BlockSpec((tm, n), lambda i: (i, gblk)),
                  pl.BlockSpec((tm, n), lambda i: (i, gblk + 1)),
                  _resident((k, n), layer), _resident((k, n), layer), _resident((k, n), layer)],
        out_specs=pl.BlockSpec((tm, n), lambda i: (i, 0)),
        out_shape=jax.ShapeDtypeStruct((m, n), BF16),
        compiler_params=_cparams(("parallel",)),
        name="mix",
    )(ga_p, ga_s, yb_p, yb_s, z, z, wa, wb, wp)


def _resmm_body(a_ref, w_ref, h_ref, g_ref, o_ref, xn_ref):
    a = a_ref[...]
    for cs in _col_chunks(o_ref.shape[1], MXU_COLS):
        o_ref[:, cs] = h_ref[:, cs] + _dot(a, w_ref[:, cs].astype(BF16))
    xn_ref[...] = _rms_bf16(o_ref[...], g_ref[...])


def _resmm(a, w, hres, g_next, layer, tm):
    m, k = a.shape
    n = w.shape[-1]
    return pl.pallas_call(
        _resmm_body,
        grid=(m // tm,),
        in_specs=[pl.BlockSpec((tm, k), lambda i: (i, 0)),
                  _resident((k, n), layer),
                  pl.BlockSpec((tm, n), lambda i: (i, 0)),
                  pl.BlockSpec((None, 1, n), lambda i: (layer, 0, 0))],
        out_specs=[pl.BlockSpec((tm, n), lambda i: (i, 0)), pl.BlockSpec((tm, n), lambda i: (i, 0))],
        out_shape=[jax.ShapeDtypeStruct((m, n), F32), jax.ShapeDtypeStruct((m, n), BF16)],
        compiler_params=_cparams(("parallel",)),
        name="resmm",
    )(a, w, hres, g_next)


def _ple_update(x_ref, g_ref, pp_ref, ps_ref, wp_ref, wg_ref, dst_ref, p_tiles):
    xn = _rms_bf16(x_ref[...], g_ref[...])
    pb = jnp.where(pl.program_id(0) < p_tiles, pp_ref[...], ps_ref[...]).astype(BF16)
    for cs in _col_chunks(dst_ref.shape[1], MXU_COLS):
        gate = jax.nn.sigmoid(_dot(xn, wg_ref[:, cs].astype(BF16)))
        dst_ref[:, cs] = x_ref[:, cs] + _dot(pb, wp_ref[:, cs].astype(BF16)) * gate


def _ple_body(x_ref, g_ref, pp_ref, ps_ref, wp_ref, wg_ref, gn_ref, o_ref, xn_ref, *, p_tiles):
    _ple_update(x_ref, g_ref, pp_ref, ps_ref, wp_ref, wg_ref, o_ref, p_tiles)
    xn_ref[...] = _rms_bf16(o_ref[...], gn_ref[...])


def _ple_final_body(x_ref, g_ref, pp_ref, ps_ref, wp_ref, wg_ref, gf_ref, op_ref, os_ref, h_ref, *, p_tiles):
    _ple_update(x_ref, g_ref, pp_ref, ps_ref, wp_ref, wg_ref, h_ref, p_tiles)
    h = h_ref[...]
    y = h * lax.rsqrt(jnp.mean(h * h, axis=-1, keepdims=True) + RMS_EPS) * gf_ref[...]
    i = pl.program_id(0)

    @pl.when(i < p_tiles)
    def _():
        op_ref[...] = y

    @pl.when(i >= p_tiles)
    def _():
        os_ref[...] = y


def _ple(x, g, p_p, p_s, wp, wg, layer, tm, g_next=None, g_final=None):
    m, d = x.shape
    mp, pd = p_p.shape[1:]
    p_tiles = mp // tm
    in_specs = [pl.BlockSpec((tm, d), lambda i: (i, 0)),
                pl.BlockSpec((None, 1, d), lambda i: (layer, 0, 0)),
                pl.BlockSpec((None, tm, pd), lambda i: (layer, jnp.minimum(i, p_tiles - 1), 0)),
                pl.BlockSpec((None, tm, pd), lambda i: (layer, jnp.maximum(i - p_tiles, 0), 0)),
                _resident((pd, d), layer), _resident((d, d), layer)]
    if g_final is None:
        row = pl.BlockSpec((tm, d), lambda i: (i, 0))
        return pl.pallas_call(
            functools.partial(_ple_body, p_tiles=p_tiles), grid=(m // tm,),
            in_specs=in_specs + [pl.BlockSpec((None, 1, d), lambda i: (layer + 1, 0, 0))],
            out_specs=[row, row],
            out_shape=[jax.ShapeDtypeStruct((m, d), F32), jax.ShapeDtypeStruct((m, d), BF16)],
            compiler_params=_cparams(("parallel",)),
            name="ple",
        )(x, g, p_p, p_s, wp, wg, g_next)
    return pl.pallas_call(
        functools.partial(_ple_final_body, p_tiles=p_tiles),
        grid=(m // tm,),
        in_specs=in_specs + [pl.BlockSpec((1, d), lambda i: (0, 0))],
        out_specs=[pl.BlockSpec((tm, d), lambda i: (jnp.minimum(i, p_tiles - 1), 0)),
                   pl.BlockSpec((tm, d), lambda i: (jnp.maximum(i - p_tiles, 0), 0))],
        out_shape=[jax.ShapeDtypeStruct((mp, d), F32), jax.ShapeDtypeStruct((m - mp, d), F32)],
        scratch_shapes=[pltpu.VMEM((tm, d), F32)],
        compiler_params=_cparams(("arbitrary",)),
        name="ple_final",
    )(x, g, p_p, p_s, wp, wg, g_final)


def _prep_body(xp_ref, xs_ref, g_ref, h_ref, xn_ref, *, p_tiles):
    x = jnp.where(pl.program_id(0) < p_tiles, xp_ref[...], xs_ref[...])
    h_ref[...] = x
    xn_ref[...] = _rms_bf16(x, g_ref[...])


def _prep(x_p, x_s, g, tm):
    d = x_p.shape[1]
    m = x_p.shape[0] + x_s.shape[0]
    p_tiles = x_p.shape[0] // tm
    row = pl.BlockSpec((tm, d), lambda i: (i, 0))
    return pl.pallas_call(
        functools.partial(_prep_body, p_tiles=p_tiles),
        grid=(m // tm,),
        in_specs=_split_specs(tm, d, p_tiles) + [pl.BlockSpec((None, 1, d), lambda i: (0, 0, 0))],
        out_specs=[row, row],
        out_shape=[jax.ShapeDtypeStruct((m, d), F32), jax.ShapeDtypeStruct((m, d), BF16)],
        compiler_params=_cparams(("parallel",)),
        name="prep",
    )(x_p, x_s, g)


def _pick_tile(n, pref):
    t = min(pref, n)
    while n % t:
        t //= 2
    return t


def kernel(x_prompt, x_sample, state_ssm_re, state_ssm_im, state_pool, p_prompt, p_sample, g_ffn1, w_ffn1_gate, w_ffn1_up, w_ffn1_down, g_mix, w_in, ssm_a_re, ssm_a_im, ssm_log_dt, ssm_b_re, ssm_b_im, ssm_c_re, ssm_c_im, ssm_d, w_glu_a, w_glu_b, w_pool, pool_scale, w_pool_up, w_out, g_ffn2, w_ffn2_gate, w_ffn2_up, w_ffn2_down, g_ple, w_ple, w_ple_gate, g_final):
    nb, seq, d = x_prompt.shape
    ns, dseq, _ = x_sample.shape
    depth, n_groups, p_state = ssm_a_re.shape
    h = ssm_b_re.shape[-1]
    sw = n_groups * h
    pw = pool_scale.shape[-1]
    buf = state_pool.shape[2]
    chunk = SSM_CHUNK
    assert h == SSM_GROUP and dseq * 2 == chunk and seq % chunk == 0 and buf == max(POOL_WINDOWS) - 1
    n_chunks = seq // chunk
    assert n_chunks & (n_chunks - 1) == 0 and n_chunks <= 128
    mp, ms = nb * seq, ns * dseq
    m = mp + ms
    tm = _pick_tile(m, 1024)

    g3 = lambda a: a.reshape(depth, 1, -1)
    wg1, wu1, wd1 = w_ffn1_gate, w_ffn1_up, w_ffn1_down
    wg2, wu2, wd2 = w_ffn2_gate, w_ffn2_up, w_ffn2_down
    wga, wgb, wpu, wo = w_glu_a, w_glu_b, w_pool_up, w_out
    wpl, wpg, wpool = w_ple, w_ple_gate, w_pool
    tn_in = _pick_tile(sw + pw, 2048)
    gf1, gmx, gf2, gpl = g3(g_ffn1), g3(g_mix), g3(g_ffn2), g3(g_ple)
    pscale = g3(pool_scale)

    flat = lambda a: a.reshape((depth * n_groups,) + a.shape[2:])
    ops = _ssm_weights(flat(ssm_a_re), flat(ssm_a_im), flat(ssm_log_dt), flat(ssm_b_re), flat(ssm_b_im),
                       flat(ssm_c_re), flat(ssm_c_im), chunk)
    d_t = jnp.tile(flat(ssm_d), (1, chunk)).reshape(depth * n_groups, chunk * h, 1)

    p_p = p_prompt.reshape(depth, mp, -1)
    p_s = p_sample.transpose(0, 2, 1, 3).reshape(depth, ms, -1)
    tf = _pick_tile(w_ffn1_gate.shape[-1], 512)
    tm2 = _pick_tile(math.gcd(mp, ms), 512)
    gpb = min(8, n_groups)
    hcur, xn = _prep(x_prompt.reshape(mp, d), x_sample.transpose(1, 0, 2).reshape(ms, d), gf1, tm2)
    new_re_p, new_im_p, new_pool_p, new_re_s, new_im_s, new_pool_s = [], [], [], [], [], []
    for i in range(depth):
        h1 = _ffn(xn, hcur, wg1, wu1, wd1, i, tm, tf)
        z = _inproj(h1, gmx, w_in, i, tm2, tn_in, sw + pw)

        ga_p, sr_p, si_p = _ssm(z, 0, mp, d_t, ops, i, n_groups, chunk, n_chunks, gpb=gpb)
        s0 = (state_ssm_re[i].transpose(1, 2, 0), state_ssm_im[i].transpose(1, 2, 0))
        ga_s, sr_s, si_s = _ssm(z, mp // ms, ms, d_t, ops, i, n_groups, dseq, 1, s0=s0, gpb=gpb)
        new_re_p.append(sr_p.transpose(2, 0, 1))
        new_im_p.append(si_p.transpose(2, 0, 1))
        new_re_s.append(sr_s.transpose(2, 0, 1))
        new_im_s.append(si_s.transpose(2, 0, 1))

        yb_p = _pool_seq(z, wpool, pscale, i, nb, seq, pw, sw // pw)
        yb_s = _pool_step(z, state_pool, wpool, pscale, i, ms, sw // pw)
        ub_s = z[mp:, sw:sw + pw].reshape(dseq, ns, pw).transpose(1, 0, 2)
        new_pool_p.append(jnp.stack([z[(n + 1) * seq - buf:(n + 1) * seq, sw:sw + pw] for n in range(nb)]))
        new_pool_s.append(jnp.concatenate([state_pool[i][:, dseq:, :], ub_s], axis=1))

        merged = _mix(ga_p, ga_s, yb_p, yb_s, z, wga, wgb, wpu, i, tm2, sw + pw)
        h2, xn2 = _resmm(merged, wo, h1, gf2, i, tm2)
        h3 = _ffn(xn2, h2, wg2, wu2, wd2, i, tm, tf)
        if i + 1 < depth:
            hcur, xn = _ple(h3, gpl, p_p, p_s, wpl, wpg, i, tm2, g_next=gf1)
        else:
            y_p, y_s = _ple(h3, gpl, p_p, p_s, wpl, wpg, i, tm2, g_final=g_final.reshape(1, d))

    return (y_p.reshape(nb, seq, d), y_s.reshape(dseq, ns, d).transpose(1, 0, 2),
            jnp.stack(new_re_p), jnp.stack(new_im_p), jnp.stack(new_pool_p),
            jnp.stack(new_re_s), jnp.stack(new_im_s), jnp.stack(new_pool_s))
```

```python
import functools
import math

import numpy as np
import jax
import jax.numpy as jnp
from jax import lax
from jax.experimental import pallas as pl
from jax.experimental.pallas import tpu as pltpu

F32 = jnp.float32
BF16 = jnp.bfloat16
RMS_EPS = 1e-6
POOL_WINDOWS = (2, 4, 8, 16)
SSM_GROUP = 16
SSM_CHUNK = 16
GELU_C = math.sqrt(2.0 / math.pi)
VMEM_LIMIT = 62 * 1024 * 1024
MXU_COLS = 256


def _cparams(sem):
    return pltpu.CompilerParams(dimension_semantics=sem, vmem_limit_bytes=VMEM_LIMIT)


def _rms_bf16(x, g):
    inv = lax.rsqrt(jnp.mean(x * x, axis=-1, keepdims=True) + RMS_EPS)
    return (x * inv * g).astype(BF16)


def _dot(a, b):
    return jnp.dot(a, b, preferred_element_type=F32)


def _col_chunks(width, chunk):
    chunk = min(chunk, width)
    return [slice(c0, c0 + chunk) for c0 in range(0, width, chunk)]


def _ffn_body(xn_ref, h_ref, wg_ref, wu_ref, wd_ref, *rest, res_chunks):
    gn_ref, o_ref, xo_ref = rest if len(rest) == 3 else (None,) + rest + (None,)
    j = pl.program_id(1)
    nj = pl.num_programs(1)

    @pl.when(j == 0)
    def _():
        o_ref[...] = jnp.zeros_like(o_ref)

    xn = xn_ref[...]
    mids = []
    for cs in _col_chunks(wg_ref.shape[1], MXU_COLS):
        a = _dot(xn, wg_ref[:, cs].astype(BF16))
        b = _dot(xn, wu_ref[:, cs].astype(BF16))
        mids.append((a * jax.nn.sigmoid(a) * b).astype(BF16))
    mid = jnp.concatenate(mids, axis=1)
    for cs in _col_chunks(o_ref.shape[1], 2 * MXU_COLS):
        o_ref[:, cs] += _dot(mid, wd_ref[:, cs].astype(BF16))

    cw = o_ref.shape[1] // res_chunks
    for c in range(res_chunks):
        @pl.when(j == nj - res_chunks + c)
        def _(c=c):
            o_ref[:, c * cw:(c + 1) * cw] += 2.0 * h_ref[...]

    @pl.when(j == nj - 1)
    def _():
        o_ref[...] = 0.5 * o_ref[...]
        if xo_ref is not None:
            xo_ref[...] = _rms_bf16(o_ref[...], gn_ref[...])


def _ffn(xn, hres, wg, wu, wd, layer, tm, tf, g_next=None):
    m, d = xn.shape
    f = wg.shape[-1]
    nj = f // tf
    res_chunks = min(4, nj)
    row = lambda: pl.BlockSpec((tm, d), lambda i, j: (i, 0), pipeline_mode=pl.Buffered(1))
    in_specs = [
        pl.BlockSpec((tm, d), lambda i, j: (i, 0)),
        pl.BlockSpec((tm, d // res_chunks),
                     lambda i, j: (i, jnp.clip(j - (nj - res_chunks), 0, res_chunks - 1))),
        pl.BlockSpec((None, d, tf), lambda i, j: (layer, 0, j)),
        pl.BlockSpec((None, d, tf), lambda i, j: (layer, 0, j)),
        pl.BlockSpec((None, tf, d), lambda i, j: (layer, j, 0)),
    ]
    args = [xn, hres, wg, wu, wd]
    out_specs, out_shape = row(), jax.ShapeDtypeStruct((m, d), F32)
    if g_next is not None:
        in_specs.append(pl.BlockSpec((None, 1, d), lambda i, j: (layer, 0, 0)))
        args.append(g_next)
        out_specs, out_shape = [row(), row()], [out_shape, jax.ShapeDtypeStruct((m, d), BF16)]
    return pl.pallas_call(
        functools.partial(_ffn_body, res_chunks=res_chunks),
        grid=(m // tm, nj),
        in_specs=in_specs,
        out_specs=out_specs,
        out_shape=out_shape,
        compiler_params=_cparams(("parallel", "arbitrary")),
        name="ffn",
    )(*args)


def _inproj_body(xn_ref, w_ref, o_ref, *, gate_from):
    is_gate = pl.program_id(0) >= gate_from
    xn = xn_ref[...]
    for cs in _col_chunks(o_ref.shape[1], MXU_COLS):
        r = _dot(xn, w_ref[:, cs].astype(BF16))
        o_ref[:, cs] = jnp.where(is_gate, jax.nn.sigmoid(r), r)


def _inproj(xn, w, layer, tm, tn, n_plain):
    m, d = xn.shape
    n = w.shape[-1]
    return pl.pallas_call(
        functools.partial(_inproj_body, gate_from=n_plain // tn),
        grid=(n // tn, m // tm),
        in_specs=[
            pl.BlockSpec((tm, d), lambda j, i: (i, 0)),
            pl.BlockSpec((None, d, tn), lambda j, i: (layer, 0, j)),
        ],
        out_specs=pl.BlockSpec((tm, tn), lambda j, i: (i, j)),
        out_shape=jax.ShapeDtypeStruct((m, n), F32),
        compiler_params=_cparams(("arbitrary", "arbitrary")),
        name="inproj",
    )(xn, w)


def _cmul(ar, ai, br, bi):
    return ar * br - ai * bi, ar * bi + ai * br


def _ssm_weights_body(ar_ref, ai_ref, ldt_ref, btr_ref, bti_ref, ctr_ref, cti_ref, e_col, e_row,
                      tt_o, wor_o, woi_o, wstr_o, wsti_o, scr_o, sci_o, *, chunk, gblk):
    h = SSM_GROUP
    causal = e_col[...] >= e_row[...]
    nt = (((1,), (1,)), ((), ()))
    hi = lax.Precision.HIGHEST

    def rows(pows):
        width = pows[0][0].shape[1]
        return tuple(jnp.concatenate([jnp.broadcast_to(x[k], (h, width)) for x in pows], axis=0)
                     for k in (0, 1))

    for gl in range(gblk):
        dt = jnp.exp(ldt_ref[gl])
        a_re, a_im = ar_ref[gl], ai_ref[gl]
        mag = jnp.exp(a_re * dt)
        ang = a_im * dt
        lr, li = mag * jnp.cos(ang), mag * jnp.sin(ang)
        den = a_re * a_re + a_im * a_im
        num_re = lr - 1.0
        k_re = (num_re * a_re + li * a_im) / den
        k_im = (li * a_re - num_re * a_im) / den
        inv = 1.0 / (lr * lr + li * li)
        nr, ni = lr * inv, -li * inv
        pw = [(jnp.ones_like(lr), jnp.zeros_like(lr))]
        npw = list(pw)
        for _ in range(chunk):
            pw.append(_cmul(*pw[-1], lr, li))
            npw.append(_cmul(*npw[-1], nr, ni))
        tile = lambda x: jnp.concatenate([x] * chunk, axis=0)
        kb = _cmul(k_re, k_im, tile(btr_ref[gl]), tile(bti_ref[gl]))
        c = (tile(ctr_ref[gl]), tile(cti_ref[gl]))

        l_re, l_im = _cmul(*c, *rows(pw[:chunk]))
        r_re, r_im = _cmul(*rows(npw[:chunk]), *kb)
        kmat = (lax.dot_general(l_re, r_re, nt, precision=hi, preferred_element_type=F32)
                - lax.dot_general(l_im, r_im, nt, precision=hi, preferred_element_type=F32))
        tt_o[gl] = jnp.where(causal, kmat, 0.0).astype(BF16)

        e_re, e_im = _cmul(*c, *rows(pw[1:chunk + 1]))
        wor_o[gl] = e_re.astype(BF16)
        woi_o[gl] = (-e_im).astype(BF16)

        s_re, s_im = _cmul(*rows(pw[chunk - 1::-1]), *kb)
        wstr_o[gl] = s_re.astype(BF16)
        wsti_o[gl] = s_im.astype(BF16)

        sc = [pw[chunk]]
        for _ in range(6):
            sc.append(_cmul(*sc[-1], *sc[-1]))
        sc.append(pw[chunk // 2])
        scr_o[gl] = jnp.concatenate([x[0] for x in sc], axis=0)
        sci_o[gl] = jnp.concatenate([x[1] for x in sc], axis=0)


def _ssm_weights(a_re, a_im, log_dt, b_re, b_im, c_re, c_im, chunk):
    dg, p = a_re.shape
    h = b_re.shape[-1]
    th = chunk * h
    tau = np.repeat(np.arange(chunk, dtype=np.float32), h)
    e_col = jnp.asarray(tau.reshape(th, 1))
    e_row = jnp.asarray(tau.reshape(1, th))
    gblk = min(16, dg)
    row = lambda x: x.reshape(dg, 1, p)
    bt = lambda x: jnp.swapaxes(x, 1, 2)
    per_g = lambda *s: pl.BlockSpec((gblk,) + s, lambda g: (g,) + (0,) * len(s))
    const = lambda *s: pl.BlockSpec(s, lambda g: (0,) * len(s))
    tt, wor, woi, wstr, wsti, scr, sci = pl.pallas_call(
        functools.partial(_ssm_weights_body, chunk=chunk, gblk=gblk),
        grid=(dg // gblk,),
        in_specs=[per_g(1, p), per_g(1, p), per_g(1, 1),
                  per_g(h, p), per_g(h, p), per_g(h, p), per_g(h, p),
                  const(th, 1), const(1, th)],
        out_specs=[per_g(th, th), per_g(th, p), per_g(th, p), per_g(th, p), per_g(th, p),
                   per_g(8, p), per_g(8, p)],
        out_shape=[jax.ShapeDtypeStruct((dg, th, th), BF16),
                   jax.ShapeDtypeStruct((dg, th, p), BF16),
                   jax.ShapeDtypeStruct((dg, th, p), BF16),
                   jax.ShapeDtypeStruct((dg, th, p), BF16),
                   jax.ShapeDtypeStruct((dg, th, p), BF16),
                   jax.ShapeDtypeStruct((dg, 8, p), F32),
                   jax.ShapeDtypeStruct((dg, 8, p), F32)],
        compiler_params=_cparams(("parallel",)),
        name="ssm_weights",
    )(row(a_re), row(a_im), log_dt.reshape(dg, 1, 1),
      bt(b_re), bt(b_im), c_re, c_im, e_col, e_row)
    sw = lambda x: jnp.swapaxes(x, 1, 2)
    return tt, wor, woi, sw(wstr), sw(wsti), sw(scr), sw(sci)


def _ssm_body(*refs, steps, chunks, gpb, has_init):
    if has_init:
        (u_ref, d_ref, tt_ref, wor_ref, woi_ref, wsr_ref, wsi_ref, scr_ref, sci_ref,
         s0r_ref, s0i_ref, y_ref, sfr_ref, sfi_ref, ys_ref) = refs
    else:
        (u_ref, d_ref, tt_ref, wor_ref, woi_ref, wsr_ref, wsi_ref, scr_ref, sci_ref,
         y_ref, sfr_ref, sfi_ref, ys_ref) = refs
    h = SSM_GROUP
    th = steps * h
    w = u_ref.shape[0] // steps
    p = scr_ref.shape[1]
    ws_off = wsr_ref.shape[2] - th

    def step_rows(t):
        return pl.ds(t, w, stride=steps) if chunks > 1 else pl.ds(t * w, w)

    slabs = [u_ref[step_rows(t), :].T for t in range(steps)]
    for gl in range(gpb):
        rows = slice(gl * h, (gl + 1) * h)
        u = jnp.concatenate([s[rows, :] for s in slabs], axis=0)
        ub = u.astype(BF16)
        y = _dot(tt_ref[gl, :th, :th], ub)
        xr = _dot(wsr_ref[gl, :, ws_off:], ub)
        xi = _dot(wsi_ref[gl, :, ws_off:], ub)
        if chunks > 1:
            c_idx = lax.broadcasted_iota(jnp.int32, (p, w), 1) & (chunks - 1)

            def shift(x, sh):
                return jnp.concatenate([pltpu.roll(x[:, q:q + chunks], sh, axis=1)
                                        for q in range(0, w, chunks)], axis=1)

            sr, si = xr, xi
            k = 0
            while (1 << k) < chunks:
                sh = 1 << k
                rr = shift(sr, sh)
                ri = shift(si, sh)
                mr = scr_ref[gl, :, k:k + 1]
                mi = sci_ref[gl, :, k:k + 1]
                keep = c_idx >= sh
                sr = sr + jnp.where(keep, mr * rr - mi * ri, 0.0)
                si = si + jnp.where(keep, mr * ri + mi * rr, 0.0)
                k += 1
            first = c_idx >= 1
            pr = jnp.where(first, shift(sr, 1), 0.0)
            pi = jnp.where(first, shift(si, 1), 0.0)
            for n in range(w // chunks):
                last = n * chunks + chunks - 1
                sfr_ref[gl, :, n:n + 1] = sr[:, last:last + 1]
                sfi_ref[gl, :, n:n + 1] = si[:, last:last + 1]
        else:
            pr, pi = s0r_ref[gl], s0i_ref[gl]
            lr = scr_ref[gl, :, 7:8]
            li = sci_ref[gl, :, 7:8]
            sfr_ref[gl] = lr * pr - li * pi + xr
            sfi_ref[gl] = lr * pi + li * pr + xi
        y = y + _dot(wor_ref[gl, :th, :], pr.astype(BF16)) + _dot(woi_ref[gl, :th, :], pi.astype(BF16))
        y = y + d_ref[gl, :th, :] * u
        y = 0.5 * y * (1.0 + jnp.tanh(GELU_C * (y + 0.044715 * (y * y * y))))
        for t in range(steps):
            ys_ref[t, rows, :] = y[t * h:(t + 1) * h, :]
    for t in range(steps):
        y_ref[step_rows(t), :] = ys_ref[t].T


def _ssm(z, row_blk, rows, d_t, ops, layer, n_groups, steps, chunks, s0=None, gpb=8):
    tt, wor, woi, wsr, wsi, scr, sci = ops
    h = SSM_GROUP
    w = rows // steps
    nseq = w // chunks
    p = scr.shape[1]
    thf = tt.shape[1]
    base = layer * (n_groups // gpb)
    wblk = lambda *s: pl.BlockSpec((gpb,) + s, lambda g: (base + g,) + (0,) * len(s))
    in_specs = [pl.BlockSpec((rows, gpb * h), lambda g: (row_blk, g)),
                wblk(thf, 1), wblk(thf, thf), wblk(thf, p), wblk(thf, p), wblk(p, thf), wblk(p, thf),
                wblk(p, scr.shape[2]), wblk(p, scr.shape[2])]
    args = [z, d_t, tt, wor, woi, wsr, wsi, scr, sci]
    if s0 is not None:
        in_specs += [pl.BlockSpec((gpb, p, w), lambda g: (g, 0, 0))] * 2
        args += list(s0)
    sf_spec = pl.BlockSpec((gpb, p, nseq), lambda g: (g, 0, 0))
    return pl.pallas_call(
        functools.partial(_ssm_body, steps=steps, chunks=chunks, gpb=gpb, has_init=s0 is not None),
        grid=(n_groups // gpb,),
        in_specs=in_specs,
        out_specs=[pl.BlockSpec((rows, gpb * h), lambda g: (0, g)), sf_spec, sf_spec],
        out_shape=[jax.ShapeDtypeStruct((rows, n_groups * h), F32),
                   jax.ShapeDtypeStruct((n_groups, p, nseq), F32),
                   jax.ShapeDtypeStruct((n_groups, p, nseq), F32)],
        scratch_shapes=[pltpu.VMEM((steps, gpb * h, w), F32)],
        compiler_params=_cparams(("parallel",)),
        name="ssm_chunks" if s0 is None else "ssm_step",
    )(*args)


def _pool_seq_body(u_ref, w_ref, sc_ref, o_ref, z_ref):
    l, c = u_ref.shape
    pad = z_ref.shape[0] - l
    cg = c // len(POOL_WINDOWS)
    z_ref[:pad, :] = jnp.zeros((pad, c), F32)
    z_ref[pad:, :] = u_ref[...]
    pos1 = (lax.broadcasted_iota(jnp.int32, (l, 1), 0) + 1).astype(F32)
    for gi, win in enumerate(POOL_WINDOWS):
        cols = slice(gi * cg, (gi + 1) * cg)
        cur = z_ref[pad:, cols]
        tot = cur
        for k in range(1, win):
            tot = tot + z_ref[pad - k:pad - k + l, cols]
        inv_cnt = 1.0 / jnp.minimum(pos1, float(win))
        mixed = _dot((tot * inv_cnt - cur).astype(BF16), w_ref[gi].astype(BF16))
        o_ref[:, cols] = (mixed * sc_ref[:, cols]).astype(BF16)


def _pool_seq(z, w_pool, scale, layer, nseq, seqlen, width, col_blk):
    return pl.pallas_call(
        _pool_seq_body,
        grid=(nseq,),
        in_specs=[pl.BlockSpec((seqlen, width), lambda n: (n, col_blk)),
                  pl.BlockSpec((None,) + w_pool.shape[1:], lambda n: (layer, 0, 0, 0)),
                  pl.BlockSpec((None, 1, width), lambda n: (layer, 0, 0))],
        out_specs=pl.BlockSpec((seqlen, width), lambda n: (n, 0)),
        out_shape=jax.ShapeDtypeStruct((nseq * seqlen, width), BF16),
        scratch_shapes=[pltpu.VMEM((seqlen + 16, width), F32)],
        compiler_params=_cparams(("parallel",)),
        name="pool_seq",
    )(z, w_pool, scale)


def _pool_step_body(u_ref, prev_ref, w_ref, sc_ref, o_ref):
    n, buf, c = prev_ref.shape
    steps = u_ref.shape[0] // n
    cg = c // len(POOL_WINDOWS)

    def row(j, cols):
        return prev_ref[:, j, cols] if j < buf else u_ref[(j - buf) * n:(j - buf + 1) * n, cols]

    for gi, win in enumerate(POOL_WINDOWS):
        cols = slice(gi * cg, (gi + 1) * cg)
        for t in range(steps):
            cur = row(buf + t, cols)
            tot = cur
            for k in range(1, win):
                tot = tot + row(buf + t - k, cols)
            mixed = _dot((tot * (1.0 / win) - cur).astype(BF16), w_ref[gi].astype(BF16))
            o_ref[t * n:(t + 1) * n, cols] = (mixed * sc_ref[:, cols]).astype(BF16)


def _pool_step(z, prev, w_pool, scale, layer, rows, col_blk):
    n, buf, c = prev.shape[1:]
    row_blk = z.shape[0] // rows - 1
    return pl.pallas_call(
        _pool_step_body,
        grid=(1,),
        in_specs=[pl.BlockSpec((rows, c), lambda i: (row_blk, col_blk)),
                  pl.BlockSpec((None, n, buf, c), lambda i: (layer, 0, 0, 0)),
                  pl.BlockSpec((None,) + w_pool.shape[1:], lambda i: (layer, 0, 0, 0)),
                  pl.BlockSpec((None, 1, c), lambda i: (layer, 0, 0))],
        out_specs=pl.BlockSpec((rows, c), lambda i: (0, 0)),
        out_shape=jax.ShapeDtypeStruct((rows, c), BF16),
        compiler_params=_cparams(("arbitrary",)),
        name="pool_step",
    )(z, prev, w_pool, scale)


def _mix_body(gap_ref, gas_ref, ybp_ref, ybs_ref, sa_ref, sb_ref, wa_ref, wb_ref, wp_ref, o_ref, *, p_tiles):
    first = pl.program_id(0) < p_tiles
    ga = jnp.where(first, gap_ref[...], gas_ref[...]).astype(BF16)
    yb =jnp.where(first, ybp_ref[...], ybs_ref[...])
    for cs in _col_chunks(o_ref.shape[1], MXU_COLS):
        br_a = _dot(ga, wa_ref[:, cs].astype(BF16)) * jax.nn.sigmoid(_dot(ga, wb_ref[:, cs].astype(BF16)))
        br_b = _dot(yb, wp_ref[:, cs].astype(BF16))
        o_ref[:, cs] = (sa_ref[:, cs] * br_a + sb_ref[:, cs] * br_b).astype(BF16)


def _resident(shape, layer):
    return pl.BlockSpec((None,) + shape, lambda i: (layer,) + (0,) * len(shape), pipeline_mode=pl.Buffered(1))


def _split_specs(tm, width, p_tiles):
    return [pl.BlockSpec((tm, width), lambda i: (jnp.minimum(i, p_tiles - 1), 0)),
            pl.BlockSpec((tm, width), lambda i: (jnp.maximum(i - p_tiles, 0), 0))]


def _mix(ga_p, ga_s, yb_p, yb_s, z, wa, wb, wp, layer, tm, gate_col):
    m = z.shape[0]
    k = ga_p.shape[1]
    n = wa.shape[-1]
    gblk = gate_col // n
    p_tiles = ga_p.shape[0] // tm
    return pl.pallas_call(
        functools.partial(_mix_body, p_tiles=p_tiles),
        grid=(m // tm,),
        in_specs=_split_specs(tm, k, p_tiles) + _split_specs(tm, k, p_tiles) + [
                  pl.BlockSpec((tm, n), lambda i: (i, gblk)),
                  pl.BlockSpec((tm, n), lambda i: (i, gblk + 1)),
                  _resident((k, n), layer), _resident((k, n), layer), _resident((k, n), layer)],
        out_specs=pl.BlockSpec((tm, n), lambda i: (i, 0)),
        out_shape=jax.ShapeDtypeStruct((m, n), BF16),
        compiler_params=_cparams(("parallel",)),
        name="mix",
    )(ga_p, ga_s, yb_p, yb_s, z, z, wa, wb, wp)


def _resmm_body(a_ref, w_ref, h_ref, g_ref, o_ref, xn_ref):
    a = a_ref[...]
    for cs in _col_chunks(o_ref.shape[1], MXU_COLS):
        o_ref[:, cs] = h_ref[:, cs] + _dot(a, w_ref[:, cs].astype(BF16))
    xn_ref[...] = _rms_bf16(o_ref[...], g_ref[...])


def _resmm(a, w, hres, g_next, layer, tm):
    m, k = a.shape
    n = w.shape[-1]
    return pl.pallas_call(
        _resmm_body,
        grid=(m // tm,),
        in_specs=[pl.BlockSpec((tm, k), lambda i: (i, 0)),
                  _resident((k, n), layer),
                  pl.BlockSpec((tm, n), lambda i: (i, 0)),
                  pl.BlockSpec((None, 1, n), lambda i: (layer, 0, 0))],
        out_specs=[pl.BlockSpec((tm, n), lambda i: (i, 0)), pl.BlockSpec((tm, n), lambda i: (i, 0))],
        out_shape=[jax.ShapeDtypeStruct((m, n), F32), jax.ShapeDtypeStruct((m, n), BF16)],
        compiler_params=_cparams(("parallel",)),
        name="resmm",
    )(a, w, hres, g_next)


def _ple_update(x_ref, g_ref, pp_ref, ps_ref, wp_ref, wg_ref, dst_ref, p_tiles):
    xn = _rms_bf16(x_ref[...], g_ref[...])
    pb = jnp.where(pl.program_id(0) < p_tiles, pp_ref[...], ps_ref[...]).astype(BF16)
    for cs in _col_chunks(dst_ref.shape[1], MXU_COLS):
        gate = jax.nn.sigmoid(_dot(xn, wg_ref[:, cs].astype(BF16)))
        dst_ref[:, cs] = x_ref[:, cs] + _dot(pb, wp_ref[:, cs].astype(BF16)) * gate


def _ple_body(x_ref, g_ref, pp_ref, ps_ref, wp_ref, wg_ref, gn_ref, o_ref, xn_ref, *, p_tiles):
    _ple_update(x_ref, g_ref, pp_ref, ps_ref, wp_ref, wg_ref, o_ref, p_tiles)
    xn_ref[...] = _rms_bf16(o_ref[...], gn_ref[...])


def _ple_final_body(x_ref, g_ref, pp_ref, ps_ref, wp_ref, wg_ref, gf_ref, op_ref, os_ref, h_ref, *, p_tiles):
    _ple_update(x_ref, g_ref, pp_ref, ps_ref, wp_ref, wg_ref, h_ref, p_tiles)
    h = h_ref[...]
    y = h * lax.rsqrt(jnp.mean(h * h, axis=-1, keepdims=True) + RMS_EPS) * gf_ref[...]
    i = pl.program_id(0)

    @pl.when(i < p_tiles)
    def _():
        op_ref[...] = y

    @pl.when(i >= p_tiles)
    def _():
        os_ref[...] = y


def _ple(x, g, p_p, p_s, wp, wg, layer, tm, g_next=None, g_final=None):
    m, d = x.shape
    mp, pd = p_p.shape[1:]
    p_tiles = mp // tm
    in_specs = [pl.BlockSpec((tm, d), lambda i: (i, 0)),
                pl.BlockSpec((None, 1, d), lambda i: (layer, 0, 0)),
                pl.BlockSpec((None, tm, pd), lambda i: (layer, jnp.minimum(i, p_tiles - 1), 0)),
                pl.BlockSpec((None, tm, pd), lambda i: (layer, jnp.maximum(i - p_tiles, 0), 0)),
                _resident((pd, d), layer), _resident((d, d), layer)]
    if g_final is None:
        row = pl.BlockSpec((tm, d), lambda i: (i, 0))
        return pl.pallas_call(
            functools.partial(_ple_body, p_tiles=p_tiles), grid=(m // tm,),
            in_specs=in_specs + [pl.BlockSpec((None, 1, d), lambda i: (layer + 1, 0, 0))],
            out_specs=[row, row],
            out_shape=[jax.ShapeDtypeStruct((m, d), F32), jax.ShapeDtypeStruct((m, d), BF16)],
            compiler_params=_cparams(("parallel",)),
            name="ple",
        )(x, g, p_p, p_s, wp, wg, g_next)
    return pl.pallas_call(
        functools.partial(_ple_final_body, p_tiles=p_tiles),
        grid=(m // tm,),
        in_specs=in_specs + [pl.BlockSpec((1, d), lambda i: (0, 0))],
        out_specs=[pl.BlockSpec((tm, d), lambda i: (jnp.minimum(i, p_tiles - 1), 0)),
                   pl.BlockSpec((tm, d), lambda i: (jnp.maximum(i - p_tiles, 0), 0))],
        out_shape=[jax.ShapeDtypeStruct((mp, d), F32), jax.ShapeDtypeStruct((m - mp, d), F32)],
        scratch_shapes=[pltpu.VMEM((tm, d), F32)],
        compiler_params=_cparams(("arbitrary",)),
        name="ple_final",
    )(x, g, p_p, p_s, wp, wg, g_final)


def _prep_body(xp_ref, xs_ref, g_ref, h_ref, xn_ref, *, p_tiles):
    x = jnp.where(pl.program_id(0) < p_tiles, xp_ref[...], xs_ref[...])
    h_ref[...] = x
    xn_ref[...] = _rms_bf16(x, g_ref[...])


def _prep(x_p, x_s, g, tm):
    d = x_p.shape[1]
    m = x_p.shape[0] + x_s.shape[0]
    p_tiles = x_p.shape[0] // tm
    row = pl.BlockSpec((tm, d), lambda i: (i, 0))
    return pl.pallas_call(
        functools.partial(_prep_body, p_tiles=p_tiles),
        grid=(m // tm,),
        in_specs=_split_specs(tm, d, p_tiles) + [pl.BlockSpec((None, 1, d), lambda i: (0, 0, 0))],
        out_specs=[row, row],
        out_shape=[jax.ShapeDtypeStruct((m, d), F32), jax.ShapeDtypeStruct((m, d), BF16)],
        compiler_params=_cparams(("parallel",)),
        name="prep",
    )(x_p, x_s, g)


def _pick_tile(n, pref):
    t = min(pref, n)
    while n % t:
        t //= 2
    return t


def kernel(x_prompt, x_sample, state_ssm_re, state_ssm_im, state_pool, p_prompt, p_sample, g_ffn1, w_ffn1_gate, w_ffn1_up, w_ffn1_down, g_mix, w_in, ssm_a_re, ssm_a_im, ssm_log_dt, ssm_b_re, ssm_b_im, ssm_c_re, ssm_c_im, ssm_d, w_glu_a, w_glu_b, w_pool, pool_scale, w_pool_up, w_out, g_ffn2, w_ffn2_gate, w_ffn2_up, w_ffn2_down, g_ple, w_ple, w_ple_gate, g_final):
    nb, seq, d = x_prompt.shape
    ns, dseq, _ = x_sample.shape
    depth, n_groups, p_state = ssm_a_re.shape
    h = ssm_b_re.shape[-1]
    sw = n_groups * h
    pw = pool_scale.shape[-1]
    buf = state_pool.shape[2]
    chunk = SSM_CHUNK
    assert h == SSM_GROUP and dseq * 2 == chunk and seq % chunk == 0 and buf == max(POOL_WINDOWS) - 1
    n_chunks = seq // chunk
    assert n_chunks & (n_chunks - 1) == 0 and n_chunks <= 128
    mp, ms = nb * seq, ns * dseq
    m = mp + ms
    tm = _pick_tile(m, 1024)

    g3 = lambda a: a.reshape(depth, 1, -1)
    wg1, wu1, wd1 = w_ffn1_gate, w_ffn1_up, w_ffn1_down
    wg2, wu2, wd2 = w_ffn2_gate, w_ffn2_up, w_ffn2_down
    wga, wgb, wpu, wo = w_glu_a, w_glu_b, w_pool_up, w_out
    wpl, wpg, wpool = w_ple, w_ple_gate, w_pool
    tn_in = _pick_tile(sw + pw, 1024)
    gf1, gmx, gf2, gpl = g3(g_ffn1), g3(g_mix), g3(g_ffn2), g3(g_ple)
    pscale = g3(pool_scale)

    flat = lambda a: a.reshape((depth * n_groups,) + a.shape[2:])
    ops = _ssm_weights(flat(ssm_a_re), flat(ssm_a_im), flat(ssm_log_dt), flat(ssm_b_re), flat(ssm_b_im),
                       flat(ssm_c_re), flat(ssm_c_im), chunk)
    d_t = jnp.tile(flat(ssm_d), (1, chunk)).reshape(depth * n_groups, chunk * h, 1)

    p_p = p_prompt.reshape(depth, mp, -1)
    p_s = p_sample.transpose(0, 2, 1, 3).reshape(depth, ms, -1)
    tf = _pick_tile(w_ffn1_gate.shape[-1], 512)
    tm2 = _pick_tile(math.gcd(mp, ms), 512)
    gpb = min(8, n_groups)
    hcur, xn = _prep(x_prompt.reshape(mp, d), x_sample.transpose(1, 0, 2).reshape(ms, d), gf1, tm2)
    new_re_p, new_im_p, new_pool_p, new_re_s, new_im_s, new_pool_s = [], [], [], [], [], []
    for i in range(depth):
        h1, xn_mix = _ffn(xn, hcur, wg1, wu1, wd1, i, tm, tf, g_next=gmx)
        z = _inproj(xn_mix, w_in, i, tm2, tn_in, sw + pw)

        ga_p, sr_p, si_p = _ssm(z, 0, mp, d_t, ops, i, n_groups, chunk, n_chunks, gpb=gpb)
        s0 = (state_ssm_re[i].transpose(1, 2, 0), state_ssm_im[i].transpose(1, 2, 0))
        ga_s, sr_s, si_s = _ssm(z, mp // ms, ms, d_t, ops, i, n_groups, dseq, 1, s0=s0, gpb=gpb)
        new_re_p.append(sr_p.transpose(2, 0, 1))
        new_im_p.append(si_p.transpose(2, 0, 1))
        new_re_s.append(sr_s.transpose(2, 0, 1))
        new_im_s.append(si_s.transpose(2, 0, 1))

        yb_p = _pool_seq(z, wpool, pscale, i, nb, seq, pw, sw // pw)
        yb_s = _pool_step(z, state_pool, wpool, pscale, i, ms, sw // pw)
        ub_s = z[mp:, sw:sw + pw].reshape(dseq, ns, pw).transpose(1, 0, 2)
        new_pool_p.append(jnp.stack([z[(n + 1) * seq - buf:(n + 1) * seq, sw:sw + pw] for n in range(nb)]))
        new_pool_s.append(jnp.concatenate([state_pool[i][:, dseq:, :], ub_s], axis=1))

        merged = _mix(ga_p, ga_s, yb_p, yb_s, z, wga, wgb, wpu, i, tm2, sw + pw)
        h2, xn2 = _resmm(merged, wo, h1, gf2, i, tm2)
        h3 = _ffn(xn2, h2, wg2, wu2, wd2, i, tm, tf)
        if i + 1 < depth:
            hcur, xn = _ple(h3, gpl, p_p, p_s, wpl, wpg, i, tm2, g_next=gf1)
        else:
            y_p, y_s = _ple(h3, gpl, p_p, p_s, wpl, wpg, i, tm2, g_final=g_final.reshape(1, d))

    return (y_p.reshape(nb, seq, d), y_s.reshape(dseq, ns, d).transpose(1, 0, 2),
            jnp.stack(new_re_p), jnp.stack(new_im_p), jnp.stack(new_pool_p),
            jnp.stack(new_re_s), jnp.stack(new_im_s), jnp.stack(new_pool_s))
```

```python
import functools
import math

import numpy as np
import jax
import jax.numpy as jnp
from jax import lax
from jax.experimental import pallas as pl
from jax.experimental.pallas import tpu as pltpu

F32 = jnp.float32
BF16 = jnp.bfloat16
RMS_EPS = 1e-6
POOL_WINDOWS = (2, 4, 8, 16)
SSM_GROUP = 16
SSM_CHUNK = 16
GELU_C = math.sqrt(2.0 / math.pi)
VMEM_LIMIT = 62 * 1024 * 1024
MXU_COLS = 256


def _cparams(sem):
    return pltpu.CompilerParams(dimension_semantics=sem, vmem_limit_bytes=VMEM_LIMIT)


def _rms_bf16(x, g):
    inv = lax.rsqrt(jnp.mean(x * x, axis=-1, keepdims=True) + RMS_EPS)
    return (x * inv * g).astype(BF16)


def _dot(a, b):
    return jnp.dot(a, b, preferred_element_type=F32)


def _col_chunks(width, chunk):
    chunk = min(chunk, width)
    return [slice(c0, c0 + chunk) for c0 in range(0, width, chunk)]


def _ffn_body(xn_ref, h_ref, wg_ref, wu_ref, wd_ref, *rest, res_chunks):
    gn_ref, o_ref, xo_ref = rest if len(rest) == 3 else (None,) + rest + (None,)
    j = pl.program_id(1)
    nj = pl.num_programs(1)

    @pl.when(j == 0)
    def _():
        o_ref[...] = jnp.zeros_like(o_ref)

    xn = xn_ref[...]
    mids = []
    for cs in _col_chunks(wg_ref.shape[1], MXU_COLS):
        a = _dot(xn, wg_ref[:, cs].astype(BF16))
        b = _dot(xn, wu_ref[:, cs].astype(BF16))
        mids.append((a * jax.nn.sigmoid(a) * b).astype(BF16))
    mid = jnp.concatenate(mids, axis=1)
    for cs in _col_chunks(o_ref.shape[1], 2 * MXU_COLS):
        o_ref[:, cs] += _dot(mid, wd_ref[:, cs].astype(BF16))

    cw = o_ref.shape[1] // res_chunks
    for c in range(res_chunks):
        @pl.when(j == nj - res_chunks + c)
        def _(c=c):
            o_ref[:, c * cw:(c + 1) * cw] += 2.0 * h_ref[...]

    @pl.when(j == nj - 1)
    def _():
        o_ref[...] = 0.5 * o_ref[...]
        if xo_ref is not None:
            xo_ref[...] = _rms_bf16(o_ref[...], gn_ref[...])


def _ffn(xn, hres, wg, wu, wd, layer, tm, tf, g_next=None):
    m, d = xn.shape
    f = wg.shape[-1]
    nj = f // tf
    res_chunks = min(4, nj)
    row = lambda: pl.BlockSpec((tm, d), lambda i, j: (i, 0))
    in_specs = [
        pl.BlockSpec((tm, d), lambda i, j: (i, 0)),
        pl.BlockSpec((tm, d // res_chunks),
                     lambda i, j: (i, jnp.clip(j - (nj - res_chunks), 0, res_chunks - 1))),
        pl.BlockSpec((None, d, tf), lambda i, j: (layer, 0, j)),
        pl.BlockSpec((None, d, tf), lambda i, j: (layer, 0, j)),
        pl.BlockSpec((None, tf, d), lambda i, j: (layer, j, 0)),
    ]
    args = [xn, hres, wg, wu, wd]
    out_specs, out_shape = row(), jax.ShapeDtypeStruct((m, d), F32)
    if g_next is not None:
        in_specs.append(pl.BlockSpec((None, 1, d), lambda i, j: (layer, 0, 0)))
        args.append(g_next)
        out_specs, out_shape = [row(), row()], [out_shape, jax.ShapeDtypeStruct((m, d), BF16)]
    return pl.pallas_call(
        functools.partial(_ffn_body, res_chunks=res_chunks),
        grid=(m // tm, nj),
        in_specs=in_specs,
        out_specs=out_specs,
        out_shape=out_shape,
        compiler_params=_cparams(("parallel", "arbitrary")),
        name="ffn",
    )(*args)


def _inproj_body(x_ref, g_ref, w_ref, o_ref, *, gate_from):
    is_gate = pl.program_id(0) >= gate_from
    xn = _rms_bf16(x_ref[...], g_ref[...])
    for cs in _col_chunks(o_ref.shape[1], MXU_COLS):
        r = _dot(xn, w_ref[:, cs].astype(BF16))
        o_ref[:, cs] = jnp.where(is_gate, jax.nn.sigmoid(r), r)


def _inproj(x, g, w, layer, tm, tn, n_plain):
    m, d = x.shape
    n = w.shape[-1]
    return pl.pallas_call(
        functools.partial(_inproj_body, gate_from=n_plain // tn),
        grid=(n // tn, m // tm),
        in_specs=[
            pl.BlockSpec((tm, d), lambda j, i: (i, 0)),
            pl.BlockSpec((None, 1, d), lambda j, i: (layer, 0, 0)),
            pl.BlockSpec((None, d, tn), lambda j, i: (layer, 0, j), pipeline_mode=pl.Buffered(1)),
        ],
        out_specs=pl.BlockSpec((tm, tn), lambda j, i: (i, j)),
        out_shape=jax.ShapeDtypeStruct((m, n), F32),
        compiler_params=_cparams(("arbitrary", "arbitrary")),
        name="inproj",
    )(x, g, w)


def _cmul(ar, ai, br, bi):
    return ar * br - ai * bi, ar * bi + ai * br


def _dot_nt3(a, b):
    nt = (((1,), (1,)), ((), ()))
    dot = lambda x, y: lax.dot_general(x, y, nt, preferred_element_type=F32)
    ah, bh = a.astype(BF16), b.astype(BF16)
    al = (a - ah.astype(F32)).astype(BF16)
    bl = (b - bh.astype(F32)).astype(BF16)
    return dot(ah, bh) + dot(ah, bl) + dot(al, bh)


def _ssm_weights_body(ar_ref, ai_ref, ldt_ref, btr_ref, bti_ref, ctr_ref, cti_ref, e_col, e_row,
                      tt_o, wor_o, woi_o, wstr_o, wsti_o, scr_o, sci_o, *, chunk, gblk):
    h = SSM_GROUP
    causal = e_col[...] >= e_row[...]
    nt = (((1,), (1,)), ((), ()))
    hi = lax.Precision.HIGHEST

    def rows(pows):
        width = pows[0][0].shape[1]
        return tuple(jnp.concatenate([jnp.broadcast_to(x[k], (h, width)) for x in pows], axis=0)
                     for k in (0, 1))

    for gl in range(gblk):
        dt = jnp.exp(ldt_ref[gl])
        a_re, a_im = ar_ref[gl], ai_ref[gl]
        mag = jnp.exp(a_re * dt)
        ang = a_im * dt
        lr, li = mag * jnp.cos(ang), mag * jnp.sin(ang)
        den = a_re * a_re + a_im * a_im
        num_re = lr - 1.0
        k_re = (num_re * a_re + li * a_im) / den
        k_im = (li * a_re - num_re * a_im) / den
        inv = 1.0 / (lr * lr + li * li)
        nr, ni = lr * inv, -li * inv
        pw = [(jnp.ones_like(lr), jnp.zeros_like(lr))]
        npw = list(pw)
        for _ in range(chunk):
            pw.append(_cmul(*pw[-1], lr, li))
            npw.append(_cmul(*npw[-1], nr, ni))
        tile = lambda x: jnp.concatenate([x] * chunk, axis=0)
        kb = _cmul(k_re, k_im, tile(btr_ref[gl]), tile(bti_ref[gl]))
        c = (tile(ctr_ref[gl]), tile(cti_ref[gl]))

        l_re, l_im = _cmul(*c, *rows(pw[:chunk]))
        r_re, r_im = _cmul(*rows(npw[:chunk]), *kb)
        kmat = _dot_nt3(l_re, r_re) - _dot_nt3(l_im, r_im)
        tt_o[gl] = jnp.where(causal, kmat, 0.0).astype(BF16)

        e_re, e_im = _cmul(*c, *rows(pw[1:chunk + 1]))
        wor_o[gl] = e_re.astype(BF16)
        woi_o[gl] = (-e_im).astype(BF16)

        s_re, s_im = _cmul(*rows(pw[chunk - 1::-1]), *kb)
        wstr_o[gl] = s_re.astype(BF16)
        wsti_o[gl] = s_im.astype(BF16)

        sc = [pw[chunk]]
        for _ in range(6):
            sc.append(_cmul(*sc[-1], *sc[-1]))
        sc.append(pw[chunk // 2])
        scr_o[gl] = jnp.concatenate([x[0] for x in sc], axis=0)
        sci_o[gl] = jnp.concatenate([x[1] for x in sc], axis=0)


def _ssm_weights(a_re, a_im, log_dt, b_re, b_im, c_re, c_im, chunk):
    dg, p = a_re.shape
    h = b_re.shape[-1]
    th = chunk * h
    tau = np.repeat(np.arange(chunk, dtype=np.float32), h)
    e_col = jnp.asarray(tau.reshape(th, 1))
    e_row = jnp.asarray(tau.reshape(1, th))
    gblk = min(16, dg)
    row = lambda x: x.reshape(dg, 1, p)
    bt = lambda x: jnp.swapaxes(x, 1, 2)
    per_g = lambda *s: pl.BlockSpec((gblk,) + s, lambda g: (g,) + (0,) * len(s))
    const = lambda *s: pl.BlockSpec(s, lambda g: (0,) * len(s))
    tt, wor, woi, wstr, wsti, scr, sci = pl.pallas_call(
        functools.partial(_ssm_weights_body, chunk=chunk, gblk=gblk),
        grid=(dg // gblk,),
        in_specs=[per_g(1, p), per_g(1, p), per_g(1, 1),
                  per_g(h, p), per_g(h, p), per_g(h, p), per_g(h, p),
                  const(th, 1), const(1, th)],
        out_specs=[per_g(th, th), per_g(th, p), per_g(th, p), per_g(th, p), per_g(th, p),
                   per_g(8, p), per_g(8, p)],
        out_shape=[jax.ShapeDtypeStruct((dg, th, th), BF16),
                   jax.ShapeDtypeStruct((dg, th, p), BF16),
                   jax.ShapeDtypeStruct((dg, th, p), BF16),
                   jax.ShapeDtypeStruct((dg, th, p), BF16),
                   jax.ShapeDtypeStruct((dg, th, p), BF16),
                   jax.ShapeDtypeStruct((dg, 8, p), F32),
                   jax.ShapeDtypeStruct((dg, 8, p), F32)],
        compiler_params=_cparams(("parallel",)),
        name="ssm_weights",
    )(row(a_re), row(a_im), log_dt.reshape(dg, 1, 1),
      bt(b_re), bt(b_im), c_re, c_im, e_col, e_row)
    sw = lambda x: jnp.swapaxes(x, 1, 2)
    return tt, wor, woi, sw(wstr), sw(wsti), sw(scr), sw(sci)


def _ssm_body(*refs, steps, chunks, gpb, has_init):
    if has_init:
        (u_ref, d_ref, tt_ref, wor_ref, woi_ref, wsr_ref, wsi_ref, scr_ref, sci_ref,
         s0r_ref, s0i_ref, y_ref, sfr_ref, sfi_ref, ys_ref) = refs
    else:
        (u_ref, d_ref, tt_ref, wor_ref, woi_ref, wsr_ref, wsi_ref, scr_ref, sci_ref,
         y_ref, sfr_ref, sfi_ref, ys_ref) = refs
    h = SSM_GROUP
    th = steps * h
    w = u_ref.shape[0] // steps
    p = scr_ref.shape[1]
    ws_off = wsr_ref.shape[2] - th

    def step_rows(t):
        return pl.ds(t, w, stride=steps) if chunks > 1 else pl.ds(t * w, w)

    slabs = [u_ref[step_rows(t), :].T for t in range(steps)]
    for gl in range(gpb):
        rows = slice(gl * h, (gl + 1) * h)
        u = jnp.concatenate([s[rows, :] for s in slabs], axis=0)
        ub = u.astype(BF16)
        y = _dot(tt_ref[gl, :th, :th], ub)
        xr = _dot(wsr_ref[gl, :, ws_off:], ub)
        xi = _dot(wsi_ref[gl, :, ws_off:], ub)
        if chunks > 1:
            c_idx = lax.broadcasted_iota(jnp.int32, (p, w), 1) & (chunks - 1)

            def shift(x, sh):
                return jnp.concatenate([pltpu.roll(x[:, q:q + chunks], sh, axis=1)
                                        for q in range(0, w, chunks)], axis=1)

            sr, si = xr, xi
            k = 0
            while (1 << k) < chunks:
                sh = 1 << k
                rr = shift(sr, sh)
                ri = shift(si, sh)
                mr = scr_ref[gl, :, k:k + 1]
                mi = sci_ref[gl, :, k:k + 1]
                keep = c_idx >= sh
                sr = sr + jnp.where(keep, mr * rr - mi * ri, 0.0)
                si = si + jnp.where(keep, mr * ri + mi * rr, 0.0)
                k += 1
            first = c_idx >= 1
            pr = jnp.where(first, shift(sr, 1), 0.0)
            pi = jnp.where(first, shift(si, 1), 0.0)
            for n in range(w // chunks):
                last = n * chunks + chunks - 1
                sfr_ref[gl, :, n:n + 1] = sr[:, last:last + 1]
                sfi_ref[gl, :, n:n + 1] = si[:, last:last + 1]
        else:
            pr, pi = s0r_ref[gl], s0i_ref[gl]
            lr = scr_ref[gl, :, 7:8]
            li = sci_ref[gl, :, 7:8]
            sfr_ref[gl] = lr * pr - li * pi + xr
            sfi_ref[gl] = lr * pi + li * pr + xi
        y = y + _dot(wor_ref[gl, :th, :], pr.astype(BF16)) + _dot(woi_ref[gl, :th, :], pi.astype(BF16))
        y = y + d_ref[gl, :th, :] * u
        y = 0.5 * y * (1.0 + jnp.tanh(GELU_C * (y + 0.044715 * (y * y * y))))
        for t in range(steps):
            ys_ref[t, rows, :] = y[t * h:(t + 1) * h, :]
    for t in range(steps):
        y_ref[step_rows(t), :] = ys_ref[t].T


def _ssm(z, row_blk, rows, d_t, ops, layer, n_groups, steps, chunks, s0=None, gpb=8):
    tt, wor, woi, wsr, wsi, scr, sci = ops
    h = SSM_GROUP
    w = rows // steps
    nseq = w // chunks
    p = scr.shape[1]
    thf = tt.shape[1]
    base = layer * (n_groups // gpb)
    wblk = lambda *s: pl.BlockSpec((gpb,) + s, lambda g: (base + g,) + (0,) * len(s))
    in_specs = [pl.BlockSpec((rows, gpb * h), lambda g: (row_blk, g)),
                wblk(thf, 1), wblk(thf, thf), wblk(thf, p), wblk(thf, p), wblk(p, thf), wblk(p, thf),
                wblk(p, scr.shape[2]), wblk(p, scr.shape[2])]
    args = [z, d_t, tt, wor, woi, wsr, wsi, scr, sci]
    if s0 is not None:
        in_specs += [pl.BlockSpec((gpb, p, w), lambda g: (g, 0, 0))] * 2
        args += list(s0)
    sf_spec = pl.BlockSpec((gpb, p, nseq), lambda g: (g, 0, 0))
    return pl.pallas_call(
        functools.partial(_ssm_body, steps=steps, chunks=chunks, gpb=gpb, has_init=s0 is not None),
        grid=(n_groups // gpb,),
        in_specs=in_specs,
        out_specs=[pl.BlockSpec((rows, gpb * h), lambda g: (0, g)), sf_spec, sf_spec],
        out_shape=[jax.ShapeDtypeStruct((rows, n_groups * h), F32),
                   jax.ShapeDtypeStruct((n_groups, p, nseq), F32),
                   jax.ShapeDtypeStruct((n_groups, p, nseq), F32)],
        scratch_shapes=[pltpu.VMEM((steps, gpb * h, w), F32)],
        compiler_params=_cparams(("parallel",)),
        name="ssm_chunks" if s0 is None else "ssm_step",
    )(*args)


def _pool_seq_body(u_ref, w_ref, sc_ref, o_ref, z_ref):
    l, c = u_ref.shape
    pad = z_ref.shape[0] - l
    cg = c // len(POOL_WINDOWS)
    z_ref[:pad, :] = jnp.zeros((pad, c), F32)
    z_ref[pad:, :] = u_ref[...]
    pos1 = (lax.broadcasted_iota(jnp.int32, (l, 1), 0) + 1).astype(F32)
    for gi, win in enumerate(POOL_WINDOWS):
        cols = slice(gi * cg, (gi + 1) * cg)
        cur = z_ref[pad:, cols]
        tot = cur
        for k in range(1, win):
            tot = tot + z_ref[pad - k:pad - k + l, cols]
        inv_cnt = 1.0 / jnp.minimum(pos1, float(win))
        mixed = _dot((tot * inv_cnt - cur).astype(BF16), w_ref[gi].astype(BF16))
        o_ref[:, cols] = (mixed * sc_ref[:, cols]).astype(BF16)


def _pool_seq(z, w_pool, scale, layer, nseq, seqlen, width, col_blk):
    return pl.pallas_call(
        _pool_seq_body,
        grid=(nseq,),
        in_specs=[pl.BlockSpec((seqlen, width), lambda n: (n, col_blk)),
                  pl.BlockSpec((None,) + w_pool.shape[1:], lambda n: (layer, 0, 0, 0)),
                  pl.BlockSpec((None, 1, width), lambda n: (layer, 0, 0))],
        out_specs=pl.BlockSpec((seqlen, width), lambda n: (n, 0)),
        out_shape=jax.ShapeDtypeStruct((nseq * seqlen, width), BF16),
        scratch_shapes=[pltpu.VMEM((seqlen + 16, width), F32)],
        compiler_params=_cparams(("parallel",)),
        name="pool_seq",
    )(z, w_pool, scale)


def _pool_step_body(u_ref, prev_ref, w_ref, sc_ref, o_ref):
    n, buf, c = prev_ref.shape
    steps = u_ref.shape[0] // n
    cg = c // len(POOL_WINDOWS)

    def row(j, cols):
        return prev_ref[:, j, cols] if j < buf else u_ref[(j - buf) * n:(j - buf + 1) * n, cols]

    for gi, win in enumerate(POOL_WINDOWS):
        cols = slice(gi * cg, (gi + 1) * cg)
        for t in range(steps):
            cur = row(buf + t, cols)
            tot = cur
            for k in range(1, win):
                tot = tot + row(buf + t - k, cols)
            mixed = _dot((tot * (1.0 / win) - cur).astype(BF16), w_ref[gi].astype(BF16))
            o_ref[t * n:(t + 1) * n, cols] = (mixed * sc_ref[:, cols]).astype(BF16)


def _pool_step(z, prev, w_pool, scale, layer, rows, col_blk):
    n, buf, c = prev.shape[1:]
    row_blk = z.shape[0] // rows - 1
    return pl.pallas_call(
        _pool_step_body,
        grid=(1,),
        in_specs=[pl.BlockSpec((rows, c), lambda i: (row_blk, col_blk)),
                  pl.BlockSpec((None, n, buf, c), lambda i: (layer, 0, 0, 0)),
                  pl.BlockSpec((None,) + w_pool.shape[1:], lambda i: (layer, 0, 0, 0)),
                  pl.BlockSpec((None, 1, c), lambda i: (layer, 0, 0))],
        out_specs=pl.BlockSpec((rows, c), lambda i: (0, 0)),
        out_shape=jax.ShapeDtypeStruct((rows, c), BF16),
        compiler_params=_cparams(("arbitrary",)),
        name="pool_step",
    )(z, prev, w_pool, scale)


def _mix_body(gap_ref, gas_ref, ybp_ref, ybs_ref, sa_ref, sb_ref, wa_ref, wb_ref, wp_ref, o_ref, *, p_tiles):
    first = pl.program_id(0) < p_tiles
    ga = jnp.where(first, gap_ref[...], gas_ref[...]).astype(BF16)
    yb =jnp.where(first, ybp_ref[...], ybs_ref[...])
    for cs in _col_chunks(o_ref.shape[1], MXU_COLS):
        br_a = _dot(ga, wa_ref[:, cs].astype(BF16)) * jax.nn.sigmoid(_dot(ga, wb_ref[:, cs].astype(BF16)))
        br_b = _dot(yb, wp_ref[:, cs].astype(BF16))
        o_ref[:, cs] = (sa_ref[:, cs] * br_a + sb_ref[:, cs] * br_b).astype(BF16)


def _resident(shape, layer):
    return pl.BlockSpec((None,) + shape, lambda i: (layer,) + (0,) * len(shape), pipeline_mode=pl.Buffered(1))


def _split_specs(tm, width, p_tiles):
    return [pl.BlockSpec((tm, width), lambda i: (jnp.minimum(i, p_tiles - 1), 0)),
            pl.BlockSpec((tm, width), lambda i: (jnp.maximum(i - p_tiles, 0), 0))]


def _mix(ga_p, ga_s, yb_p, yb_s, z, wa, wb, wp, layer, tm, gate_col):
    m = z.shape[0]
    k = ga_p.shape[1]
    n = wa.shape[-1]
    gblk = gate_col // n
    p_tiles = ga_p.shape[0] // tm
    return pl.pallas_call(
        functools.partial(_mix_body, p_tiles=p_tiles),
        grid=(m // tm,),
        in_specs=_split_specs(tm, k, p_tiles) + _split_specs(tm, k, p_tiles) + [
                  pl.BlockSpec((tm, n), lambda i: (i, gblk)),
                  pl.BlockSpec((tm, n), lambda i: (i, gblk + 1)),
                  _resident((k, n), layer), _resident((k, n), layer), _resident((k, n), layer)],
        out_specs=pl.BlockSpec((tm, n), lambda i: (i, 0)),
        out_shape=jax.ShapeDtypeStruct((m, n), BF16),
        compiler_params=_cparams(("parallel",)),
        name="mix",
    )(ga_p, ga_s, yb_p, yb_s, z, z, wa, wb, wp)


def _resmm_body(a_ref, w_ref, h_ref, g_ref, o_ref, xn_ref):
    a = a_ref[...]
    for cs in _col_chunks(o_ref.shape[1], MXU_COLS):
        o_ref[:, cs] = h_ref[:, cs] + _dot(a, w_ref[:, cs].astype(BF16))
    xn_ref[...] = _rms_bf16(o_ref[...], g_ref[...])


def _resmm(a, w, hres, g_next, layer, tm):
    m, k = a.shape
    n = w.shape[-1]
    return pl.pallas_call(
        _resmm_body,
        grid=(m // tm,),
        in_specs=[pl.BlockSpec((tm, k), lambda i: (i, 0)),
                  _resident((k, n), layer),
                  pl.BlockSpec((tm, n), lambda i: (i, 0)),
                  pl.BlockSpec((None, 1, n), lambda i: (layer, 0, 0))],
        out_specs=[pl.BlockSpec((tm, n), lambda i: (i, 0)), pl.BlockSpec((tm, n), lambda i: (i, 0))],
        out_shape=[jax.ShapeDtypeStruct((m, n), F32), jax.ShapeDtypeStruct((m, n), BF16)],
        compiler_params=_cparams(("parallel",)),
        name="resmm",
    )(a, w, hres, g_next)


def _ple_update(x_ref, g_ref, pp_ref, ps_ref, wp_ref, wg_ref, dst_ref, p_tiles):
    xn = _rms_bf16(x_ref[...], g_ref[...])
    pb = jnp.where(pl.program_id(0) < p_tiles, pp_ref[...], ps_ref[...]).astype(BF16)
    for cs in _col_chunks(dst_ref.shape[1], MXU_COLS):
        gate = jax.nn.sigmoid(_dot(xn, wg_ref[:, cs].astype(BF16)))
        dst_ref[:, cs] = x_ref[:, cs] + _dot(pb, wp_ref[:, cs].astype(BF16)) * gate


def _ple_body(x_ref, g_ref, pp_ref, ps_ref, wp_ref, wg_ref, gn_ref, o_ref, xn_ref, *, p_tiles):
    _ple_update(x_ref, g_ref, pp_ref, ps_ref, wp_ref, wg_ref, o_ref, p_tiles)
    xn_ref[...] = _rms_bf16(o_ref[...], gn_ref[...])


def _ple_final_body(x_ref, g_ref, pp_ref, ps_ref, wp_ref, wg_ref, gf_ref, op_ref, os_ref, h_ref, *, p_tiles):
    _ple_update(x_ref, g_ref, pp_ref, ps_ref, wp_ref, wg_ref, h_ref, p_tiles)
    h = h_ref[...]
    y = h * lax.rsqrt(jnp.mean(h * h, axis=-1, keepdims=True) + RMS_EPS) * gf_ref[...]
    i = pl.program_id(0)

    @pl.when(i < p_tiles)
    def _():
        op_ref[...] = y

    @pl.when(i >= p_tiles)
    def _():
        os_ref[...] = y


def _ple(x, g, p_p, p_s, wp, wg, layer, tm, g_next=None, g_final=None):
    m, d = x.shape
    mp, pd = p_p.shape[1:]
    p_tiles = mp // tm
    in_specs = [pl.BlockSpec((tm, d), lambda i: (i, 0)),
                pl.BlockSpec((None, 1, d), lambda i: (layer, 0, 0)),
                pl.BlockSpec((None, tm, pd), lambda i: (layer, jnp.minimum(i, p_tiles - 1), 0)),
                pl.BlockSpec((None, tm, pd), lambda i: (layer, jnp.maximum(i - p_tiles, 0), 0)),
                _resident((pd, d), layer), _resident((d, d), layer)]
    if g_final is None:
        row = pl.BlockSpec((tm, d), lambda i: (i, 0))
        return pl.pallas_call(
            functools.partial(_ple_body, p_tiles=p_tiles), grid=(m // tm,),
            in_specs=in_specs + [pl.BlockSpec((None, 1, d), lambda i: (layer + 1, 0, 0))],
            out_specs=[row, row],
            out_shape=[jax.ShapeDtypeStruct((m, d), F32), jax.ShapeDtypeStruct((m, d), BF16)],
            compiler_params=_cparams(("parallel",)),
            name="ple",
        )(x, g, p_p, p_s, wp, wg, g_next)
    return pl.pallas_call(
        functools.partial(_ple_final_body, p_tiles=p_tiles),
        grid=(m // tm,),
        in_specs=in_specs + [pl.BlockSpec((1, d), lambda i: (0, 0))],
        out_specs=[pl.BlockSpec((tm, d), lambda i: (jnp.minimum(i, p_tiles - 1), 0)),
                   pl.BlockSpec((tm, d), lambda i: (jnp.maximum(i - p_tiles, 0), 0))],
        out_shape=[jax.ShapeDtypeStruct((mp, d), F32), jax.ShapeDtypeStruct((m - mp, d), F32)],
        scratch_shapes=[pltpu.VMEM((tm, d), F32)],
        compiler_params=_cparams(("arbitrary",)),
        name="ple_final",
    )(x, g, p_p, p_s, wp, wg, g_final)


def _prep_body(xp_ref, xs_ref, g_ref, h_ref, xn_ref, *, p_tiles):
    x = jnp.where(pl.program_id(0) < p_tiles, xp_ref[...], xs_ref[...])
    h_ref[...] = x
    xn_ref[...] = _rms_bf16(x, g_ref[...])


def _prep(x_p, x_s, g, tm):
    d = x_p.shape[1]
    m = x_p.shape[0] + x_s.shape[0]
    p_tiles = x_p.shape[0] // tm
    row = pl.BlockSpec((tm, d), lambda i: (i, 0))
    return pl.pallas_call(
        functools.partial(_prep_body, p_tiles=p_tiles),
        grid=(m // tm,),
        in_specs=_split_specs(tm, d, p_tiles) + [pl.BlockSpec((None, 1, d), lambda i: (0, 0, 0))],
        out_specs=[row, row],
        out_shape=[jax.ShapeDtypeStruct((m, d), F32), jax.ShapeDtypeStruct((m, d), BF16)],
        compiler_params=_cparams(("parallel",)),
        name="prep",
    )(x_p, x_s, g)


def _pick_tile(n, pref):
    t = min(pref, n)
    while n % t:
        t //= 2
    return t


def kernel(x_prompt, x_sample, state_ssm_re, state_ssm_im, state_pool, p_prompt, p_sample, g_ffn1, w_ffn1_gate, w_ffn1_up, w_ffn1_down, g_mix, w_in, ssm_a_re, ssm_a_im, ssm_log_dt, ssm_b_re, ssm_b_im, ssm_c_re, ssm_c_im, ssm_d, w_glu_a, w_glu_b, w_pool, pool_scale, w_pool_up, w_out, g_ffn2, w_ffn2_gate, w_ffn2_up, w_ffn2_down, g_ple, w_ple, w_ple_gate, g_final):
    nb, seq, d = x_prompt.shape
    ns, dseq, _ = x_sample.shape
    depth, n_groups, p_state = ssm_a_re.shape
    h = ssm_b_re.shape[-1]
    sw = n_groups * h
    pw = pool_scale.shape[-1]
    buf = state_pool.shape[2]
    chunk = SSM_CHUNK
    assert h == SSM_GROUP and dseq * 2 == chunk and seq % chunk == 0 and buf == max(POOL_WINDOWS) - 1
    n_chunks = seq // chunk
    assert n_chunks & (n_chunks - 1) == 0 and n_chunks <= 128
    mp, ms = nb * seq, ns * dseq
    m = mp + ms
    tm = _pick_tile(m, 1024)

    g3 = lambda a: a.reshape(depth, 1, -1)
    wg1, wu1, wd1 = w_ffn1_gate, w_ffn1_up, w_ffn1_down
    wg2, wu2, wd2 = w_ffn2_gate, w_ffn2_up, w_ffn2_down
    wga, wgb, wpu, wo = w_glu_a, w_glu_b, w_pool_up, w_out
    wpl, wpg, wpool = w_ple, w_ple_gate, w_pool
    tn_in = _pick_tile(sw + pw, 2048)
    gf1, gmx, gf2, gpl = g3(g_ffn1), g3(g_mix), g3(g_ffn2), g3(g_ple)
    pscale = g3(pool_scale)

    flat = lambda a: a.reshape((depth * n_groups,) + a.shape[2:])
    ops = _ssm_weights(flat(ssm_a_re), flat(ssm_a_im), flat(ssm_log_dt), flat(ssm_b_re), flat(ssm_b_im),
                       flat(ssm_c_re), flat(ssm_c_im), chunk)
    d_t = jnp.tile(flat(ssm_d), (1, chunk)).reshape(depth * n_groups, chunk * h, 1)

    p_p = p_prompt.reshape(depth, mp, -1)
    p_s = p_sample.transpose(0, 2, 1, 3).reshape(depth, ms, -1)
    tf = _pick_tile(w_ffn1_gate.shape[-1], 512)
    tm2 = _pick_tile(math.gcd(mp, ms), 512)
    gpb = min(8, n_groups)
    hcur, xn = _prep(x_prompt.reshape(mp, d), x_sample.transpose(1, 0, 2).reshape(ms, d), gf1, tm2)
    new_re_p, new_im_p, new_pool_p, new_re_s, new_im_s, new_pool_s = [], [], [], [], [], []
    for i in range(depth):
        h1 = _ffn(xn, hcur, wg1, wu1, wd1, i, tm, tf)
        z = _inproj(h1, gmx, w_in, i, tm2, tn_in, sw + pw)

        ga_p, sr_p, si_p = _ssm(z, 0, mp, d_t, ops, i, n_groups, chunk, n_chunks, gpb=gpb)
        s0 = (state_ssm_re[i].transpose(1, 2, 0), state_ssm_im[i].transpose(1, 2, 0))
        ga_s, sr_s, si_s = _ssm(z, mp // ms, ms, d_t, ops, i, n_groups, dseq, 1, s0=s0, gpb=gpb)
        new_re_p.append(sr_p.transpose(2, 0, 1))
        new_im_p.append(si_p.transpose(2, 0, 1))
        new_re_s.append(sr_s.transpose(2, 0, 1))
        new_im_s.append(si_s.transpose(2, 0, 1))

        yb_p = _pool_seq(z, wpool, pscale, i, nb, seq, pw, sw // pw)
        yb_s = _pool_step(z, state_pool, wpool, pscale, i, ms, sw // pw)
        ub_s = z[mp:, sw:sw + pw].reshape(dseq, ns, pw).transpose(1, 0, 2)
        new_pool_p.append(jnp.stack([z[(n + 1) * seq - buf:(n + 1) * seq, sw:sw + pw] for n in range(nb)]))
        new_pool_s.append(jnp.concatenate([state_pool[i][:, dseq:, :], ub_s], axis=1))

        merged = _mix(ga_p, ga_s, yb_p, yb_s, z, wga, wgb, wpu, i, tm2, sw + pw)
        h2, xn2 = _resmm(merged, wo, h1, gf2, i, tm2)
        h3 = _ffn(xn2, h2, wg2, wu2, wd2, i, tm, tf)
        if i + 1 < depth:
            hcur, xn = _ple(h3, gpl, p_p, p_s, wpl, wpg, i, tm2, g_next=gf1)
        else:
            y_p, y_s = _ple(h3, gpl, p_p, p_s, wpl, wpg, i, tm2, g_final=g_final.reshape(1, d))

    return (y_p.reshape(nb, seq, d), y_s.reshape(dseq, ns, d).transpose(1, 0, 2),
            jnp.stack(new_re_p), jnp.stack(new_im_p), jnp.stack(new_pool_p),
            jnp.stack(new_re_s), jnp.stack(new_im_s), jnp.stack(new_pool_s))
```

```python
import functools
import math

import numpy as np
import jax
import jax.numpy as jnp
from jax import lax
from jax.experimental import pallas as pl
from jax.experimental.pallas import tpu as pltpu

F32 = jnp.float32
BF16 = jnp.bfloat16
RMS_EPS = 1e-6
POOL_WINDOWS = (2, 4, 8, 16)
SSM_GROUP = 16
SSM_CHUNK = 16
GELU_C = math.sqrt(2.0 / math.pi)
VMEM_LIMIT = 62 * 1024 * 1024
MXU_COLS = 256


def _cparams(sem):
    return pltpu.CompilerParams(dimension_semantics=sem, vmem_limit_bytes=VMEM_LIMIT)


def _rms_bf16(x, g):
    inv = lax.rsqrt(jnp.mean(x * x, axis=-1, keepdims=True) + RMS_EPS)
    return (x * inv * g).astype(BF16)


def _dot(a, b):
    return jnp.dot(a, b, preferred_element_type=F32)


def _col_chunks(width, chunk):
    chunk = min(chunk, width)
    return [slice(c0, c0 + chunk) for c0 in range(0, width, chunk)]


def _ffn_body(xn_ref, h_ref, wg_ref, wu_ref, wd_ref, *rest, res_chunks):
    gn_ref, o_ref, xo_ref = rest if len(rest) == 3 else (None,) + rest + (None,)
    j = pl.program_id(1)
    nj = pl.num_programs(1)

    @pl.when(j == 0)
    def _():
        o_ref[...] = jnp.zeros_like(o_ref)

    xn = xn_ref[...]
    mids = []
    for cs in _col_chunks(wg_ref.shape[1], MXU_COLS):
        a = _dot(xn, wg_ref[:, cs].astype(BF16))
        b = _dot(xn, wu_ref[:, cs].astype(BF16))
        mids.append((a * jax.nn.sigmoid(a) * b).astype(BF16))
    mid = jnp.concatenate(mids, axis=1)
    for cs in _col_chunks(o_ref.shape[1], 2 * MXU_COLS):
        o_ref[:, cs] += _dot(mid, wd_ref[:, cs].astype(BF16))

    cw = o_ref.shape[1] // res_chunks
    for c in range(res_chunks):
        @pl.when(j == nj - res_chunks + c)
        def _(c=c):
            o_ref[:, c * cw:(c + 1) * cw] += 2.0 * h_ref[...]

    @pl.when(j == nj - 1)
    def _():
        o_ref[...] = 0.5 * o_ref[...]
        if xo_ref is not None:
            xo_ref[...] = _rms_bf16(o_ref[...], gn_ref[...])


def _ffn(xn, hres, wg, wu, wd, layer, tm, tf, g_next=None):
    m, d = xn.shape
    f = wg.shape[-1]
    nj = f // tf
    res_chunks = min(4, nj)
    row = lambda: pl.BlockSpec((tm, d), lambda i, j: (i, 0))
    in_specs = [
        pl.BlockSpec((tm, d), lambda i, j: (i, 0)),
        pl.BlockSpec((tm, d // res_chunks),
                     lambda i, j: (i, jnp.clip(j - (nj - res_chunks), 0, res_chunks - 1))),
        pl.BlockSpec((None, d, tf), lambda i, j: (layer, 0, j)),
        pl.BlockSpec((None, d, tf), lambda i, j: (layer, 0, j)),
        pl.BlockSpec((None, tf, d), lambda i, j: (layer, j, 0)),
    ]
    args = [xn, hres, wg, wu, wd]
    out_specs, out_shape = row(), jax.ShapeDtypeStruct((m, d), F32)
    if g_next is not None:
        in_specs.append(pl.BlockSpec((None, 1, d), lambda i, j: (layer, 0, 0)))
        args.append(g_next)
        out_specs, out_shape = [row(), row()], [out_shape, jax.ShapeDtypeStruct((m, d), BF16)]
    return pl.pallas_call(
        functools.partial(_ffn_body, res_chunks=res_chunks),
        grid=(m // tm, nj),
        in_specs=in_specs,
        out_specs=out_specs,
        out_shape=out_shape,
        compiler_params=_cparams(("parallel", "arbitrary")),
        name="ffn",
    )(*args)


def _inproj_body(x_ref, g_ref, w_ref, o_ref, *, gate_from):
    is_gate = pl.program_id(0) >= gate_from
    xn = _rms_bf16(x_ref[...], g_ref[...])
    for cs in _col_chunks(o_ref.shape[1], MXU_COLS):
        r = _dot(xn, w_ref[:, cs].astype(BF16))
        o_ref[:, cs] = jnp.where(is_gate, jax.nn.sigmoid(r), r)


def _inproj(x, g, w, layer, tm, tn, n_plain):
    m, d = x.shape
    n = w.shape[-1]
    return pl.pallas_call(
        functools.partial(_inproj_body, gate_from=n_plain // tn),
        grid=(n // tn, m // tm),
        in_specs=[
            pl.BlockSpec((tm, d), lambda j, i: (i, 0)),
            pl.BlockSpec((None, 1, d), lambda j, i: (layer, 0, 0)),
            pl.BlockSpec((None, d, tn), lambda j, i: (layer, 0, j), pipeline_mode=pl.Buffered(1)),
        ],
        out_specs=pl.BlockSpec((tm, tn), lambda j, i: (i, j)),
        out_shape=jax.ShapeDtypeStruct((m, n), F32),
        compiler_params=_cparams(("arbitrary", "arbitrary")),
        name="inproj",
    )(x, g, w)


def _cmul(ar, ai, br, bi):
    return ar * br - ai * bi, ar * bi + ai * br


def _dot_nt3(a, b):
    nt = (((1,), (1,)), ((), ()))
    dot = lambda x, y: lax.dot_general(x, y, nt, preferred_element_type=F32)
    ah, bh = a.astype(BF16), b.astype(BF16)
    al = (a - ah.astype(F32)).astype(BF16)
    bl = (b - bh.astype(F32)).astype(BF16)
    return dot(ah, bh) + dot(ah, bl) + dot(al, bh)


def _ssm_weights_body(ar_ref, ai_ref, ldt_ref, btr_ref, bti_ref, ctr_ref, cti_ref, e_col, e_row,
                      tt_o, wor_o, woi_o, wstr_o, wsti_o, scr_o, sci_o, *, chunk, gblk):
    h = SSM_GROUP
    causal = e_col[...] >= e_row[...]
    nt = (((1,), (1,)), ((), ()))
    hi = lax.Precision.HIGHEST

    def rows(pows):
        width = pows[0][0].shape[1]
        return tuple(jnp.concatenate([jnp.broadcast_to(x[k], (h, width)) for x in pows], axis=0)
                     for k in (0, 1))

    for gl in range(gblk):
        dt = jnp.exp(ldt_ref[gl])
        a_re, a_im = ar_ref[gl], ai_ref[gl]
        mag = jnp.exp(a_re * dt)
        ang = a_im * dt
        lr, li = mag * jnp.cos(ang), mag * jnp.sin(ang)
        den = a_re * a_re + a_im * a_im
        num_re = lr - 1.0
        k_re = (num_re * a_re + li * a_im) / den
        k_im = (li * a_re - num_re * a_im) / den
        inv = 1.0 / (lr * lr + li * li)
        nr, ni = lr * inv, -li * inv
        pw = [(jnp.ones_like(lr), jnp.zeros_like(lr))]
        npw = list(pw)
        for _ in range(chunk):
            pw.append(_cmul(*pw[-1], lr, li))
            npw.append(_cmul(*npw[-1], nr, ni))
        tile = lambda x: jnp.concatenate([x] * chunk, axis=0)
        kb = _cmul(k_re, k_im, tile(btr_ref[gl]), tile(bti_ref[gl]))
        c = (tile(ctr_ref[gl]), tile(cti_ref[gl]))

        l_re, l_im = _cmul(*c, *rows(pw[:chunk]))
        r_re, r_im = _cmul(*rows(npw[:chunk]), *kb)
        kmat = _dot_nt3(l_re, r_re) - _dot_nt3(l_im, r_im)
        tt_o[gl] = jnp.where(causal, kmat, 0.0).astype(BF16)

        e_re, e_im = _cmul(*c, *rows(pw[1:chunk + 1]))
        wor_o[gl] = e_re.astype(BF16)
        woi_o[gl] = (-e_im).astype(BF16)

        s_re, s_im = _cmul(*rows(pw[chunk - 1::-1]), *kb)
        wstr_o[gl] = s_re.astype(BF16)
        wsti_o[gl] = s_im.astype(BF16)

        sc = [pw[chunk]]
        for _ in range(6):
            sc.append(_cmul(*sc[-1], *sc[-1]))
        sc.append(pw[chunk // 2])
        scr_o[gl] = jnp.concatenate([x[0] for x in sc], axis=0)
        sci_o[gl] = jnp.concatenate([x[1] for x in sc], axis=0)


def _ssm_weights(a_re, a_im, log_dt, b_re, b_im, c_re, c_im, chunk):
    dg, p = a_re.shape
    h = b_re.shape[-1]
    th = chunk * h
    tau = np.repeat(np.arange(chunk, dtype=np.float32), h)
    e_col = jnp.asarray(tau.reshape(th, 1))
    e_row = jnp.asarray(tau.reshape(1, th))
    gblk = min(16, dg)
    row = lambda x: x.reshape(dg, 1, p)
    bt = lambda x: jnp.swapaxes(x, 1, 2)
    per_g = lambda *s: pl.BlockSpec((gblk,) + s, lambda g: (g,) + (0,) * len(s))
    const = lambda *s: pl.BlockSpec(s, lambda g: (0,) * len(s))
    tt, wor, woi, wstr, wsti, scr, sci = pl.pallas_call(
        functools.partial(_ssm_weights_body, chunk=chunk, gblk=gblk),
        grid=(dg // gblk,),
        in_specs=[per_g(1, p), per_g(1, p), per_g(1, 1),
                  per_g(h, p), per_g(h, p), per_g(h, p), per_g(h, p),
                  const(th, 1), const(1, th)],
        out_specs=[per_g(th, th), per_g(th, p), per_g(th, p), per_g(th, p), per_g(th, p),
                   per_g(8, p), per_g(8, p)],
        out_shape=[jax.ShapeDtypeStruct((dg, th, th), BF16),
                   jax.ShapeDtypeStruct((dg, th, p), BF16),
                   jax.ShapeDtypeStruct((dg, th, p), BF16),
                   jax.ShapeDtypeStruct((dg, th, p), BF16),
                   jax.ShapeDtypeStruct((dg, th, p), BF16),
                   jax.ShapeDtypeStruct((dg, 8, p), F32),
                   jax.ShapeDtypeStruct((dg, 8, p), F32)],
        compiler_params=_cparams(("parallel",)),
        name="ssm_weights",
    )(row(a_re), row(a_im), log_dt.reshape(dg, 1, 1),
      bt(b_re), bt(b_im), c_re, c_im, e_col, e_row)
    sw = lambda x: jnp.swapaxes(x, 1, 2)
    return tt, wor, woi, sw(wstr), sw(wsti), sw(scr), sw(sci)


def _ssm_body(*refs, steps, chunks, gpb, has_init):
    if has_init:
        (u_ref, d_ref, tt_ref, wor_ref, woi_ref, wsr_ref, wsi_ref, scr_ref, sci_ref,
         s0r_ref, s0i_ref, y_ref, sfr_ref, sfi_ref, ys_ref, xr_ref, xi_ref) = refs
    else:
        (u_ref, d_ref, tt_ref, wor_ref, woi_ref, wsr_ref, wsi_ref, scr_ref, sci_ref,
         y_ref, sfr_ref, sfi_ref, ys_ref, xr_ref, xi_ref) = refs
    h = SSM_GROUP
    th = steps * h
    w = u_ref.shape[0] // steps
    p = scr_ref.shape[1]
    ws_off = wsr_ref.shape[2] - th

    def step_rows(t):
        return pl.ds(t, w, stride=steps) if chunks > 1 else pl.ds(t * w, w)

    slabs = [u_ref[step_rows(t), :].T for t in range(steps)]

    for gl in range(gpb):
        rows = slice(gl * h, (gl + 1) * h)
        u = jnp.concatenate([s[rows, :] for s in slabs], axis=0)
        ub = u.astype(BF16)
        y = _dot(tt_ref[gl, :th, :th], ub) + d_ref[gl, :th, :] * u
        xr_ref[gl * p:(gl + 1) * p, :] = _dot(wsr_ref[gl, :, ws_off:], ub)
        xi_ref[gl * p:(gl + 1) * p, :] = _dot(wsi_ref[gl, :, ws_off:], ub)
        for t in range(steps):
            ys_ref[t, rows, :] = y[t * h:(t + 1) * h, :]

    gp = gpb * p
    col = lambda ref, k: ref[:, :, k:k + 1].reshape(gp, 1)
    xr, xi = xr_ref[...], xi_ref[...]
    if chunks > 1:
        c_idx = lax.broadcasted_iota(jnp.int32, (gp, w), 1) & (chunks - 1)

        def shift(x, sh):
            return jnp.concatenate([pltpu.roll(x[:, q:q + chunks], sh, axis=1)
                                    for q in range(0, w, chunks)], axis=1)

        sr, si = xr, xi
        k = 0
        while (1 << k) < chunks:
            sh = 1 << k
            rr, ri = shift(sr, sh), shift(si, sh)
            mr, mi = col(scr_ref, k), col(sci_ref, k)
            keep = c_idx >= sh
            sr = sr + jnp.where(keep, mr * rr - mi * ri, 0.0)
            si = si + jnp.where(keep, mr * ri + mi * rr, 0.0)
            k += 1
        first = c_idx >= 1
        xr_ref[...] = jnp.where(first, shift(sr, 1), 0.0)
        xi_ref[...] = jnp.where(first, shift(si, 1), 0.0)
        for n in range(w // chunks):
            last = n * chunks + chunks - 1
            sfr_ref[:, :, n:n + 1] = sr[:, last:last + 1].reshape(gpb, p, 1)
            sfi_ref[:, :, n:n + 1] = si[:, last:last + 1].reshape(gpb, p, 1)
    else:
        pr, pi = s0r_ref[...].reshape(gp, w), s0i_ref[...].reshape(gp, w)
        lr, li = col(scr_ref, 7), col(sci_ref, 7)
        sfr_ref[...] = (lr * pr - li * pi + xr).reshape(gpb, p, w)
        sfi_ref[...] = (lr * pi + li * pr + xi).reshape(gpb, p, w)
        xr_ref[...] = pr
        xi_ref[...] = pi

    for gl in range(gpb):
        rows = slice(gl * h, (gl + 1) * h)
        pr = xr_ref[gl * p:(gl + 1) * p, :].astype(BF16)
        pi = xi_ref[gl * p:(gl + 1) * p, :].astype(BF16)
        ya = _dot(wor_ref[gl, :th, :], pr) + _dot(woi_ref[gl, :th, :], pi)
        for t in range(steps):
            y = ys_ref[t, rows, :] + ya[t * h:(t + 1) * h, :]
            ys_ref[t, rows, :] = 0.5 * y * (1.0 + jnp.tanh(GELU_C * (y + 0.044715 * (y * y * y))))
    for t in range(steps):
        y_ref[step_rows(t), :] = ys_ref[t].T


def _ssm(z, row_blk, rows, d_t, ops, layer, n_groups, steps, chunks, s0=None, gpb=8):
    tt, wor, woi, wsr, wsi, scr, sci = ops
    h = SSM_GROUP
    w = rows // steps
    nseq = w // chunks
    p = scr.shape[1]
    thf = tt.shape[1]
    base = layer * (n_groups // gpb)
    wblk = lambda *s: pl.BlockSpec((gpb,) + s, lambda g: (base + g,) + (0,) * len(s))
    in_specs = [pl.BlockSpec((rows, gpb * h), lambda g: (row_blk, g)),
                wblk(thf, 1), wblk(thf, thf), wblk(thf, p), wblk(thf, p), wblk(p, thf), wblk(p, thf),
                wblk(p, scr.shape[2]), wblk(p, scr.shape[2])]
    args = [z, d_t, tt, wor, woi, wsr, wsi, scr, sci]
    if s0 is not None:
        in_specs += [pl.BlockSpec((gpb, p, w), lambda g: (g, 0, 0))] * 2
        args += list(s0)
    sf_spec = pl.BlockSpec((gpb, p, nseq), lambda g: (g, 0, 0))
    return pl.pallas_call(
        functools.partial(_ssm_body, steps=steps, chunks=chunks, gpb=gpb, has_init=s0 is not None),
        grid=(n_groups // gpb,),
        in_specs=in_specs,
        out_specs=[pl.BlockSpec((rows, gpb * h), lambda g: (0, g)), sf_spec, sf_spec],
        out_shape=[jax.ShapeDtypeStruct((rows, n_groups * h), F32),
                   jax.ShapeDtypeStruct((n_groups, p, nseq), F32),
                   jax.ShapeDtypeStruct((n_groups, p, nseq), F32)],
        scratch_shapes=[pltpu.VMEM((steps, gpb * h, w), F32),
                        pltpu.VMEM((gpb * p, w), F32), pltpu.VMEM((gpb * p, w), F32)],
        compiler_params=_cparams(("parallel",)),
        name="ssm_chunks" if s0 is None else "ssm_step",
    )(*args)


def _pool_seq_body(u_ref, w_ref, sc_ref, o_ref, z_ref):
    l, c = u_ref.shape
    pad = z_ref.shape[0] - l
    cg = c // len(POOL_WINDOWS)
    z_ref[:pad, :] = jnp.zeros((pad, c), F32)
    z_ref[pad:, :] = u_ref[...]
    pos1 = (lax.broadcasted_iota(jnp.int32, (l, 1), 0) + 1).astype(F32)
    for gi, win in enumerate(POOL_WINDOWS):
        cols = slice(gi * cg, (gi + 1) * cg)
        cur = z_ref[pad:, cols]
        tot = cur
        for k in range(1, win):
            tot = tot + z_ref[pad - k:pad - k + l, cols]
        inv_cnt = 1.0 / jnp.minimum(pos1, float(win))
        mixed = _dot((tot * inv_cnt - cur).astype(BF16), w_ref[gi].astype(BF16))
        o_ref[:, cols] = (mixed * sc_ref[:, cols]).astype(BF16)


def _pool_seq(z, w_pool, scale, layer, nseq, seqlen, width, col_blk):
    return pl.pallas_call(
        _pool_seq_body,
        grid=(nseq,),
        in_specs=[pl.BlockSpec((seqlen, width), lambda n: (n, col_blk)),
                  pl.BlockSpec((None,) + w_pool.shape[1:], lambda n: (layer, 0, 0, 0)),
                  pl.BlockSpec((None, 1, width), lambda n: (layer, 0, 0))],
        out_specs=pl.BlockSpec((seqlen, width), lambda n: (n, 0)),
        out_shape=jax.ShapeDtypeStruct((nseq * seqlen, width), BF16),
        scratch_shapes=[pltpu.VMEM((seqlen + 16, width), F32)],
        compiler_params=_cparams(("parallel",)),
        name="pool_seq",
    )(z, w_pool, scale)


def _pool_step_body(u_ref, prev_ref, w_ref, sc_ref, o_ref):
    n, buf, c = prev_ref.shape
    steps = u_ref.shape[0] // n
    cg = c // len(POOL_WINDOWS)

    def row(j, cols):
        return prev_ref[:, j, cols] if j < buf else u_ref[(j - buf) * n:(j - buf + 1) * n, cols]

    for gi, win in enumerate(POOL_WINDOWS):
        cols = slice(gi * cg, (gi + 1) * cg)
        for t in range(steps):
            cur = row(buf + t, cols)
            tot = cur
            for k in range(1, win):
                tot = tot + row(buf + t - k, cols)
            mixed = _dot((tot * (1.0 / win) - cur).astype(BF16), w_ref[gi].astype(BF16))
            o_ref[t * n:(t + 1) * n, cols] = (mixed * sc_ref[:, cols]).astype(BF16)


def _pool_step(z, prev, w_pool, scale, layer, rows, col_blk):
    n, buf, c = prev.shape[1:]
    row_blk = z.shape[0] // rows - 1
    return pl.pallas_call(
        _pool_step_body,
        grid=(1,),
        in_specs=[pl.BlockSpec((rows, c), lambda i: (row_blk, col_blk)),
                  pl.BlockSpec((None, n, buf, c), lambda i: (layer, 0, 0, 0)),
                  pl.BlockSpec((None,) + w_pool.shape[1:], lambda i: (layer, 0, 0, 0)),
                  pl.BlockSpec((None, 1, c), lambda i: (layer, 0, 0))],
        out_specs=pl.BlockSpec((rows, c), lambda i: (0, 0)),
        out_shape=jax.ShapeDtypeStruct((rows, c), BF16),
        compiler_params=_cparams(("arbitrary",)),
        name="pool_step",
    )(z, prev, w_pool, scale)


def _mix_body(gap_ref, gas_ref, ybp_ref, ybs_ref, sa_ref, sb_ref, wa_ref, wb_ref, wp_ref, o_ref, *, p_tiles):
    first = pl.program_id(0) < p_tiles
    ga = jnp.where(first, gap_ref[...], gas_ref[...]).astype(BF16)
    yb =jnp.where(first, ybp_ref[...], ybs_ref[...])
    for cs in _col_chunks(o_ref.shape[1], MXU_COLS):
        br_a = _dot(ga, wa_ref[:, cs].astype(BF16)) * jax.nn.sigmoid(_dot(ga, wb_ref[:, cs].astype(BF16)))
        br_b = _dot(yb, wp_ref[:, cs].astype(BF16))
        o_ref[:, cs] = (sa_ref[:, cs] * br_a + sb_ref[:, cs] * br_b).astype(BF16)


def _resident(shape, layer):
    return pl.BlockSpec((None,) + shape, lambda i: (layer,) + (0,) * len(shape), pipeline_mode=pl.Buffered(1))


def _split_specs(tm, width, p_tiles):
    return [pl.BlockSpec((tm, width), lambda i: (jnp.minimum(i, p_tiles - 1), 0)),
            pl.BlockSpec((tm, width), lambda i: (jnp.maximum(i - p_tiles, 0), 0))]


def _mix(ga_p, ga_s, yb_p, yb_s, z, wa, wb, wp, layer, tm, gate_col):
    m = z.shape[0]
    k = ga_p.shape[1]
    n = wa.shape[-1]
    gblk = gate_col // n
    p_tiles = ga_p.shape[0] // tm
    return pl.pallas_call(
        functools.partial(_mix_body, p_tiles=p_tiles),
        grid=(m // tm,),
        in_specs=_split_specs(tm, k, p_tiles) + _split_specs(tm, k, p_tiles) + [
                  pl.BlockSpec((tm, n), lambda i: (i, gblk)),
                  pl.BlockSpec((tm, n), lambda i: (i, gblk + 1)),
                  _resident((k, n), layer), _resident((k, n), layer), _resident((k, n), layer)],
        out_specs=pl.BlockSpec((tm, n), lambda i: (i, 0)),
        out_shape=jax.ShapeDtypeStruct((m, n), BF16),
        compiler_params=_cparams(("parallel",)),
        name="mix",
    )(ga_p, ga_s, yb_p, yb_s, z, z, wa, wb, wp)


def _resmm_body(a_ref, w_ref, h_ref, g_ref, o_ref, xn_ref):
    a = a_ref[...]
    for cs in _col_chunks(o_ref.shape[1], MXU_COLS):
        o_ref[:, cs] = h_ref[:, cs] + _dot(a, w_ref[:, cs].astype(BF16))
    xn_ref[...] = _rms_bf16(o_ref[...], g_ref[...])


def _resmm(a, w, hres, g_next, layer, tm):
    m, k = a.shape
    n = w.shape[-1]
    return pl.pallas_call(
        _resmm_body,
        grid=(m // tm,),
        in_specs=[pl.BlockSpec((tm, k), lambda i: (i, 0)),
                  _resident((k, n), layer),
                  pl.BlockSpec((tm, n), lambda i: (i, 0)),
                  pl.BlockSpec((None, 1, n), lambda i: (layer, 0, 0))],
        out_specs=[pl.BlockSpec((tm, n), lambda i: (i, 0)), pl.BlockSpec((tm, n), lambda i: (i, 0))],
        out_shape=[jax.ShapeDtypeStruct((m, n), F32), jax.ShapeDtypeStruct((m, n), BF16)],
        compiler_params=_cparams(("parallel",)),
        name="resmm",
    )(a, w, hres, g_next)


def _ple_update(x_ref, g_ref, pp_ref, ps_ref, wp_ref, wg_ref, dst_ref, p_tiles):
    xn = _rms_bf16(x_ref[...], g_ref[...])
    pb = jnp.where(pl.program_id(0) < p_tiles, pp_ref[...], ps_ref[...]).astype(BF16)
    for cs in _col_chunks(dst_ref.shape[1], MXU_COLS):
        gate = jax.nn.sigmoid(_dot(xn, wg_ref[:, cs].astype(BF16)))
        dst_ref[:, cs] = x_ref[:, cs] + _dot(pb, wp_ref[:, cs].astype(BF16)) * gate


def _ple_body(x_ref, g_ref, pp_ref, ps_ref, wp_ref, wg_ref, gn_ref, o_ref, xn_ref, *, p_tiles):
    _ple_update(x_ref, g_ref, pp_ref, ps_ref, wp_ref, wg_ref, o_ref, p_tiles)
    xn_ref[...] = _rms_bf16(o_ref[...], gn_ref[...])


def _ple_final_body(x_ref, g_ref, pp_ref, ps_ref, wp_ref, wg_ref, gf_ref, op_ref, os_ref, h_ref, *, p_tiles):
    _ple_update(x_ref, g_ref, pp_ref, ps_ref, wp_ref, wg_ref, h_ref, p_tiles)
    h = h_ref[...]
    y = h * lax.rsqrt(jnp.mean(h * h, axis=-1, keepdims=True) + RMS_EPS) * gf_ref[...]
    i = pl.program_id(0)

    @pl.when(i < p_tiles)
    def _():
        op_ref[...] = y

    @pl.when(i >= p_tiles)
    def _():
        os_ref[...] = y


def _ple(x, g, p_p, p_s, wp, wg, layer, tm, g_next=None, g_final=None):
    m, d = x.shape
    mp, pd = p_p.shape[1:]
    p_tiles = mp // tm
    in_specs = [pl.BlockSpec((tm, d), lambda i: (i, 0)),
                pl.BlockSpec((None, 1, d), lambda i: (layer, 0, 0)),
                pl.BlockSpec((None, tm, pd), lambda i: (layer, jnp.minimum(i, p_tiles - 1), 0)),
                pl.BlockSpec((None, tm, pd), lambda i: (layer, jnp.maximum(i - p_tiles, 0), 0)),
                _resident((pd, d), layer), _resident((d, d), layer)]
    if g_final is None:
        row = pl.BlockSpec((tm, d), lambda i: (i, 0))
        return pl.pallas_call(
            functools.partial(_ple_body, p_tiles=p_tiles), grid=(m // tm,),
            in_specs=in_specs + [pl.BlockSpec((None, 1, d), lambda i: (layer + 1, 0, 0))],
            out_specs=[row, row],
            out_shape=[jax.ShapeDtypeStruct((m, d), F32), jax.ShapeDtypeStruct((m, d), BF16)],
            compiler_params=_cparams(("parallel",)),
            name="ple",
        )(x, g, p_p, p_s, wp, wg, g_next)
    return pl.pallas_call(
        functools.partial(_ple_final_body, p_tiles=p_tiles),
        grid=(m // tm,),
        in_specs=in_specs + [pl.BlockSpec((1, d), lambda i: (0, 0))],
        out_specs=[pl.BlockSpec((tm, d), lambda i: (jnp.minimum(i, p_tiles - 1), 0)),
                   pl.BlockSpec((tm, d), lambda i: (jnp.maximum(i - p_tiles, 0), 0))],
        out_shape=[jax.ShapeDtypeStruct((mp, d), F32), jax.ShapeDtypeStruct((m - mp, d), F32)],
        scratch_shapes=[pltpu.VMEM((tm, d), F32)],
        compiler_params=_cparams(("arbitrary",)),
        name="ple_final",
    )(x, g, p_p, p_s, wp, wg, g_final)


def _prep_body(xp_ref, xs_ref, g_ref, h_ref, xn_ref, *, p_tiles):
    x = jnp.where(pl.program_id(0) < p_tiles, xp_ref[...], xs_ref[...])
    h_ref[...] = x
    xn_ref[...] = _rms_bf16(x, g_ref[...])


def _prep(x_p, x_s, g, tm):
    d = x_p.shape[1]
    m = x_p.shape[0] + x_s.shape[0]
    p_tiles = x_p.shape[0] // tm
    row = pl.BlockSpec((tm, d), lambda i: (i, 0))
    return pl.pallas_call(
        functools.partial(_prep_body, p_tiles=p_tiles),
        grid=(m // tm,),
        in_specs=_split_specs(tm, d, p_tiles) + [pl.BlockSpec((None, 1, d), lambda i: (0, 0, 0))],
        out_specs=[row, row],
        out_shape=[jax.ShapeDtypeStruct((m, d), F32), jax.ShapeDtypeStruct((m, d), BF16)],
        compiler_params=_cparams(("parallel",)),
        name="prep",
    )(x_p, x_s, g)


def _pick_tile(n, pref):
    t = min(pref, n)
    while n % t:
        t //= 2
    return t


def kernel(x_prompt, x_sample, state_ssm_re, state_ssm_im, state_pool, p_prompt, p_sample, g_ffn1, w_ffn1_gate, w_ffn1_up, w_ffn1_down, g_mix, w_in, ssm_a_re, ssm_a_im, ssm_log_dt, ssm_b_re, ssm_b_im, ssm_c_re, ssm_c_im, ssm_d, w_glu_a, w_glu_b, w_pool, pool_scale, w_pool_up, w_out, g_ffn2, w_ffn2_gate, w_ffn2_up, w_ffn2_down, g_ple, w_ple, w_ple_gate, g_final):
    nb, seq, d = x_prompt.shape
    ns, dseq, _ = x_sample.shape
    depth, n_groups, p_state = ssm_a_re.shape
    h = ssm_b_re.shape[-1]
    sw = n_groups * h
    pw = pool_scale.shape[-1]
    buf = state_pool.shape[2]
    chunk = SSM_CHUNK
    assert h == SSM_GROUP and dseq * 2 == chunk and seq % chunk == 0 and buf == max(POOL_WINDOWS) - 1
    n_chunks = seq // chunk
    assert n_chunks & (n_chunks - 1) == 0 and n_chunks <= 128
    mp, ms = nb * seq, ns * dseq
    m = mp + ms
    tm = _pick_tile(m, 1024)

    g3 = lambda a: a.reshape(depth, 1, -1)
    wg1, wu1, wd1 = w_ffn1_gate, w_ffn1_up, w_ffn1_down
    wg2, wu2, wd2 = w_ffn2_gate, w_ffn2_up, w_ffn2_down
    wga, wgb, wpu, wo = w_glu_a, w_glu_b, w_pool_up, w_out
    wpl, wpg, wpool = w_ple, w_ple_gate, w_pool
    tn_in = _pick_tile(sw + pw, 2048)
    gf1, gmx, gf2, gpl = g3(g_ffn1), g3(g_mix), g3(g_ffn2), g3(g_ple)
    pscale = g3(pool_scale)

    flat = lambda a: a.reshape((depth * n_groups,) + a.shape[2:])
    ops = _ssm_weights(flat(ssm_a_re), flat(ssm_a_im), flat(ssm_log_dt), flat(ssm_b_re), flat(ssm_b_im),
                       flat(ssm_c_re), flat(ssm_c_im), chunk)
    d_t = jnp.tile(flat(ssm_d), (1, chunk)).reshape(depth * n_groups, chunk * h, 1)

    p_p = p_prompt.reshape(depth, mp, -1)
    p_s = p_sample.transpose(0, 2, 1, 3).reshape(depth, ms, -1)
    tf = _pick_tile(w_ffn1_gate.shape[-1], 512)
    tm2 = _pick_tile(math.gcd(mp, ms), 512)
    gpb = min(8, n_groups)
    hcur, xn = _prep(x_prompt.reshape(mp, d), x_sample.transpose(1, 0, 2).reshape(ms, d), gf1, tm2)
    new_re_p, new_im_p, new_pool_p, new_re_s, new_im_s, new_pool_s = [], [], [], [], [], []
    for i in range(depth):
        h1 = _ffn(xn, hcur, wg1, wu1, wd1, i, tm, tf)
        z = _inproj(h1, gmx, w_in, i, tm, tn_in, sw + pw)

        ga_p, sr_p, si_p = _ssm(z, 0, mp, d_t, ops, i, n_groups, chunk, n_chunks, gpb=gpb)
        s0 = (state_ssm_re[i].transpose(1, 2, 0), state_ssm_im[i].transpose(1, 2, 0))
        ga_s, sr_s, si_s = _ssm(z, mp // ms, ms, d_t, ops, i, n_groups, dseq, 1, s0=s0, gpb=gpb)
        new_re_p.append(sr_p.transpose(2, 0, 1))
        new_im_p.append(si_p.transpose(2, 0, 1))
        new_re_s.append(sr_s.transpose(2, 0, 1))
        new_im_s.append(si_s.transpose(2, 0, 1))

        yb_p = _pool_seq(z, wpool, pscale, i, nb, seq, pw, sw // pw)
        yb_s = _pool_step(z, state_pool, wpool, pscale, i, ms, sw // pw)
        ub_s = z[mp:, sw:sw + pw].reshape(dseq, ns, pw).transpose(1, 0, 2)
        new_pool_p.append(jnp.stack([z[(n + 1) * seq - buf:(n + 1) * seq, sw:sw + pw] for n in range(nb)]))
        new_pool_s.append(jnp.concatenate([state_pool[i][:, dseq:, :], ub_s], axis=1))

        merged = _mix(ga_p, ga_s, yb_p, yb_s, z, wga, wgb, wpu, i, tm2, sw + pw)
        h2, xn2 = _resmm(merged, wo, h1, gf2, i, tm2)
        h3 = _ffn(xn2, h2, wg2, wu2, wd2, i, tm, tf)
        if i + 1 < depth:
            hcur, xn = _ple(h3, gpl, p_p, p_s, wpl, wpg, i, tm2, g_next=gf1)
        else:
            y_p, y_s = _ple(h3, gpl, p_p, p_s, wpl, wpg, i, tm2, g_final=g_final.reshape(1, d))

    return (y_p.reshape(nb, seq, d), y_s.reshape(dseq, ns, d).transpose(1, 0, 2),
            jnp.stack(new_re_p), jnp.stack(new_im_p), jnp.stack(new_pool_p),
            jnp.stack(new_re_s), jnp.stack(new_im_s), jnp.stack(new_pool_s))
```

```python
import functools
import math

import numpy as np
import jax
import jax.numpy as jnp
from jax import lax
from jax.experimental import pallas as pl
from jax.experimental.pallas import tpu as pltpu

F32 = jnp.float32
BF16 = jnp.bfloat16
RMS_EPS = 1e-6
POOL_WINDOWS = (2, 4, 8, 16)
SSM_GROUP = 16
SSM_CHUNK = 16
GELU_C = math.sqrt(2.0 / math.pi)
VMEM_LIMIT = 62 * 1024 * 1024
MXU_COLS = 256


def _cparams(sem):
    return pltpu.CompilerParams(dimension_semantics=sem, vmem_limit_bytes=VMEM_LIMIT)


def _rms_bf16(x, g):
    inv = lax.rsqrt(jnp.mean(x * x, axis=-1, keepdims=True) + RMS_EPS)
    return (x * inv * g).astype(BF16)


def _dot(a, b):
    return jnp.dot(a, b, preferred_element_type=F32)


def _col_chunks(width, chunk):
    chunk = min(chunk, width)
    return [slice(c0, c0 + chunk) for c0 in range(0, width, chunk)]


def _ffn_body(xn_ref, h_ref, wg_ref, wu_ref, wd_ref, *rest, res_chunks):
    gn_ref, o_ref, xo_ref = rest if len(rest) == 3 else (None,) + rest + (None,)
    j = pl.program_id(1)
    nj = pl.num_programs(1)

    @pl.when(j == 0)
    def _():
        o_ref[...] = jnp.zeros_like(o_ref)

    xn = xn_ref[...]
    mids = []
    for cs in _col_chunks(wg_ref.shape[1], MXU_COLS):
        a = _dot(xn, wg_ref[:, cs].astype(BF16))
        b = _dot(xn, wu_ref[:, cs].astype(BF16))
        mids.append((0.5 * a * jax.nn.sigmoid(a) * b).astype(BF16))
    mid = jnp.concatenate(mids, axis=1)
    for cs in _col_chunks(o_ref.shape[1], 2 * MXU_COLS):
        o_ref[:, cs] += _dot(mid, wd_ref[:, cs].astype(BF16))

    cw = o_ref.shape[1] // res_chunks
    for c in range(res_chunks):
        @pl.when(j == nj - res_chunks + c)
        def _(c=c):
            o_ref[:, c * cw:(c + 1) * cw] += h_ref[...]

    if xo_ref is not None:
        @pl.when(j == nj - 1)
        def _():
            xo_ref[...] = _rms_bf16(o_ref[...], gn_ref[...])


def _ffn(xn, hres, wg, wu, wd, layer, tm, tf, g_next=None):
    m, d = xn.shape
    f = wg.shape[-1]
    nj = f // tf
    res_chunks = min(4, nj)
    row = lambda: pl.BlockSpec((tm, d), lambda i, j: (i, 0))
    in_specs = [
        pl.BlockSpec((tm, d), lambda i, j: (i, 0)),
        pl.BlockSpec((tm, d // res_chunks),
                     lambda i, j: (i, jnp.clip(j - (nj - res_chunks), 0, res_chunks - 1))),
        pl.BlockSpec((None, d, tf), lambda i, j: (layer, 0, j)),
        pl.BlockSpec((None, d, tf), lambda i, j: (layer, 0, j)),
        pl.BlockSpec((None, tf, d), lambda i, j: (layer, j, 0)),
    ]
    args = [xn, hres, wg, wu, wd]
    out_specs, out_shape = row(), jax.ShapeDtypeStruct((m, d), F32)
    if g_next is not None:
        in_specs.append(pl.BlockSpec((None, 1, d), lambda i, j: (layer, 0, 0)))
        args.append(g_next)
        out_specs, out_shape = [row(), row()], [out_shape, jax.ShapeDtypeStruct((m, d), BF16)]
    return pl.pallas_call(
        functools.partial(_ffn_body, res_chunks=res_chunks),
        grid=(m // tm, nj),
        in_specs=in_specs,
        out_specs=out_specs,
        out_shape=out_shape,
        compiler_params=_cparams(("parallel", "arbitrary")),
        name="ffn",
    )(*args)


def _inproj_body(x_ref, g_ref, w_ref, o_ref, *, gate_from):
    is_gate = pl.program_id(0) >= gate_from
    xn = _rms_bf16(x_ref[...], g_ref[...])
    for cs in _col_chunks(o_ref.shape[1], MXU_COLS):
        r = _dot(xn, w_ref[:, cs].astype(BF16))
        o_ref[:, cs] = jnp.where(is_gate, jax.nn.sigmoid(r), r)


def _inproj(x, g, w, layer, tm, tn, n_plain):
    m, d = x.shape
    n = w.shape[-1]
    return pl.pallas_call(
        functools.partial(_inproj_body, gate_from=n_plain // tn),
        grid=(n // tn, m // tm),
        in_specs=[
            pl.BlockSpec((tm, d), lambda j, i: (i, 0)),
            pl.BlockSpec((None, 1, d), lambda j, i: (layer, 0, 0)),
            pl.BlockSpec((None, d, tn), lambda j, i: (layer, 0, j), pipeline_mode=pl.Buffered(1)),
        ],
        out_specs=pl.BlockSpec((tm, tn), lambda j, i: (i, j)),
        out_shape=jax.ShapeDtypeStruct((m, n), F32),
        compiler_params=_cparams(("arbitrary", "arbitrary")),
        name="inproj",
    )(x, g, w)


def _cmul(ar, ai, br, bi):
    return ar * br - ai * bi, ar * bi + ai * br


def _dot_nt3(a, b):
    nt = (((1,), (1,)), ((), ()))
    dot = lambda x, y: lax.dot_general(x, y, nt, preferred_element_type=F32)
    ah, bh = a.astype(BF16), b.astype(BF16)
    al = (a - ah.astype(F32)).astype(BF16)
    bl = (b - bh.astype(F32)).astype(BF16)
    return dot(ah, bh) + dot(ah, bl) + dot(al, bh)


def _ssm_weights_body(ar_ref, ai_ref, ldt_ref, btr_ref, bti_ref, ctr_ref, cti_ref, e_col, e_row,
                      tt_o, wor_o, woi_o, wstr_o, wsti_o, scr_o, sci_o, *, chunk, gblk):
    h = SSM_GROUP
    causal = e_col[...] >= e_row[...]
    nt = (((1,), (1,)), ((), ()))
    hi = lax.Precision.HIGHEST

    def rows(pows):
        width = pows[0][0].shape[1]
        return tuple(jnp.concatenate([jnp.broadcast_to(x[k], (h, width)) for x in pows], axis=0)
                     for k in (0, 1))

    for gl in range(gblk):
        dt = jnp.exp(ldt_ref[gl])
        a_re, a_im = ar_ref[gl], ai_ref[gl]
        mag = jnp.exp(a_re * dt)
        ang = a_im * dt
        lr, li = mag * jnp.cos(ang), mag * jnp.sin(ang)
        den = a_re * a_re + a_im * a_im
        num_re = lr - 1.0
        k_re = (num_re * a_re + li * a_im) / den
        k_im = (li * a_re - num_re * a_im) / den
        inv = 1.0 / (lr * lr + li * li)
        nr, ni = lr * inv, -li * inv
        pw = [(jnp.ones_like(lr), jnp.zeros_like(lr))]
        npw = list(pw)
        for _ in range(chunk):
            pw.append(_cmul(*pw[-1], lr, li))
            npw.append(_cmul(*npw[-1], nr, ni))
        tile = lambda x: jnp.concatenate([x] * chunk, axis=0)
        kb = _cmul(k_re, k_im, tile(btr_ref[gl]), tile(bti_ref[gl]))
        c = (tile(ctr_ref[gl]), tile(cti_ref[gl]))

        l_re, l_im = _cmul(*c, *rows(pw[:chunk]))
        r_re, r_im = _cmul(*rows(npw[:chunk]), *kb)
        kmat = _dot_nt3(l_re, r_re) - _dot_nt3(l_im, r_im)
        tt_o[gl] = jnp.where(causal, kmat, 0.0).astype(BF16)

        e_re, e_im = _cmul(*c, *rows(pw[1:chunk + 1]))
        wor_o[gl] = e_re.astype(BF16)
        woi_o[gl] = (-e_im).astype(BF16)

        s_re, s_im = _cmul(*rows(pw[chunk - 1::-1]), *kb)
        wstr_o[gl] = s_re.astype(BF16)
        wsti_o[gl] = s_im.astype(BF16)

        sc = [pw[chunk]]
        for _ in range(6):
            sc.append(_cmul(*sc[-1], *sc[-1]))
        sc.append(pw[chunk // 2])
        scr_o[gl] = jnp.concatenate([x[0] for x in sc], axis=0)
        sci_o[gl] = jnp.concatenate([x[1] for x in sc], axis=0)


def _ssm_weights(a_re, a_im, log_dt, b_re, b_im, c_re, c_im, chunk):
    dg, p = a_re.shape
    h = b_re.shape[-1]
    th = chunk * h
    tau = np.repeat(np.arange(chunk, dtype=np.float32), h)
    e_col = jnp.asarray(tau.reshape(th, 1))
    e_row = jnp.asarray(tau.reshape(1, th))
    gblk = min(16, dg)
    row = lambda x: x.reshape(dg, 1, p)
    bt = lambda x: jnp.swapaxes(x, 1, 2)
    per_g = lambda *s: pl.BlockSpec((gblk,) + s, lambda g: (g,) + (0,) * len(s))
    const = lambda *s: pl.BlockSpec(s, lambda g: (0,) * len(s))
    tt, wor, woi, wstr, wsti, scr, sci = pl.pallas_call(
        functools.partial(_ssm_weights_body, chunk=chunk, gblk=gblk),
        grid=(dg // gblk,),
        in_specs=[per_g(1, p), per_g(1, p), per_g(1, 1),
                  per_g(h, p), per_g(h, p), per_g(h, p), per_g(h, p),
                  const(th, 1), const(1, th)],
        out_specs=[per_g(th, th), per_g(th, p), per_g(th, p), per_g(th, p), per_g(th, p),
                   per_g(8, p), per_g(8, p)],
        out_shape=[jax.ShapeDtypeStruct((dg, th, th), BF16),
                   jax.ShapeDtypeStruct((dg, th, p), BF16),
                   jax.ShapeDtypeStruct((dg, th, p), BF16),
                   jax.ShapeDtypeStruct((dg, th, p), BF16),
                   jax.ShapeDtypeStruct((dg, th, p), BF16),
                   jax.ShapeDtypeStruct((dg, 8, p), F32),
                   jax.ShapeDtypeStruct((dg, 8, p), F32)],
        compiler_params=_cparams(("parallel",)),
        name="ssm_weights",
    )(row(a_re), row(a_im), log_dt.reshape(dg, 1, 1),
      bt(b_re), bt(b_im), c_re, c_im, e_col, e_row)
    sw = lambda x: jnp.swapaxes(x, 1, 2)
    return tt, wor, woi, sw(wstr), sw(wsti), sw(scr), sw(sci)


def _ssm_body(*refs, steps, chunks, gpb, has_init):
    if has_init:
        (u_ref, d_ref, tt_ref, wor_ref, woi_ref, wsr_ref, wsi_ref, scr_ref, sci_ref,
         s0r_ref, s0i_ref, y_ref, sfr_ref, sfi_ref, ys_ref, xr_ref, xi_ref) = refs
    else:
        (u_ref, d_ref, tt_ref, wor_ref, woi_ref, wsr_ref, wsi_ref, scr_ref, sci_ref,
         y_ref, sfr_ref, sfi_ref, ys_ref, xr_ref, xi_ref) = refs
    h = SSM_GROUP
    th = steps * h
    w = u_ref.shape[0] // steps
    p = scr_ref.shape[1]
    ws_off = wsr_ref.shape[2] - th

    def step_rows(t):
        return pl.ds(t, w, stride=steps) if chunks > 1 else pl.ds(t * w, w)

    slabs = [u_ref[step_rows(t), :].T for t in range(steps)]

    for gl in range(gpb):
        rows = slice(gl * h, (gl + 1) * h)
        u = jnp.concatenate([s[rows, :] for s in slabs], axis=0)
        ub = u.astype(BF16)
        y = _dot(tt_ref[gl, :th, :th], ub) + jnp.concatenate([d_ref[gl]] * steps, axis=0) * u
        xr_ref[gl * p:(gl + 1) * p, :] = _dot(wsr_ref[gl, :, ws_off:], ub)
        xi_ref[gl * p:(gl + 1) * p, :] = _dot(wsi_ref[gl, :, ws_off:], ub)
        for t in range(steps):
            ys_ref[t, rows, :] = y[t * h:(t + 1) * h, :]

    gp = gpb * p
    col = lambda ref, k: ref[:, :, k:k + 1].reshape(gp, 1)
    xr, xi = xr_ref[...], xi_ref[...]
    if chunks > 1:
        c_idx = lax.broadcasted_iota(jnp.int32, (gp, w), 1) & (chunks - 1)

        def shift(x, sh):
            return jnp.concatenate([pltpu.roll(x[:, q:q + chunks], sh, axis=1)
                                    for q in range(0, w, chunks)], axis=1)

        sr, si = xr, xi
        k = 0
        while (1 << k) < chunks:
            sh = 1 << k
            rr, ri = shift(sr, sh), shift(si, sh)
            mr, mi = col(scr_ref, k), col(sci_ref, k)
            keep = c_idx >= sh
            sr = sr + jnp.where(keep, mr * rr - mi * ri, 0.0)
            si = si + jnp.where(keep, mr * ri + mi * rr, 0.0)
            k += 1
        first = c_idx >= 1
        xr_ref[...] = jnp.where(first, shift(sr, 1), 0.0)
        xi_ref[...] = jnp.where(first, shift(si, 1), 0.0)
        for n in range(w // chunks):
            last = n * chunks + chunks - 1
            sfr_ref[:, :, n:n + 1] = sr[:, last:last + 1].reshape(gpb, p, 1)
            sfi_ref[:, :, n:n + 1] = si[:, last:last + 1].reshape(gpb, p, 1)
    else:
        pr, pi = s0r_ref[...].reshape(gp, w), s0i_ref[...].reshape(gp, w)
        lr, li = col(scr_ref, 7), col(sci_ref, 7)
        sfr_ref[...] = (lr * pr - li * pi + xr).reshape(gpb, p, w)
        sfi_ref[...] = (lr * pi + li * pr + xi).reshape(gpb, p, w)
        xr_ref[...] = pr
        xi_ref[...] = pi

    for gl in range(gpb):
        rows = slice(gl * h, (gl + 1) * h)
        pr = xr_ref[gl * p:(gl + 1) * p, :].astype(BF16)
        pi = xi_ref[gl * p:(gl + 1) * p, :].astype(BF16)
        ya = _dot(wor_ref[gl, :th, :], pr) + _dot(woi_ref[gl, :th, :], pi)
        for t in range(steps):
            y = ys_ref[t, rows, :] + ya[t * h:(t + 1) * h, :]
            ys_ref[t, rows, :] = 0.5 * y * (1.0 + jnp.tanh(GELU_C * (y + 0.044715 * (y * y * y))))
    for t in range(steps):
        y_ref[step_rows(t), :] = ys_ref[t].T


def _ssm(z, row_blk, rows, d_t, ops, layer, n_groups, steps, chunks, s0=None, gpb=8):
    tt, wor, woi, wsr, wsi, scr, sci = ops
    h = SSM_GROUP
    w = rows // steps
    nseq = w // chunks
    p = scr.shape[1]
    thf = tt.shape[1]
    base = layer * (n_groups // gpb)
    wblk = lambda *s: pl.BlockSpec((gpb,) + s, lambda g: (base + g,) + (0,) * len(s))
    in_specs = [pl.BlockSpec((rows, gpb * h), lambda g: (row_blk, g)),
                wblk(h, 1), wblk(thf, thf), wblk(thf, p), wblk(thf, p), wblk(p, thf), wblk(p, thf),
                wblk(p, scr.shape[2]), wblk(p, scr.shape[2])]
    args = [z, d_t, tt, wor, woi, wsr, wsi, scr, sci]
    if s0 is not None:
        in_specs += [pl.BlockSpec((gpb, p, w), lambda g: (g, 0, 0))] * 2
        args += list(s0)
    sf_spec = pl.BlockSpec((gpb, p, nseq), lambda g: (g, 0, 0))
    return pl.pallas_call(
        functools.partial(_ssm_body, steps=steps, chunks=chunks, gpb=gpb, has_init=s0 is not None),
        grid=(n_groups // gpb,),
        in_specs=in_specs,
        out_specs=[pl.BlockSpec((rows, gpb * h), lambda g: (0, g)), sf_spec, sf_spec],
        out_shape=[jax.ShapeDtypeStruct((rows, n_groups * h), F32),
                   jax.ShapeDtypeStruct((n_groups, p, nseq), F32),
                   jax.ShapeDtypeStruct((n_groups, p, nseq), F32)],
        scratch_shapes=[pltpu.VMEM((steps, gpb * h, w), F32),
                        pltpu.VMEM((gpb * p, w), F32), pltpu.VMEM((gpb * p, w), F32)],
        compiler_params=_cparams(("parallel",)),
        name="ssm_chunks" if s0 is None else "ssm_step",
    )(*args)


def _pool_seq_body(u_ref, w_ref, sc_ref, o_ref, z_ref):
    l, c = u_ref.shape
    pad = z_ref.shape[0] - l
    cg = c // len(POOL_WINDOWS)
    z_ref[:pad, :] = jnp.zeros((pad, c), F32)
    z_ref[pad:, :] = u_ref[...]
    pos1 = (lax.broadcasted_iota(jnp.int32, (l, 1), 0) + 1).astype(F32)
    for gi, win in enumerate(POOL_WINDOWS):
        cols = slice(gi * cg, (gi + 1) * cg)
        cur = z_ref[pad:, cols]
        tot = cur
        for k in range(1, win):
            tot = tot + z_ref[pad - k:pad - k + l, cols]
        inv_cnt = 1.0 / jnp.minimum(pos1, float(win))
        mixed = _dot((tot * inv_cnt - cur).astype(BF16), w_ref[gi].astype(BF16))
        o_ref[:, cols] = (mixed * sc_ref[:, cols]).astype(BF16)


def _pool_seq(z, w_pool, scale, layer, nseq, seqlen, width, col_blk):
    return pl.pallas_call(
        _pool_seq_body,
        grid=(nseq,),
        in_specs=[pl.BlockSpec((seqlen, width), lambda n: (n, col_blk)),
                  pl.BlockSpec((None,) + w_pool.shape[1:], lambda n: (layer, 0, 0, 0)),
                  pl.BlockSpec((None, 1, width), lambda n: (layer, 0, 0))],
        out_specs=pl.BlockSpec((seqlen, width), lambda n: (n, 0)),
        out_shape=jax.ShapeDtypeStruct((nseq * seqlen, width), BF16),
        scratch_shapes=[pltpu.VMEM((seqlen + 16, width), F32)],
        compiler_params=_cparams(("parallel",)),
        name="pool_seq",
    )(z, w_pool, scale)


def _pool_step_body(u_ref, prev_ref, w_ref, sc_ref, *rest):
    o_ref, nxt_ref = rest[-2:]
    n, buf, c = prev_ref.shape
    steps = u_ref.shape[0] // n
    cg = c // len(POOL_WINDOWS)

    def row(j, cols):
        return prev_ref[:, j, cols] if j < buf else u_ref[(j - buf) * n:(j - buf + 1) * n, cols]

    for gi, win in enumerate(POOL_WINDOWS):
        cols = slice(gi * cg, (gi + 1) * cg)
        for t in range(steps):
            cur = row(buf + t, cols)
            tot = cur
            for k in range(1, win):
                tot = tot + row(buf + t - k, cols)
            mixed = _dot((tot * (1.0 / win) - cur).astype(BF16), w_ref[gi].astype(BF16))
            o_ref[t * n:(t + 1) * n, cols] = (mixed * sc_ref[:, cols]).astype(BF16)
    for j in range(buf):
        nxt_ref[:, j, :] = row(j + steps, slice(None))


def _pool_step(z, prev, w_pool, scale, layer, rows, col_blk, carried=None):
    n, buf, c = prev.shape[1:]
    row_blk = z.shape[0] // rows - 1
    in_specs = [pl.BlockSpec((rows, c), lambda i: (row_blk, col_blk)),
                pl.BlockSpec((None, n, buf, c), lambda i: (layer, 0, 0, 0)),
                pl.BlockSpec((None,) + w_pool.shape[1:], lambda i: (layer, 0, 0, 0)),
                pl.BlockSpec((None, 1, c), lambda i: (layer, 0, 0))]
    args = [z, prev, w_pool, scale]
    aliases = {}
    if carried is not None:
        in_specs.append(pl.BlockSpec(memory_space=pl.ANY))
        args.append(carried)
        aliases = {len(args) - 1: 1}
    return pl.pallas_call(
        _pool_step_body,
        grid=(1,),
        in_specs=in_specs,
        out_specs=[pl.BlockSpec((rows, c), lambda i: (0, 0)),
                   pl.BlockSpec((None, n, buf, c), lambda i: (layer, 0, 0, 0))],
        out_shape=[jax.ShapeDtypeStruct((rows, c), BF16), jax.ShapeDtypeStruct(prev.shape, F32)],
        input_output_aliases=aliases,
        compiler_params=_cparams(("arbitrary",)),
        name="pool_step",
    )(*args)


def _mix_body(gap_ref, gas_ref, ybp_ref, ybs_ref, sa_ref, sb_ref, wa_ref, wb_ref, wp_ref, o_ref, *, p_tiles):
    first = pl.program_id(0) < p_tiles
    ga = jnp.where(first, gap_ref[...], gas_ref[...]).astype(BF16)
    yb =jnp.where(first, ybp_ref[...], ybs_ref[...])
    for cs in _col_chunks(o_ref.shape[1], MXU_COLS):
        br_a = _dot(ga, wa_ref[:, cs].astype(BF16)) * jax.nn.sigmoid(_dot(ga, wb_ref[:, cs].astype(BF16)))
        br_b = _dot(yb, wp_ref[:, cs].astype(BF16))
        o_ref[:, cs] = (sa_ref[:, cs] * br_a + sb_ref[:, cs] * br_b).astype(BF16)


def _resident(shape, layer):
    return pl.BlockSpec((None,) + shape, lambda i: (layer,) + (0,) * len(shape), pipeline_mode=pl.Buffered(1))


def _split_specs(tm, width, p_tiles):
    return [pl.BlockSpec((tm, width), lambda i: (jnp.minimum(i, p_tiles - 1), 0)),
            pl.BlockSpec((tm, width), lambda i: (jnp.maximum(i - p_tiles, 0), 0))]


def _mix(ga_p, ga_s, yb_p, yb_s, z, wa, wb, wp, layer, tm, gate_col):
    m = z.shape[0]
    k = ga_p.shape[1]
    n = wa.shape[-1]
    gblk = gate_col // n
    p_tiles = ga_p.shape[0] // tm
    return pl.pallas_call(
        functools.partial(_mix_body, p_tiles=p_tiles),
        grid=(m // tm,),
        in_specs=_split_specs(tm, k, p_tiles) + _split_specs(tm, k, p_tiles) + [
                  pl.BlockSpec((tm, n), lambda i: (i, gblk)),
                  pl.BlockSpec((tm, n), lambda i: (i, gblk + 1)),
                  _resident((k, n), layer), _resident((k, n), layer), _resident((k, n), layer)],
        out_specs=pl.BlockSpec((tm, n), lambda i: (i, 0)),
        out_shape=jax.ShapeDtypeStruct((m, n), BF16),
        compiler_params=_cparams(("parallel",)),
        name="mix",
    )(ga_p, ga_s, yb_p, yb_s, z, z, wa, wb, wp)


def _resmm_body(a_ref, w_ref, h_ref, g_ref, o_ref, xn_ref):
    a = a_ref[...]
    for cs in _col_chunks(o_ref.shape[1], MXU_COLS):
        o_ref[:, cs] = h_ref[:, cs] + _dot(a, w_ref[:, cs].astype(BF16))
    xn_ref[...] = _rms_bf16(o_ref[...], g_ref[...])


def _resmm(a, w, hres, g_next, layer, tm):
    m, k = a.shape
    n = w.shape[-1]
    return pl.pallas_call(
        _resmm_body,
        grid=(m // tm,),
        in_specs=[pl.BlockSpec((tm, k), lambda i: (i, 0)),
                  _resident((k, n), layer),
                  pl.BlockSpec((tm, n), lambda i: (i, 0)),
                  pl.BlockSpec((None, 1, n), lambda i: (layer, 0, 0))],
        out_specs=[pl.BlockSpec((tm, n), lambda i: (i, 0)), pl.BlockSpec((tm, n), lambda i: (i, 0))],
        out_shape=[jax.ShapeDtypeStruct((m, n), F32), jax.ShapeDtypeStruct((m, n), BF16)],
        compiler_params=_cparams(("parallel",)),
        name="resmm",
    )(a, w, hres, g_next)


def _ple_update(x_ref, g_ref, pp_ref, ps_ref, wp_ref, wg_ref, dst_ref, p_tiles):
    xn = _rms_bf16(x_ref[...], g_ref[...])
    pb = jnp.where(pl.program_id(0) < p_tiles, pp_ref[...], ps_ref[...]).astype(BF16)
    for cs in _col_chunks(dst_ref.shape[1], MXU_COLS):
        gate = jax.nn.sigmoid(_dot(xn, wg_ref[:, cs].astype(BF16)))
        dst_ref[:, cs] = x_ref[:, cs] + _dot(pb, wp_ref[:, cs].astype(BF16)) * gate


def _ple_body(x_ref, g_ref, pp_ref, ps_ref, wp_ref, wg_ref, gn_ref, o_ref, xn_ref, *, p_tiles):
    _ple_update(x_ref, g_ref, pp_ref, ps_ref, wp_ref, wg_ref, o_ref, p_tiles)
    xn_ref[...] = _rms_bf16(o_ref[...], gn_ref[...])


def _ple_final_body(x_ref, g_ref, pp_ref, ps_ref, wp_ref, wg_ref, gf_ref, op_ref, os_ref, h_ref, *, p_tiles):
    _ple_update(x_ref, g_ref, pp_ref, ps_ref, wp_ref, wg_ref, h_ref, p_tiles)
    h = h_ref[...]
    y = h * lax.rsqrt(jnp.mean(h * h, axis=-1, keepdims=True) + RMS_EPS) * gf_ref[...]
    i = pl.program_id(0)

    @pl.when(i < p_tiles)
    def _():
        op_ref[...] = y

    @pl.when(i >= p_tiles)
    def _():
        os_ref[...] = y


def _ple(x, g, p_p, p_s, wp, wg, layer, tm, g_next=None, g_final=None):
    m, d = x.shape
    mp, pd = p_p.shape[1:]
    p_tiles = mp // tm
    in_specs = [pl.BlockSpec((tm, d), lambda i: (i, 0)),
                pl.BlockSpec((None, 1, d), lambda i: (layer, 0, 0)),
                pl.BlockSpec((None, tm, pd), lambda i: (layer, jnp.minimum(i, p_tiles - 1), 0)),
                pl.BlockSpec((None, tm, pd), lambda i: (layer, jnp.maximum(i - p_tiles, 0), 0)),
                _resident((pd, d), layer), _resident((d, d), layer)]
    if g_final is None:
        row = pl.BlockSpec((tm, d), lambda i: (i, 0))
        return pl.pallas_call(
            functools.partial(_ple_body, p_tiles=p_tiles), grid=(m // tm,),
            in_specs=in_specs + [pl.BlockSpec((None, 1, d), lambda i: (layer + 1, 0, 0))],
            out_specs=[row, row],
            out_shape=[jax.ShapeDtypeStruct((m, d), F32), jax.ShapeDtypeStruct((m, d), BF16)],
            compiler_params=_cparams(("parallel",)),
            name="ple",
        )(x, g, p_p, p_s, wp, wg, g_next)
    return pl.pallas_call(
        functools.partial(_ple_final_body, p_tiles=p_tiles),
        grid=(m // tm,),
        in_specs=in_specs + [pl.BlockSpec((1, d), lambda i: (0, 0))],
        out_specs=[pl.BlockSpec((tm, d), lambda i: (jnp.minimum(i, p_tiles - 1), 0)),
                   pl.BlockSpec((tm, d), lambda i: (jnp.maximum(i - p_tiles, 0), 0))],
        out_shape=[jax.ShapeDtypeStruct((mp, d), F32), jax.ShapeDtypeStruct((m - mp, d), F32)],
        scratch_shapes=[pltpu.VMEM((tm, d), F32)],
        compiler_params=_cparams(("arbitrary",)),
        name="ple_final",
    )(x, g, p_p, p_s, wp, wg, g_final)


def _prep_body(xp_ref, xs_ref, g_ref, h_ref, xn_ref, *, p_tiles):
    x = jnp.where(pl.program_id(0) < p_tiles, xp_ref[...], xs_ref[...])
    h_ref[...] = x
    xn_ref[...] = _rms_bf16(x, g_ref[...])


def _prep(x_p, x_s, g, tm):
    d = x_p.shape[1]
    m = x_p.shape[0] + x_s.shape[0]
    p_tiles = x_p.shape[0] // tm
    row = pl.BlockSpec((tm, d), lambda i: (i, 0))
    return pl.pallas_call(
        functools.partial(_prep_body, p_tiles=p_tiles),
        grid=(m // tm,),
        in_specs=_split_specs(tm, d, p_tiles) + [pl.BlockSpec((None, 1, d), lambda i: (0, 0, 0))],
        out_specs=[row, row],
        out_shape=[jax.ShapeDtypeStruct((m, d), F32), jax.ShapeDtypeStruct((m, d), BF16)],
        compiler_params=_cparams(("parallel",)),
        name="prep",
    )(x_p, x_s, g)


def _pick_tile(n, pref):
    t = min(pref, n)
    while n % t:
        t //= 2
    return t


def kernel(x_prompt, x_sample, state_ssm_re, state_ssm_im, state_pool, p_prompt, p_sample, g_ffn1, w_ffn1_gate, w_ffn1_up, w_ffn1_down, g_mix, w_in, ssm_a_re, ssm_a_im, ssm_log_dt, ssm_b_re, ssm_b_im, ssm_c_re, ssm_c_im, ssm_d, w_glu_a, w_glu_b, w_pool, pool_scale, w_pool_up, w_out, g_ffn2, w_ffn2_gate, w_ffn2_up, w_ffn2_down, g_ple, w_ple, w_ple_gate, g_final):
    nb, seq, d = x_prompt.shape
    ns, dseq, _ = x_sample.shape
    depth, n_groups, p_state = ssm_a_re.shape
    h = ssm_b_re.shape[-1]
    sw = n_groups * h
    pw = pool_scale.shape[-1]
    buf = state_pool.shape[2]
    chunk = SSM_CHUNK
    assert h == SSM_GROUP and dseq * 2 == chunk and seq % chunk == 0 and buf == max(POOL_WINDOWS) - 1
    n_chunks = seq // chunk
    assert n_chunks & (n_chunks - 1) == 0 and n_chunks <= 128
    mp, ms = nb * seq, ns * dseq
    m = mp + ms
    tm = _pick_tile(m, 1024)

    g3 = lambda a: a.reshape(depth, 1, -1)
    wg1, wu1, wd1 = w_ffn1_gate, w_ffn1_up, w_ffn1_down
    wg2, wu2, wd2 = w_ffn2_gate, w_ffn2_up, w_ffn2_down
    wga, wgb, wpu, wo = w_glu_a, w_glu_b, w_pool_up, w_out
    wpl, wpg, wpool = w_ple, w_ple_gate, w_pool
    tn_in = _pick_tile(sw + pw, 2048)
    gf1, gmx, gf2, gpl = g3(g_ffn1), g3(g_mix), g3(g_ffn2), g3(g_ple)
    pscale = g3(pool_scale)

    flat = lambda a: a.reshape((depth * n_groups,) + a.shape[2:])
    ops = _ssm_weights(flat(ssm_a_re), flat(ssm_a_im), flat(ssm_log_dt), flat(ssm_b_re), flat(ssm_b_im),
                       flat(ssm_c_re), flat(ssm_c_im), chunk)
    d_t = flat(ssm_d).reshape(depth * n_groups, h, 1)

    p_p = p_prompt.reshape(depth, mp, -1)
    p_s = p_sample.transpose(0, 2, 1, 3).reshape(depth, ms, -1)
    tf = _pick_tile(w_ffn1_gate.shape[-1], 512)
    tm2 = _pick_tile(math.gcd(mp, ms), 512)
    gpb = min(8, n_groups)
    hcur, xn = _prep(x_prompt.reshape(mp, d), x_sample.transpose(1, 0, 2).reshape(ms, d), gf1, tm2)
    new_re_p, new_im_p, new_pool_p, new_re_s, new_im_s = [], [], [], [], []
    new_pool_s = None
    for i in range(depth):
        h1 = _ffn(xn, hcur, wg1, wu1, wd1, i, tm, tf)
        z = _inproj(h1, gmx, w_in, i, tm, tn_in, sw + pw)

        ga_p, sr_p, si_p = _ssm(z, 0, mp, d_t, ops, i, n_groups, chunk, n_chunks, gpb=gpb)
        s0 = (state_ssm_re[i].transpose(1, 2, 0), state_ssm_im[i].transpose(1, 2, 0))
        ga_s, sr_s, si_s = _ssm(z, mp // ms, ms, d_t, ops, i, n_groups, dseq, 1, s0=s0, gpb=gpb)
        new_re_p.append(sr_p.transpose(2, 0, 1))
        new_im_p.append(si_p.transpose(2, 0, 1))
        new_re_s.append(sr_s.transpose(2, 0, 1))
        new_im_s.append(si_s.transpose(2, 0, 1))

        yb_p = _pool_seq(z, wpool, pscale, i, nb, seq, pw, sw // pw)
        yb_s, new_pool_s = _pool_step(z, state_pool, wpool, pscale, i, ms, sw // pw, carried=new_pool_s)
        new_pool_p.append(jnp.stack([z[(n + 1) * seq - buf:(n + 1) * seq, sw:sw + pw] for n in range(nb)]))

        merged = _mix(ga_p, ga_s, yb_p, yb_s, z, wga, wgb, wpu, i, tm2, sw + pw)
        h2, xn2 = _resmm(merged, wo, h1, gf2, i, tm2)
        h3 = _ffn(xn2, h2, wg2, wu2, wd2, i, tm, tf)
        if i + 1 < depth:
            hcur, xn = _ple(h3, gpl, p_p, p_s, wpl, wpg, i, tm2, g_next=gf1)
        else:
            y_p, y_s = _ple(h3, gpl, p_p, p_s, wpl, wpg, i, tm2, g_final=g_final.reshape(1, d))

    return (y_p.reshape(nb, seq, d), y_s.reshape(dseq, ns, d).transpose(1, 0, 2),
            jnp.stack(new_re_p), jnp.stack(new_im_p), jnp.stack(new_pool_p),
            jnp.stack(new_re_s), jnp.stack(new_im_s), new_pool_s)
```

```python
import functools
import math

import numpy as np
import jax
import jax.numpy as jnp
from jax import lax
from jax.experimental import pallas as pl
from jax.experimental.pallas import tpu as pltpu

F32 = jnp.float32
BF16 = jnp.bfloat16
RMS_EPS = 1e-6
POOL_WINDOWS = (2, 4, 8, 16)
SSM_GROUP = 16
SSM_CHUNK = 16
GELU_C = math.sqrt(2.0 / math.pi)
VMEM_LIMIT = 62 * 1024 * 1024
MXU_COLS = 256


def _cparams(sem):
    return pltpu.CompilerParams(dimension_semantics=sem, vmem_limit_bytes=VMEM_LIMIT)


def _rms_bf16(x, g):
    inv = lax.rsqrt(jnp.mean(x * x, axis=-1, keepdims=True) + RMS_EPS)
    return (x * inv * g).astype(BF16)


def _dot(a, b):
    return jnp.dot(a, b, preferred_element_type=F32)


def _col_chunks(width, chunk):
    chunk = min(chunk, width)
    return [slice(c0, c0 + chunk) for c0 in range(0, width, chunk)]


def _ffn_body(xn_ref, h_ref, wg_ref, wu_ref, wd_ref, *rest, res_chunks):
    gn_ref, o_ref, xo_ref = rest if len(rest) == 3 else (None,) + rest + (None,)
    j = pl.program_id(1)
    nj = pl.num_programs(1)

    @pl.when(j == 0)
    def _():
        o_ref[...] = jnp.zeros_like(o_ref)

    xn = xn_ref[...]
    mids = []
    for cs in _col_chunks(wg_ref.shape[1], MXU_COLS):
        a = _dot(xn, wg_ref[:, cs].astype(BF16))
        b = _dot(xn, wu_ref[:, cs].astype(BF16))
        mids.append((0.5 * a * jax.nn.sigmoid(a) * b).astype(BF16))
    mid = jnp.concatenate(mids, axis=1)
    for cs in _col_chunks(o_ref.shape[1], 2 * MXU_COLS):
        o_ref[:, cs] += _dot(mid, wd_ref[:, cs].astype(BF16))

    cw = o_ref.shape[1] // res_chunks
    for c in range(res_chunks):
        @pl.when(j == nj - res_chunks + c)
        def _(c=c):
            o_ref[:, c * cw:(c + 1) * cw] += h_ref[...]

    if xo_ref is not None:
        @pl.when(j == nj - 1)
        def _():
            xo_ref[...] = _rms_bf16(o_ref[...], gn_ref[...])


def _ffn(xn, hres, wg, wu, wd, layer, tm, tf, g_next=None):
    m, d = xn.shape
    f = wg.shape[-1]
    nj = f // tf
    res_chunks = min(4, nj)
    row = lambda: pl.BlockSpec((tm, d), lambda i, j: (i, 0))
    in_specs = [
        pl.BlockSpec((tm, d), lambda i, j: (i, 0)),
        pl.BlockSpec((tm, d // res_chunks),
                     lambda i, j: (i, jnp.clip(j - (nj - res_chunks), 0, res_chunks - 1))),
        pl.BlockSpec((None, d, tf), lambda i, j: (layer, 0, j)),
        pl.BlockSpec((None, d, tf), lambda i, j: (layer, 0, j)),
        pl.BlockSpec((None, tf, d), lambda i, j: (layer, j, 0)),
    ]
    args = [xn, hres, wg, wu, wd]
    out_specs, out_shape = row(), jax.ShapeDtypeStruct((m, d), F32)
    if g_next is not None:
        in_specs.append(pl.BlockSpec((None, 1, d), lambda i, j: (layer, 0, 0)))
        args.append(g_next)
        out_specs, out_shape = [row(), row()], [out_shape, jax.ShapeDtypeStruct((m, d), BF16)]
    return pl.pallas_call(
        functools.partial(_ffn_body, res_chunks=res_chunks),
        grid=(m // tm, nj),
        in_specs=in_specs,
        out_specs=out_specs,
        out_shape=out_shape,
        compiler_params=_cparams(("parallel", "arbitrary")),
        name="ffn",
    )(*args)


def _inproj_body(x_ref, g_ref, w_ref, o_ref, *, gate_from):
    is_gate = pl.program_id(0) >= gate_from
    xn = _rms_bf16(x_ref[...], g_ref[...])
    for cs in _col_chunks(o_ref.shape[1], MXU_COLS):
        r = _dot(xn, w_ref[:, cs].astype(BF16))
        o_ref[:, cs] = jnp.where(is_gate, jax.nn.sigmoid(r), r)


def _inproj(x, g, w, layer, tm, tn, n_plain):
    m, d = x.shape
    n = w.shape[-1]
    return pl.pallas_call(
        functools.partial(_inproj_body, gate_from=n_plain // tn),
        grid=(n // tn, m // tm),
        in_specs=[
            pl.BlockSpec((tm, d), lambda j, i: (i, 0)),
            pl.BlockSpec((None, 1, d), lambda j, i: (layer, 0, 0)),
            pl.BlockSpec((None, d, tn), lambda j, i: (layer, 0, j), pipeline_mode=pl.Buffered(1)),
        ],
        out_specs=pl.BlockSpec((tm, tn), lambda j, i: (i, j)),
        out_shape=jax.ShapeDtypeStruct((m, n), F32),
        compiler_params=_cparams(("arbitrary", "arbitrary")),
        name="inproj",
    )(x, g, w)


def _cmul(ar, ai, br, bi):
    return ar * br - ai * bi, ar * bi + ai * br


def _dot_nt3(a, b):
    nt = (((1,), (1,)), ((), ()))
    dot = lambda x, y: lax.dot_general(x, y, nt, preferred_element_type=F32)
    ah, bh = a.astype(BF16), b.astype(BF16)
    al = (a - ah.astype(F32)).astype(BF16)
    bl = (b - bh.astype(F32)).astype(BF16)
    return dot(ah, bh) + dot(ah, bl) + dot(al, bh)


def _ssm_weights_body(ar_ref, ai_ref, ldt_ref, btr_ref, bti_ref, ctr_ref, cti_ref, e_col, e_row,
                      tt_o, wor_o, woi_o, wstr_o, wsti_o, scr_o, sci_o, *, chunk, gblk):
    h = SSM_GROUP
    causal = e_col[...] >= e_row[...]
    nt = (((1,), (1,)), ((), ()))
    hi = lax.Precision.HIGHEST

    def rows(pows):
        width = pows[0][0].shape[1]
        return tuple(jnp.concatenate([jnp.broadcast_to(x[k], (h, width)) for x in pows], axis=0)
                     for k in (0, 1))

    for gl in range(gblk):
        dt = jnp.exp(ldt_ref[gl])
        a_re, a_im = ar_ref[gl], ai_ref[gl]
        mag = jnp.exp(a_re * dt)
        ang = a_im * dt
        lr, li = mag * jnp.cos(ang), mag * jnp.sin(ang)
        den = a_re * a_re + a_im * a_im
        num_re = lr - 1.0
        k_re = (num_re * a_re + li * a_im) / den
        k_im = (li * a_re - num_re * a_im) / den
        inv = 1.0 / (lr * lr + li * li)
        nr, ni = lr * inv, -li * inv
        pw = [(jnp.ones_like(lr), jnp.zeros_like(lr))]
        npw = list(pw)
        for _ in range(chunk):
            pw.append(_cmul(*pw[-1], lr, li))
            npw.append(_cmul(*npw[-1], nr, ni))
        tile = lambda x: jnp.concatenate([x] * chunk, axis=0)
        kb = _cmul(k_re, k_im, tile(btr_ref[gl]), tile(bti_ref[gl]))
        c = (tile(ctr_ref[gl]), tile(cti_ref[gl]))

        l_re, l_im = _cmul(*c, *rows(pw[:chunk]))
        r_re, r_im = _cmul(*rows(npw[:chunk]), *kb)
        kmat = _dot_nt3(l_re, r_re) - _dot_nt3(l_im, r_im)
        tt_o[gl] = jnp.where(causal, kmat, 0.0).astype(BF16)

        e_re, e_im = _cmul(*c, *rows(pw[1:chunk + 1]))
        wor_o[gl] = e_re.astype(BF16)
        woi_o[gl] = (-e_im).astype(BF16)

        s_re, s_im = _cmul(*rows(pw[chunk - 1::-1]), *kb)
        wstr_o[gl] = s_re.astype(BF16)
        wsti_o[gl] = s_im.astype(BF16)

        sc = [pw[chunk]]
        for _ in range(6):
            sc.append(_cmul(*sc[-1], *sc[-1]))
        sc.append(pw[chunk // 2])
        scr_o[gl] = jnp.concatenate([x[0] for x in sc], axis=0)
        sci_o[gl] = jnp.concatenate([x[1] for x in sc], axis=0)


def _ssm_weights(a_re, a_im, log_dt, b_re, b_im, c_re, c_im, chunk):
    dg, p = a_re.shape
    h = b_re.shape[-1]
    th = chunk * h
    tau = np.repeat(np.arange(chunk, dtype=np.float32), h)
    e_col = jnp.asarray(tau.reshape(th, 1))
    e_row = jnp.asarray(tau.reshape(1, th))
    gblk = min(16, dg)
    row = lambda x: x.reshape(dg, 1, p)
    bt = lambda x: jnp.swapaxes(x, 1, 2)
    per_g = lambda *s: pl.BlockSpec((gblk,) + s, lambda g: (g,) + (0,) * len(s))
    const = lambda *s: pl.BlockSpec(s, lambda g: (0,) * len(s))
    tt, wor, woi, wstr, wsti, scr, sci = pl.pallas_call(
        functools.partial(_ssm_weights_body, chunk=chunk, gblk=gblk),
        grid=(dg // gblk,),
        in_specs=[per_g(1, p), per_g(1, p), per_g(1, 1),
                  per_g(h, p), per_g(h, p), per_g(h, p), per_g(h, p),
                  const(th, 1), const(1, th)],
        out_specs=[per_g(th, th), per_g(th, p), per_g(th, p), per_g(th, p), per_g(th, p),
                   per_g(8, p), per_g(8, p)],
        out_shape=[jax.ShapeDtypeStruct((dg, th, th), BF16),
                   jax.ShapeDtypeStruct((dg, th, p), BF16),
                   jax.ShapeDtypeStruct((dg, th, p), BF16),
                   jax.ShapeDtypeStruct((dg, th, p), BF16),
                   jax.ShapeDtypeStruct((dg, th, p), BF16),
                   jax.ShapeDtypeStruct((dg, 8, p), F32),
                   jax.ShapeDtypeStruct((dg, 8, p), F32)],
        compiler_params=_cparams(("parallel",)),
        name="ssm_weights",
    )(row(a_re), row(a_im), log_dt.reshape(dg, 1, 1),
      bt(b_re), bt(b_im), c_re, c_im, e_col, e_row)
    sw = lambda x: jnp.swapaxes(x, 1, 2)
    return tt, wor, woi, sw(wstr), sw(wsti), sw(scr), sw(sci), scr, sci


def _ssm_body(*refs, steps, chunks, gpb, has_init):
    if has_init:
        (u_ref, d_ref, tt_ref, wor_ref, woi_ref, wsr_ref, wsi_ref, scr_ref, sci_ref,
         s0r_ref, s0i_ref, y_ref, sfr_ref, sfi_ref, ys_ref, xr_ref, xi_ref) = refs
    else:
        (u_ref, d_ref, tt_ref, wor_ref, woi_ref, wsr_ref, wsi_ref, scr_ref, sci_ref,
         y_ref, sfr_ref, sfi_ref, ys_ref, xr_ref, xi_ref, st_r, st_i) = refs
    h = SSM_GROUP
    th = steps * h
    w = u_ref.shape[0] // steps
    p = wsr_ref.shape[1]
    ws_off = wsr_ref.shape[2] - th

    def step_rows(t):
        return pl.ds(t, w, stride=steps) if chunks > 1 else pl.ds(t * w, w)

    slabs = [u_ref[step_rows(t), :].T for t in range(steps)]

    for gl in range(gpb):
        rows = slice(gl * h, (gl + 1) * h)
        u = jnp.concatenate([s[rows, :] for s in slabs], axis=0)
        ub = u.astype(BF16)
        y = _dot(tt_ref[gl, :th, :th], ub) + jnp.concatenate([d_ref[gl]] * steps, axis=0) * u
        xr_ref[gl * p:(gl + 1) * p, :] = _dot(wsr_ref[gl, :, ws_off:], ub)
        xi_ref[gl * p:(gl + 1) * p, :] = _dot(wsi_ref[gl, :, ws_off:], ub)
        for t in range(steps):
            ys_ref[t, rows, :] = y[t * h:(t + 1) * h, :]

    gp = gpb * p
    col = lambda ref, k: ref[:, :, k:k + 1].reshape(gp, 1)
    xr, xi = xr_ref[...], xi_ref[...]
    if chunks > 1:
        pad = st_r.shape[0] - w
        c_row = lax.broadcasted_iota(jnp.int32, (w, 1), 0) & (chunks - 1)
        st_r[:pad, :] = jnp.zeros((pad, gp), F32)
        st_i[:pad, :] = jnp.zeros((pad, gp), F32)
        st_r[pad:, :] = xr.T
        st_i[pad:, :] = xi.T
        k = 0
        while (1 << k) < chunks:
            sh = 1 << k
            rr, ri = st_r[pl.ds(pad - sh, w), :], st_i[pl.ds(pad - sh, w), :]
            mr, mi = scr_ref[k:k + 1, :], sci_ref[k:k + 1, :]
            keep = c_row >= sh
            st_r[pad:, :] += jnp.where(keep, mr * rr - mi * ri, 0.0)
            st_i[pad:, :] += jnp.where(keep, mr * ri + mi * rr, 0.0)
            k += 1
        first = c_row >= 1
        xr_ref[...] = jnp.where(first, st_r[pl.ds(pad - 1, w), :], 0.0).T
        xi_ref[...] = jnp.where(first, st_i[pl.ds(pad - 1, w), :], 0.0).T
        for n in range(w // chunks):
            last = pad + n * chunks + chunks - 1
            sfr_ref[n:n + 1, :] = st_r[last:last + 1, :]
            sfi_ref[n:n + 1, :] = st_i[last:last + 1, :]
    else:
        pr, pi = s0r_ref[...].reshape(gp, w), s0i_ref[...].reshape(gp, w)
        lr, li = col(scr_ref, 7), col(sci_ref, 7)
        sfr_ref[...] = (lr * pr - li * pi + xr).reshape(gpb, p, w)
        sfi_ref[...] = (lr * pi + li * pr + xi).reshape(gpb, p, w)
        xr_ref[...] = pr
        xi_ref[...] = pi

    for gl in range(gpb):
        rows = slice(gl * h, (gl + 1) * h)
        pr = xr_ref[gl * p:(gl + 1) * p, :].astype(BF16)
        pi = xi_ref[gl * p:(gl + 1) * p, :].astype(BF16)
        ya = _dot(wor_ref[gl, :th, :], pr) + _dot(woi_ref[gl, :th, :], pi)
        for t in range(steps):
            y = ys_ref[t, rows, :] + ya[t * h:(t + 1) * h, :]
            ys_ref[t, rows, :] = 0.5 * y * (1.0 + jnp.tanh(GELU_C * (y + 0.044715 * (y * y * y))))
    for t in range(steps):
        y_ref[step_rows(t), :] = ys_ref[t].T


def _ssm(z, row_blk, rows, d_t, ops, layer, n_groups, steps, chunks, s0=None, gpb=8):
    tt, wor, woi, wsr, wsi, scr, sci, sc_rows_r, sc_rows_i = ops
    h = SSM_GROUP
    w = rows // steps
    nseq = w // chunks
    p = scr.shape[1]
    gp = gpb * p
    thf = tt.shape[1]
    nblk = n_groups // gpb
    base = layer * nblk
    wblk = lambda *s: pl.BlockSpec((gpb,) + s, lambda g: (base + g,) + (0,) * len(s))
    scratch = [pltpu.VMEM((steps, gpb * h, w), F32), pltpu.VMEM((gp, w), F32), pltpu.VMEM((gp, w), F32)]
    if s0 is None:
        sc_spec = pl.BlockSpec((None, sc_rows_r.shape[1], gp), lambda g: (base + g, 0, 0))
        sc_args = [sc_rows_r, sc_rows_i]
        sf_spec = pl.BlockSpec((None, nseq, gp), lambda g: (g, 0, 0))
        sf_shape = jax.ShapeDtypeStruct((nblk, nseq, gp), F32)
        scratch += [pltpu.VMEM((chunks + w, gp), F32)] * 2
    else:
        sc_spec = wblk(p, scr.shape[2])
        sc_args = [scr, sci]
        sf_spec = pl.BlockSpec((gpb, p, nseq), lambda g: (g, 0, 0))
        sf_shape = jax.ShapeDtypeStruct((n_groups, p, nseq), F32)
    in_specs = [pl.BlockSpec((rows, gpb * h), lambda g: (row_blk, g)),
                wblk(h, 1), wblk(thf, thf), wblk(thf, p), wblk(thf, p), wblk(p, thf), wblk(p, thf),
                sc_spec, sc_spec]
    args = [z, d_t, tt, wor, woi, wsr, wsi] + sc_args
    if s0 is not None:
        in_specs += [pl.BlockSpec((gpb, p, w), lambda g: (g, 0, 0))] * 2
        args += list(s0)
    return pl.pallas_call(
        functools.partial(_ssm_body, steps=steps, chunks=chunks, gpb=gpb, has_init=s0 is not None),
        grid=(nblk,),
        in_specs=in_specs,
        out_specs=[pl.BlockSpec((rows, gpb * h), lambda g: (0, g)), sf_spec, sf_spec],
        out_shape=[jax.ShapeDtypeStruct((rows, n_groups * h), F32), sf_shape, sf_shape],
        scratch_shapes=scratch,
        compiler_params=_cparams(("parallel",)),
        name="ssm_chunks" if s0 is None else "ssm_step",
    )(*args)


def _pool_seq_body(u_ref, w_ref, sc_ref, o_ref, z_ref):
    l, c = u_ref.shape
    pad = z_ref.shape[0] - l
    cg = c // len(POOL_WINDOWS)
    z_ref[:pad, :] = jnp.zeros((pad, c), F32)
    z_ref[pad:, :] = u_ref[...]
    pos1 = (lax.broadcasted_iota(jnp.int32, (l, 1), 0) + 1).astype(F32)
    for gi, win in enumerate(POOL_WINDOWS):
        cols = slice(gi * cg, (gi + 1) * cg)
        cur = z_ref[pad:, cols]
        tot = cur
        for k in range(1, win):
            tot = tot + z_ref[pad - k:pad - k + l, cols]
        inv_cnt = 1.0 / jnp.minimum(pos1, float(win))
        mixed = _dot((tot * inv_cnt - cur).astype(BF16), w_ref[gi].astype(BF16))
        o_ref[:, cols] = (mixed * sc_ref[:, cols]).astype(BF16)


def _pool_seq(z, w_pool, scale, layer, nseq, seqlen, width, col_blk):
    return pl.pallas_call(
        _pool_seq_body,
        grid=(nseq,),
        in_specs=[pl.BlockSpec((seqlen, width), lambda n: (n, col_blk)),
                  pl.BlockSpec((None,) + w_pool.shape[1:], lambda n: (layer, 0, 0, 0)),
                  pl.BlockSpec((None, 1, width), lambda n: (layer, 0, 0))],
        out_specs=pl.BlockSpec((seqlen, width), lambda n: (n, 0)),
        out_shape=jax.ShapeDtypeStruct((nseq * seqlen, width), BF16),
        scratch_shapes=[pltpu.VMEM((seqlen + 16, width), F32)],
        compiler_params=_cparams(("parallel",)),
        name="pool_seq",
    )(z, w_pool, scale)


def _pool_step_body(u_ref, prev_ref, w_ref, sc_ref, *rest):
    o_ref, nxt_ref = rest[-2:]
    n, buf, c = prev_ref.shape
    steps = u_ref.shape[0] // n
    cg = c // len(POOL_WINDOWS)

    def row(j, cols):
        return prev_ref[:, j, cols] if j < buf else u_ref[(j - buf) * n:(j - buf + 1) * n, cols]

    for gi, win in enumerate(POOL_WINDOWS):
        cols = slice(gi * cg, (gi + 1) * cg)
        for t in range(steps):
            cur = row(buf + t, cols)
            tot = cur
            for k in range(1, win):
                tot = tot + row(buf + t - k, cols)
            mixed = _dot((tot * (1.0 / win) - cur).astype(BF16), w_ref[gi].astype(BF16))
            o_ref[t * n:(t + 1) * n, cols] = (mixed * sc_ref[:, cols]).astype(BF16)
    for j in range(buf):
        nxt_ref[:, j, :] = row(j + steps, slice(None))


def _pool_step(z, prev, w_pool, scale, layer, rows, col_blk, carried=None):
    n, buf, c = prev.shape[1:]
    row_blk = z.shape[0] // rows - 1
    in_specs = [pl.BlockSpec((rows, c), lambda i: (row_blk, col_blk)),
                pl.BlockSpec((None, n, buf, c), lambda i: (layer, 0, 0, 0)),
                pl.BlockSpec((None,) + w_pool.shape[1:], lambda i: (layer, 0, 0, 0)),
                pl.BlockSpec((None, 1, c), lambda i: (layer, 0, 0))]
    args = [z, prev, w_pool, scale]
    aliases = {}
    if carried is not None:
        in_specs.append(pl.BlockSpec(memory_space=pl.ANY))
        args.append(carried)
        aliases = {len(args) - 1: 1}
    return pl.pallas_call(
        _pool_step_body,
        grid=(1,),
        in_specs=in_specs,
        out_specs=[pl.BlockSpec((rows, c), lambda i: (0, 0)),
                   pl.BlockSpec((None, n, buf, c), lambda i: (layer, 0, 0, 0))],
        out_shape=[jax.ShapeDtypeStruct((rows, c), BF16), jax.ShapeDtypeStruct(prev.shape, F32)],
        input_output_aliases=aliases,
        compiler_params=_cparams(("arbitrary",)),
        name="pool_step",
    )(*args)


def _mix_body(gap_ref, gas_ref, ybp_ref, ybs_ref, sa_ref, sb_ref, wa_ref, wb_ref, wp_ref, o_ref, *, p_tiles):
    first = pl.program_id(0) < p_tiles
    ga = jnp.where(first, gap_ref[...], gas_ref[...]).astype(BF16)
    yb =jnp.where(first, ybp_ref[...], ybs_ref[...])
    for cs in _col_chunks(o_ref.shape[1], MXU_COLS):
        br_a = _dot(ga, wa_ref[:, cs].astype(BF16)) * jax.nn.sigmoid(_dot(ga, wb_ref[:, cs].astype(BF16)))
        br_b = _dot(yb, wp_ref[:, cs].astype(BF16))
        o_ref[:, cs] = (sa_ref[:, cs] * br_a + sb_ref[:, cs] * br_b).astype(BF16)


def _resident(shape, layer):
    return pl.BlockSpec((None,) + shape, lambda i: (layer,) + (0,) * len(shape), pipeline_mode=pl.Buffered(1))


def _split_specs(tm, width, p_tiles):
    return [pl.BlockSpec((tm, width), lambda i: (jnp.minimum(i, p_tiles - 1), 0)),
            pl.BlockSpec((tm, width), lambda i: (jnp.maximum(i - p_tiles, 0), 0))]


def _mix(ga_p, ga_s, yb_p, yb_s, z, wa, wb, wp, layer, tm, gate_col):
    m = z.shape[0]
    k = ga_p.shape[1]
    n = wa.shape[-1]
    gblk = gate_col // n
    p_tiles = ga_p.shape[0] // tm
    return pl.pallas_call(
        functools.partial(_mix_body, p_tiles=p_tiles),
        grid=(m // tm,),
        in_specs=_split_specs(tm, k, p_tiles) + _split_specs(tm, k, p_tiles) + [
                  pl.BlockSpec((tm, n), lambda i: (i, gblk)),
                  pl.BlockSpec((tm, n), lambda i: (i, gblk + 1)),
                  _resident((k, n), layer), _resident((k, n), layer), _resident((k, n), layer)],
        out_specs=pl.BlockSpec((tm, n), lambda i: (i, 0)),
        out_shape=jax.ShapeDtypeStruct((m, n), BF16),
        compiler_params=_cparams(("parallel",)),
        name="mix",
    )(ga_p, ga_s, yb_p, yb_s, z, z, wa, wb, wp)


def _resmm_body(a_ref, w_ref, h_ref, g_ref, o_ref, xn_ref):
    a = a_ref[...]
    for cs in _col_chunks(o_ref.shape[1], MXU_COLS):
        o_ref[:, cs] = h_ref[:, cs] + _dot(a, w_ref[:, cs].astype(BF16))
    xn_ref[...] = _rms_bf16(o_ref[...], g_ref[...])


def _resmm(a, w, hres, g_next, layer, tm):
    m, k = a.shape
    n = w.shape[-1]
    return pl.pallas_call(
        _resmm_body,
        grid=(m // tm,),
        in_specs=[pl.BlockSpec((tm, k), lambda i: (i, 0)),
                  _resident((k, n), layer),
                  pl.BlockSpec((tm, n), lambda i: (i, 0)),
                  pl.BlockSpec((None, 1, n), lambda i: (layer, 0, 0))],
        out_specs=[pl.BlockSpec((tm, n), lambda i: (i, 0)), pl.BlockSpec((tm, n), lambda i: (i, 0))],
        out_shape=[jax.ShapeDtypeStruct((m, n), F32), jax.ShapeDtypeStruct((m, n), BF16)],
        compiler_params=_cparams(("parallel",)),
        name="resmm",
    )(a, w, hres, g_next)


def _ple_update(x_ref, g_ref, pp_ref, ps_ref, wp_ref, wg_ref, dst_ref, p_tiles):
    xn = _rms_bf16(x_ref[...], g_ref[...])
    pb = jnp.where(pl.program_id(0) < p_tiles, pp_ref[...], ps_ref[...]).astype(BF16)
    for cs in _col_chunks(dst_ref.shape[1], MXU_COLS):
        gate = jax.nn.sigmoid(_dot(xn, wg_ref[:, cs].astype(BF16)))
        dst_ref[:, cs] = x_ref[:, cs] + _dot(pb, wp_ref[:, cs].astype(BF16)) * gate


def _ple_body(x_ref, g_ref, pp_ref, ps_ref, wp_ref, wg_ref, gn_ref, o_ref, xn_ref, *, p_tiles):
    _ple_update(x_ref, g_ref, pp_ref, ps_ref, wp_ref, wg_ref, o_ref, p_tiles)
    xn_ref[...] = _rms_bf16(o_ref[...], gn_ref[...])


def _ple_final_body(x_ref, g_ref, pp_ref, ps_ref, wp_ref, wg_ref, gf_ref, op_ref, os_ref, h_ref, *, p_tiles):
    _ple_update(x_ref, g_ref, pp_ref, ps_ref, wp_ref, wg_ref, h_ref, p_tiles)
    h = h_ref[...]
    y = h * lax.rsqrt(jnp.mean(h * h, axis=-1, keepdims=True) + RMS_EPS) * gf_ref[...]
    i = pl.program_id(0)

    @pl.when(i < p_tiles)
    def _():
        op_ref[...] = y

    @pl.when(i >= p_tiles)
    def _():
        os_ref[...] = y


def _ple(x, g, p_p, p_s, wp, wg, layer, tm, g_next=None, g_final=None):
    m, d = x.shape
    mp, pd = p_p.shape[1:]
    p_tiles = mp // tm
    in_specs = [pl.BlockSpec((tm, d), lambda i: (i, 0)),
                pl.BlockSpec((None, 1, d), lambda i: (layer, 0, 0)),
                pl.BlockSpec((None, tm, pd), lambda i: (layer, jnp.minimum(i, p_tiles - 1), 0)),
                pl.BlockSpec((None, tm, pd), lambda i: (layer, jnp.maximum(i - p_tiles, 0), 0)),
                _resident((pd, d), layer), _resident((d, d), layer)]
    if g_final is None:
        row = pl.BlockSpec((tm, d), lambda i: (i, 0))
        return pl.pallas_call(
            functools.partial(_ple_body, p_tiles=p_tiles), grid=(m // tm,),
            in_specs=in_specs + [pl.BlockSpec((None, 1, d), lambda i: (layer + 1, 0, 0))],
            out_specs=[row, row],
            out_shape=[jax.ShapeDtypeStruct((m, d), F32), jax.ShapeDtypeStruct((m, d), BF16)],
            compiler_params=_cparams(("parallel",)),
            name="ple",
        )(x, g, p_p, p_s, wp, wg, g_next)
    return pl.pallas_call(
        functools.partial(_ple_final_body, p_tiles=p_tiles),
        grid=(m // tm,),
        in_specs=in_specs + [pl.BlockSpec((1, d), lambda i: (0, 0))],
        out_specs=[pl.BlockSpec((tm, d), lambda i: (jnp.minimum(i, p_tiles - 1), 0)),
                   pl.BlockSpec((tm, d), lambda i: (jnp.maximum(i - p_tiles, 0), 0))],
        out_shape=[jax.ShapeDtypeStruct((mp, d), F32), jax.ShapeDtypeStruct((m - mp, d), F32)],
        scratch_shapes=[pltpu.VMEM((tm, d), F32)],
        compiler_params=_cparams(("arbitrary",)),
        name="ple_final",
    )(x, g, p_p, p_s, wp, wg, g_final)


def _prep_body(xp_ref, xs_ref, g_ref, h_ref, xn_ref, *, p_tiles):
    x = jnp.where(pl.program_id(0) < p_tiles, xp_ref[...], xs_ref[...])
    h_ref[...] = x
    xn_ref[...] = _rms_bf16(x, g_ref[...])


def _prep(x_p, x_s, g, tm):
    d = x_p.shape[1]
    m = x_p.shape[0] + x_s.shape[0]
    p_tiles = x_p.shape[0] // tm
    row = pl.BlockSpec((tm, d), lambda i: (i, 0))
    return pl.pallas_call(
        functools.partial(_prep_body, p_tiles=p_tiles),
        grid=(m // tm,),
        in_specs=_split_specs(tm, d, p_tiles) + [pl.BlockSpec((None, 1, d), lambda i: (0, 0, 0))],
        out_specs=[row, row],
        out_shape=[jax.ShapeDtypeStruct((m, d), F32), jax.ShapeDtypeStruct((m, d), BF16)],
        compiler_params=_cparams(("parallel",)),
        name="prep",
    )(x_p, x_s, g)


def _pick_tile(n, pref):
    t = min(pref, n)
    while n % t:
        t //= 2
    return t


def kernel(x_prompt, x_sample, state_ssm_re, state_ssm_im, state_pool, p_prompt, p_sample, g_ffn1, w_ffn1_gate, w_ffn1_up, w_ffn1_down, g_mix, w_in, ssm_a_re, ssm_a_im, ssm_log_dt, ssm_b_re, ssm_b_im, ssm_c_re, ssm_c_im, ssm_d, w_glu_a, w_glu_b, w_pool, pool_scale, w_pool_up, w_out, g_ffn2, w_ffn2_gate, w_ffn2_up, w_ffn2_down, g_ple, w_ple, w_ple_gate, g_final):
    nb, seq, d = x_prompt.shape
    ns, dseq, _ = x_sample.shape
    depth, n_groups, p_state = ssm_a_re.shape
    h = ssm_b_re.shape[-1]
    sw = n_groups * h
    pw = pool_scale.shape[-1]
    buf = state_pool.shape[2]
    chunk = SSM_CHUNK
    assert h == SSM_GROUP and dseq * 2 == chunk and seq % chunk == 0 and buf == max(POOL_WINDOWS) - 1
    n_chunks = seq // chunk
    assert n_chunks & (n_chunks - 1) == 0 and n_chunks <= 128
    mp, ms = nb * seq, ns * dseq
    m = mp + ms
    tm = _pick_tile(m, 1024)

    g3 = lambda a: a.reshape(depth, 1, -1)
    wg1, wu1, wd1 = w_ffn1_gate, w_ffn1_up, w_ffn1_down
    wg2, wu2, wd2 = w_ffn2_gate, w_ffn2_up, w_ffn2_down
    wga, wgb, wpu, wo = w_glu_a, w_glu_b, w_pool_up, w_out
    wpl, wpg, wpool = w_ple, w_ple_gate, w_pool
    tn_in = _pick_tile(sw + pw, 2048)
    gf1, gmx, gf2, gpl = g3(g_ffn1), g3(g_mix), g3(g_ffn2), g3(g_ple)
    pscale = g3(pool_scale)

    flat = lambda a: a.reshape((depth * n_groups,) + a.shape[2:])
    gpb = min(8, n_groups)
    ops = _ssm_weights(flat(ssm_a_re), flat(ssm_a_im), flat(ssm_log_dt), flat(ssm_b_re), flat(ssm_b_im),
                       flat(ssm_c_re), flat(ssm_c_im), chunk)
    blocked = lambda a: (a.reshape(-1, gpb, a.shape[1], p_state).transpose(0, 2, 1, 3)
                         .reshape(-1, a.shape[1], gpb * p_state))
    ops = ops[:-2] + (blocked(ops[-2]), blocked(ops[-1]))
    d_t = flat(ssm_d).reshape(depth * n_groups, h, 1)

    p_p = p_prompt.reshape(depth, mp, -1)
    p_s = p_sample.transpose(0, 2, 1, 3).reshape(depth, ms, -1)
    tf = _pick_tile(w_ffn1_gate.shape[-1], 512)
    tm2 = _pick_tile(math.gcd(mp, ms), 512)
    hcur, xn = _prep(x_prompt.reshape(mp, d), x_sample.transpose(1, 0, 2).reshape(ms, d), gf1, tm2)
    new_re_p, new_im_p, new_pool_p, new_re_s, new_im_s = [], [], [], [], []
    new_pool_s = None
    for i in range(depth):
        h1 = _ffn(xn, hcur, wg1, wu1, wd1, i, tm, tf)
        z = _inproj(h1, gmx, w_in, i, tm, tn_in, sw + pw)

        ga_p, sr_p, si_p = _ssm(z, 0, mp, d_t, ops, i, n_groups, chunk, n_chunks, gpb=gpb)
        s0 = (state_ssm_re[i].transpose(1, 2, 0), state_ssm_im[i].transpose(1, 2, 0))
        ga_s, sr_s, si_s = _ssm(z, mp // ms, ms, d_t, ops, i, n_groups, dseq, 1, s0=s0, gpb=gpb)
        unblock = lambda a: (a.reshape(-1, nb, gpb, p_state).transpose(1, 0, 2, 3)
                             .reshape(nb, n_groups, p_state))
        new_re_p.append(unblock(sr_p))
        new_im_p.append(unblock(si_p))
        new_re_s.append(sr_s.transpose(2, 0, 1))
        new_im_s.append(si_s.transpose(2, 0, 1))

        yb_p = _pool_seq(z, wpool, pscale, i, nb, seq, pw, sw // pw)
        yb_s, new_pool_s = _pool_step(z, state_pool, wpool, pscale, i, ms, sw // pw, carried=new_pool_s)
        new_pool_p.append(jnp.stack([z[(n + 1) * seq - buf:(n + 1) * seq, sw:sw + pw] for n in range(nb)]))

        merged = _mix(ga_p, ga_s, yb_p, yb_s, z, wga, wgb, wpu, i, tm2, sw + pw)
        h2, xn2 = _resmm(merged, wo, h1, gf2, i, tm2)
        h3 = _ffn(xn2, h2, wg2, wu2, wd2, i, tm, tf)
        if i + 1 < depth:
            hcur, xn = _ple(h3, gpl, p_p, p_s, wpl, wpg, i, tm2, g_next=gf1)
        else:
            y_p, y_s = _ple(h3, gpl, p_p, p_s, wpl, wpg, i, tm2, g_final=g_final.reshape(1, d))

    return (y_p.reshape(nb, seq, d), y_s.reshape(dseq, ns, d).transpose(1, 0, 2),
            jnp.stack(new_re_p), jnp.stack(new_im_p), jnp.stack(new_pool_p),
            jnp.stack(new_re_s), jnp.stack(new_im_s), new_pool_s)
```

```python
import functools
import math

import numpy as np
import jax
import jax.numpy as jnp
from jax import lax
from jax.experimental import pallas as pl
from jax.experimental.pallas import tpu as pltpu

F32 = jnp.float32
BF16 = jnp.bfloat16
RMS_EPS = 1e-6
POOL_WINDOWS = (2, 4, 8, 16)
SSM_GROUP = 16
SSM_CHUNK = 16
GELU_C = math.sqrt(2.0 / math.pi)
VMEM_LIMIT = 62 * 1024 * 1024
MXU_COLS = 256


def _cparams(sem):
    return pltpu.CompilerParams(dimension_semantics=sem, vmem_limit_bytes=VMEM_LIMIT)


def _rms_bf16(x, g):
    inv = lax.rsqrt(jnp.mean(x * x, axis=-1, keepdims=True) + RMS_EPS)
    return (x * inv * g).astype(BF16)


def _dot(a, b):
    return jnp.dot(a, b, preferred_element_type=F32)


def _col_chunks(width, chunk):
    chunk = min(chunk, width)
    return [slice(c0, c0 + chunk) for c0 in range(0, width, chunk)]


def _ffn_body(xn_ref, h_ref, wg_ref, wu_ref, wd_ref, *rest, res_chunks):
    gn_ref, o_ref, xo_ref = rest if len(rest) == 3 else (None,) + rest + (None,)
    j = pl.program_id(1)
    nj = pl.num_programs(1)

    @pl.when(j == 0)
    def _():
        o_ref[...] = jnp.zeros_like(o_ref)

    xn = xn_ref[...]
    mids = []
    for cs in _col_chunks(wg_ref.shape[1], MXU_COLS):
        a = _dot(xn, wg_ref[:, cs].astype(BF16))
        b = _dot(xn, wu_ref[:, cs].astype(BF16))
        mids.append((0.5 * a * jax.nn.sigmoid(a) * b).astype(BF16))
    mid = jnp.concatenate(mids, axis=1)
    for cs in _col_chunks(o_ref.shape[1], 2 * MXU_COLS):
        o_ref[:, cs] += _dot(mid, wd_ref[:, cs].astype(BF16))

    cw = o_ref.shape[1] // res_chunks
    for c in range(res_chunks):
        @pl.when(j == nj - res_chunks + c)
        def _(c=c):
            o_ref[:, c * cw:(c + 1) * cw] += h_ref[...]

    if xo_ref is not None:
        @pl.when(j == nj - 1)
        def _():
            xo_ref[...] = _rms_bf16(o_ref[...], gn_ref[...])


def _ffn(xn, hres, wg, wu, wd, layer, tm, tf, g_next=None):
    m, d = xn.shape
    f = wg.shape[-1]
    nj = f // tf
    res_chunks = min(4, nj)
    row = lambda: pl.BlockSpec((tm, d), lambda i, j: (i, 0))
    in_specs = [
        pl.BlockSpec((tm, d), lambda i, j: (i, 0)),
        pl.BlockSpec((tm, d // res_chunks),
                     lambda i, j: (i, jnp.clip(j - (nj - res_chunks), 0, res_chunks - 1))),
        pl.BlockSpec((None, d, tf), lambda i, j: (layer, 0, j)),
        pl.BlockSpec((None, d, tf), lambda i, j: (layer, 0, j)),
        pl.BlockSpec((None, tf, d), lambda i, j: (layer, j, 0)),
    ]
    args = [xn, hres, wg, wu, wd]
    out_specs, out_shape = row(), jax.ShapeDtypeStruct((m, d), F32)
    if g_next is not None:
        in_specs.append(pl.BlockSpec((None, 1, d), lambda i, j: (layer, 0, 0)))
        args.append(g_next)
        out_specs, out_shape = [row(), row()], [out_shape, jax.ShapeDtypeStruct((m, d), BF16)]
    return pl.pallas_call(
        functools.partial(_ffn_body, res_chunks=res_chunks),
        grid=(m // tm, nj),
        in_specs=in_specs,
        out_specs=out_specs,
        out_shape=out_shape,
        compiler_params=_cparams(("parallel", "arbitrary")),
        name="ffn",
    )(*args)


def _inproj_body(x_ref, g_ref, w_ref, o_ref, *, gate_from):
    is_gate = pl.program_id(0) >= gate_from
    xn = _rms_bf16(x_ref[...], g_ref[...])
    for cs in _col_chunks(o_ref.shape[1], MXU_COLS):
        r = _dot(xn, w_ref[:, cs].astype(BF16))
        o_ref[:, cs] = jnp.where(is_gate, jax.nn.sigmoid(r), r)


def _inproj(x, g, w, layer, tm, tn, n_plain):
    m, d = x.shape
    n = w.shape[-1]
    return pl.pallas_call(
        functools.partial(_inproj_body, gate_from=n_plain // tn),
        grid=(n // tn, m // tm),
        in_specs=[
            pl.BlockSpec((tm, d), lambda j, i: (i, 0)),
            pl.BlockSpec((None, 1, d), lambda j, i: (layer, 0, 0)),
            pl.BlockSpec((None, d, tn), lambda j, i: (layer, 0, j), pipeline_mode=pl.Buffered(1)),
        ],
        out_specs=pl.BlockSpec((tm, tn), lambda j, i: (i, j)),
        out_shape=jax.ShapeDtypeStruct((m, n), F32),
        compiler_params=_cparams(("arbitrary", "arbitrary")),
        name="inproj",
    )(x, g, w)


def _cmul(ar, ai, br, bi):
    return ar * br - ai * bi, ar * bi + ai * br


def _dot_nt3(a, b):
    nt = (((1,), (1,)), ((), ()))
    dot = lambda x, y: lax.dot_general(x, y, nt, preferred_element_type=F32)
    ah, bh = a.astype(BF16), b.astype(BF16)
    al = (a - ah.astype(F32)).astype(BF16)
    bl = (b - bh.astype(F32)).astype(BF16)
    return dot(ah, bh) + dot(ah, bl) + dot(al, bh)


def _ssm_weights_body(ar_ref, ai_ref, ldt_ref, btr_ref, bti_ref, ctr_ref, cti_ref, e_col, e_row,
                      tt_o, wor_o, woi_o, wstr_o, wsti_o, scr_o, sci_o, *, chunk, gblk):
    h = SSM_GROUP
    causal = e_col[...] >= e_row[...]
    nt = (((1,), (1,)), ((), ()))
    hi = lax.Precision.HIGHEST

    def rows(pows):
        width = pows[0][0].shape[1]
        return tuple(jnp.concatenate([jnp.broadcast_to(x[k], (h, width)) for x in pows], axis=0)
                     for k in (0, 1))

    for gl in range(gblk):
        dt = jnp.exp(ldt_ref[gl])
        a_re, a_im = ar_ref[gl], ai_ref[gl]
        mag = jnp.exp(a_re * dt)
        ang = a_im * dt
        lr, li = mag * jnp.cos(ang), mag * jnp.sin(ang)
        den = a_re * a_re + a_im * a_im
        num_re = lr - 1.0
        k_re = (num_re * a_re + li * a_im) / den
        k_im = (li * a_re - num_re * a_im) / den
        inv = 1.0 / (lr * lr + li * li)
        nr, ni = lr * inv, -li * inv
        pw = [(jnp.ones_like(lr), jnp.zeros_like(lr))]
        npw = list(pw)
        for _ in range(chunk):
            pw.append(_cmul(*pw[-1], lr, li))
            npw.append(_cmul(*npw[-1], nr, ni))
        tile = lambda x: jnp.concatenate([x] * chunk, axis=0)
        kb = _cmul(k_re, k_im, tile(btr_ref[gl]), tile(bti_ref[gl]))
        c = (tile(ctr_ref[gl]), tile(cti_ref[gl]))

        l_re, l_im = _cmul(*c, *rows(pw[:chunk]))
        r_re, r_im = _cmul(*rows(npw[:chunk]), *kb)
        kmat = _dot_nt3(l_re, r_re) - _dot_nt3(l_im, r_im)
        tt_o[gl] = jnp.where(causal, kmat, 0.0).astype(BF16)

        e_re, e_im = _cmul(*c, *rows(pw[1:chunk + 1]))
        wor_o[gl] = e_re.astype(BF16)
        woi_o[gl] = (-e_im).astype(BF16)

        s_re, s_im = _cmul(*rows(pw[chunk - 1::-1]), *kb)
        wstr_o[gl] = s_re.astype(BF16)
        wsti_o[gl] = s_im.astype(BF16)

        sc = [pw[chunk]]
        for _ in range(6):
            sc.append(_cmul(*sc[-1], *sc[-1]))
        sc.append(pw[chunk // 2])
        scr_o[gl] = jnp.concatenate([x[0] for x in sc], axis=0)
        sci_o[gl] = jnp.concatenate([x[1] for x in sc], axis=0)


def _ssm_weights(a_re, a_im, log_dt, b_re, b_im, c_re, c_im, chunk):
    dg, p = a_re.shape
    h = b_re.shape[-1]
    th = chunk * h
    tau = np.repeat(np.arange(chunk, dtype=np.float32), h)
    e_col = jnp.asarray(tau.reshape(th, 1))
    e_row = jnp.asarray(tau.reshape(1, th))
    gblk = min(16, dg)
    row = lambda x: x.reshape(dg, 1, p)
    bt = lambda x: jnp.swapaxes(x, 1, 2)
    per_g = lambda *s: pl.BlockSpec((gblk,) + s, lambda g: (g,) + (0,) * len(s))
    const = lambda *s: pl.BlockSpec(s, lambda g: (0,) * len(s))
    tt, wor, woi, wstr, wsti, scr, sci = pl.pallas_call(
        functools.partial(_ssm_weights_body, chunk=chunk, gblk=gblk),
        grid=(dg // gblk,),
        in_specs=[per_g(1, p), per_g(1, p), per_g(1, 1),
                  per_g(h, p), per_g(h, p), per_g(h, p), per_g(h, p),
                  const(th, 1), const(1, th)],
        out_specs=[per_g(th, th), per_g(th, p), per_g(th, p), per_g(th, p), per_g(th, p),
                   per_g(8, p), per_g(8, p)],
        out_shape=[jax.ShapeDtypeStruct((dg, th, th), BF16),
                   jax.ShapeDtypeStruct((dg, th, p), BF16),
                   jax.ShapeDtypeStruct((dg, th, p), BF16),
                   jax.ShapeDtypeStruct((dg, th, p), BF16),
                   jax.ShapeDtypeStruct((dg, th, p), BF16),
                   jax.ShapeDtypeStruct((dg, 8, p), F32),
                   jax.ShapeDtypeStruct((dg, 8, p), F32)],
        compiler_params=_cparams(("parallel",)),
        name="ssm_weights",
    )(row(a_re), row(a_im), log_dt.reshape(dg, 1, 1),
      bt(b_re), bt(b_im), c_re, c_im, e_col, e_row)
    sw = lambda x: jnp.swapaxes(x, 1, 2)
    return tt, wor, woi, sw(wstr), sw(wsti), sw(scr), sw(sci), scr, sci


def _ssm_body(*refs, steps, chunks, gpb, has_init):
    if has_init:
        (u_ref, d_ref, tt_ref, wor_ref, woi_ref, wsr_ref, wsi_ref, scr_ref, sci_ref,
         s0r_ref, s0i_ref, y_ref, sfr_ref, sfi_ref, ys_ref, xr_ref, xi_ref) = refs
    else:
        (u_ref, d_ref, tt_ref, wor_ref, woi_ref, wsr_ref, wsi_ref, scr_ref, sci_ref,
         y_ref, sfr_ref, sfi_ref, ys_ref, xr_ref, xi_ref, st_r, st_i) = refs
    h = SSM_GROUP
    th = steps * h
    w = u_ref.shape[0] // steps
    p = wsr_ref.shape[1]
    ws_off = wsr_ref.shape[2] - th

    def step_rows(t):
        return pl.ds(t, w, stride=steps) if chunks > 1 else pl.ds(t * w, w)

    slabs = [u_ref[step_rows(t), :].T for t in range(steps)]

    for gl in range(gpb):
        rows = slice(gl * h, (gl + 1) * h)
        u = jnp.concatenate([s[rows, :] for s in slabs], axis=0)
        ub = u.astype(BF16)
        y = _dot(tt_ref[gl, :th, :th], ub) + jnp.concatenate([d_ref[gl]] * steps, axis=0) * u
        xr_ref[gl * p:(gl + 1) * p, :] = _dot(wsr_ref[gl, :, ws_off:], ub)
        xi_ref[gl * p:(gl + 1) * p, :] = _dot(wsi_ref[gl, :, ws_off:], ub)
        for t in range(steps):
            ys_ref[t, rows, :] = y[t * h:(t + 1) * h, :]

    gp = gpb * p
    col = lambda ref, k: ref[:, :, k:k + 1].reshape(gp, 1)
    xr, xi = xr_ref[...], xi_ref[...]
    if chunks > 1:
        pad = st_r.shape[0] - w
        c_row = lax.broadcasted_iota(jnp.int32, (w, 1), 0) & (chunks - 1)
        st_r[:pad, :] = jnp.zeros((pad, gp), F32)
        st_i[:pad, :] = jnp.zeros((pad, gp), F32)
        st_r[pad:, :] = xr.T
        st_i[pad:, :] = xi.T
        k = 0
        while (1 << k) < chunks:
            sh = 1 << k
            rr, ri = st_r[pl.ds(pad - sh, w), :], st_i[pl.ds(pad - sh, w), :]
            mr, mi = scr_ref[k:k + 1, :], sci_ref[k:k + 1, :]
            keep = c_row >= sh
            st_r[pad:, :] += jnp.where(keep, mr * rr - mi * ri, 0.0)
            st_i[pad:, :] += jnp.where(keep, mr * ri + mi * rr, 0.0)
            k += 1
        first = c_row >= 1
        xr_ref[...] = jnp.where(first, st_r[pl.ds(pad - 1, w), :], 0.0).T
        xi_ref[...] = jnp.where(first, st_i[pl.ds(pad - 1, w), :], 0.0).T
        for n in range(w // chunks):
            last = pad + n * chunks + chunks - 1
            sfr_ref[n:n + 1, :] = st_r[last:last + 1, :]
            sfi_ref[n:n + 1, :] = st_i[last:last + 1, :]
    else:
        pr, pi = s0r_ref[...].reshape(gp, w), s0i_ref[...].reshape(gp, w)
        lr, li = col(scr_ref, 7), col(sci_ref, 7)
        sfr_ref[...] = (lr * pr - li * pi + xr).reshape(gpb, p, w)
        sfi_ref[...] = (lr * pi + li * pr + xi).reshape(gpb, p, w)
        xr_ref[...] = pr
        xi_ref[...] = pi

    for gl in range(gpb):
        rows = slice(gl * h, (gl + 1) * h)
        pr = xr_ref[gl * p:(gl + 1) * p, :].astype(BF16)
        pi = xi_ref[gl * p:(gl + 1) * p, :].astype(BF16)
        ya = _dot(wor_ref[gl, :th, :], pr) + _dot(woi_ref[gl, :th, :], pi)
        for t in range(steps):
            y = ys_ref[t, rows, :] + ya[t * h:(t + 1) * h, :]
            ys_ref[t, rows, :] = 0.5 * y * (1.0 + jnp.tanh(GELU_C * (y + 0.044715 * (y * y * y))))
    for t in range(steps):
        y_ref[step_rows(t), :] = ys_ref[t].T


def _ssm(z, row_blk, rows, d_t, ops, layer, n_groups, steps, chunks, s0=None, gpb=8):
    tt, wor, woi, wsr, wsi, scr, sci, sc_rows_r, sc_rows_i = ops
    h = SSM_GROUP
    w = rows // steps
    nseq = w // chunks
    p = scr.shape[1]
    gp = gpb * p
    thf = tt.shape[1]
    nblk = n_groups // gpb
    base = layer * nblk
    wblk = lambda *s: pl.BlockSpec((gpb,) + s, lambda g: (base + g,) + (0,) * len(s))
    scratch = [pltpu.VMEM((steps, gpb * h, w), F32), pltpu.VMEM((gp, w), F32), pltpu.VMEM((gp, w), F32)]
    if s0 is None:
        sc_spec = pl.BlockSpec((None, sc_rows_r.shape[1], gp), lambda g: (base + g, 0, 0))
        sc_args = [sc_rows_r, sc_rows_i]
        sf_spec = pl.BlockSpec((None, nseq, gp), lambda g: (g, 0, 0))
        sf_shape = jax.ShapeDtypeStruct((nblk, nseq, gp), F32)
        scratch += [pltpu.VMEM((chunks + w, gp), F32)] * 2
    else:
        sc_spec = wblk(p, scr.shape[2])
        sc_args = [scr, sci]
        sf_spec = pl.BlockSpec((gpb, p, nseq), lambda g: (g, 0, 0))
        sf_shape = jax.ShapeDtypeStruct((n_groups, p, nseq), F32)
    in_specs = [pl.BlockSpec((rows, gpb * h), lambda g: (row_blk, g)),
                wblk(h, 1), wblk(thf, thf), wblk(thf, p), wblk(thf, p), wblk(p, thf), wblk(p, thf),
                sc_spec, sc_spec]
    args = [z, d_t, tt, wor, woi, wsr, wsi] + sc_args
    if s0 is not None:
        in_specs += [pl.BlockSpec((gpb, p, w), lambda g: (g, 0, 0))] * 2
        args += list(s0)
    return pl.pallas_call(
        functools.partial(_ssm_body, steps=steps, chunks=chunks, gpb=gpb, has_init=s0 is not None),
        grid=(nblk,),
        in_specs=in_specs,
        out_specs=[pl.BlockSpec((rows, gpb * h), lambda g: (0, g)), sf_spec, sf_spec],
        out_shape=[jax.ShapeDtypeStruct((rows, n_groups * h), F32), sf_shape, sf_shape],
        scratch_shapes=scratch,
        compiler_params=_cparams(("parallel",)),
        name="ssm_chunks" if s0 is None else "ssm_step",
    )(*args)


def _pool_seq_body(u_ref, w_ref, sc_ref, o_ref, z_ref):
    l, c = u_ref.shape
    pad = z_ref.shape[0] - l
    cg = c // len(POOL_WINDOWS)
    z_ref[:pad, :] = jnp.zeros((pad, c), F32)
    pos1 = (lax.broadcasted_iota(jnp.int32, (l, 1), 0) + 1).astype(F32)
    for gi, win in enumerate(POOL_WINDOWS):
        cols = slice(gi * cg, (gi + 1) * cg)
        cur = u_ref[:, cols]
        tot = cur
        sh = 1
        while sh < win:
            z_ref[pad:, cols] = tot
            tot = tot + z_ref[pad - sh:pad - sh + l, cols]
            sh *= 2
        inv_cnt = 1.0 / jnp.minimum(pos1, float(win))
        mixed = _dot((tot * inv_cnt - cur).astype(BF16), w_ref[gi].astype(BF16))
        o_ref[:, cols] = (mixed * sc_ref[:, cols]).astype(BF16)


def _pool_seq(z, w_pool, scale, layer, nseq, seqlen, width, col_blk):
    return pl.pallas_call(
        _pool_seq_body,
        grid=(nseq,),
        in_specs=[pl.BlockSpec((seqlen, width), lambda n: (n, col_blk)),
                  pl.BlockSpec((None,) + w_pool.shape[1:], lambda n: (layer, 0, 0, 0)),
                  pl.BlockSpec((None, 1, width), lambda n: (layer, 0, 0))],
        out_specs=pl.BlockSpec((seqlen, width), lambda n: (n, 0)),
        out_shape=jax.ShapeDtypeStruct((nseq * seqlen, width), BF16),
        scratch_shapes=[pltpu.VMEM((seqlen + 16, width), F32)],
        compiler_params=_cparams(("parallel",)),
        name="pool_seq",
    )(z, w_pool, scale)


def _pool_step_body(u_ref, prev_ref, w_ref, sc_ref, *rest):
    o_ref, nxt_ref = rest[-2:]
    n, buf, c = prev_ref.shape
    steps = u_ref.shape[0] // n
    cg = c // len(POOL_WINDOWS)

    def row(j, cols):
        return prev_ref[:, j, cols] if j < buf else u_ref[(j - buf) * n:(j - buf + 1) * n, cols]

    for gi, win in enumerate(POOL_WINDOWS):
        cols = slice(gi * cg, (gi + 1) * cg)
        for t in range(steps):
            cur = row(buf + t, cols)
            tot = cur
            for k in range(1, win):
                tot = tot + row(buf + t - k, cols)
            mixed = _dot((tot * (1.0 / win) - cur).astype(BF16), w_ref[gi].astype(BF16))
            o_ref[t * n:(t + 1) * n, cols] = (mixed * sc_ref[:, cols]).astype(BF16)
    for j in range(buf):
        nxt_ref[:, j, :] = row(j + steps, slice(None))


def _pool_step(z, prev, w_pool, scale, layer, rows, col_blk, carried=None):
    n, buf, c = prev.shape[1:]
    row_blk = z.shape[0] // rows - 1
    in_specs = [pl.BlockSpec((rows, c), lambda i: (row_blk, col_blk)),
                pl.BlockSpec((None, n, buf, c), lambda i: (layer, 0, 0, 0)),
                pl.BlockSpec((None,) + w_pool.shape[1:], lambda i: (layer, 0, 0, 0)),
                pl.BlockSpec((None, 1, c), lambda i: (layer, 0, 0))]
    args = [z, prev, w_pool, scale]
    aliases = {}
    if carried is not None:
        in_specs.append(pl.BlockSpec(memory_space=pl.ANY))
        args.append(carried)
        aliases = {len(args) - 1: 1}
    return pl.pallas_call(
        _pool_step_body,
        grid=(1,),
        in_specs=in_specs,
        out_specs=[pl.BlockSpec((rows, c), lambda i: (0, 0)),
                   pl.BlockSpec((None, n, buf, c), lambda i: (layer, 0, 0, 0))],
        out_shape=[jax.ShapeDtypeStruct((rows, c), BF16), jax.ShapeDtypeStruct(prev.shape, F32)],
        input_output_aliases=aliases,
        compiler_params=_cparams(("arbitrary",)),
        name="pool_step",
    )(*args)


def _mix_body(gap_ref, gas_ref, ybp_ref, ybs_ref, sa_ref, sb_ref, wa_ref, wb_ref, wp_ref, o_ref, *, p_tiles):
    first = pl.program_id(0) < p_tiles
    ga = jnp.where(first, gap_ref[...], gas_ref[...]).astype(BF16)
    yb =jnp.where(first, ybp_ref[...], ybs_ref[...])
    for cs in _col_chunks(o_ref.shape[1], MXU_COLS):
        br_a = _dot(ga, wa_ref[:, cs].astype(BF16)) * jax.nn.sigmoid(_dot(ga, wb_ref[:, cs].astype(BF16)))
        br_b = _dot(yb, wp_ref[:, cs].astype(BF16))
        o_ref[:, cs] = (sa_ref[:, cs] * br_a + sb_ref[:, cs] * br_b).astype(BF16)


def _resident(shape, layer):
    return pl.BlockSpec((None,) + shape, lambda i: (layer,) + (0,) * len(shape), pipeline_mode=pl.Buffered(1))


def _split_specs(tm, width, p_tiles):
    return [pl.BlockSpec((tm, width), lambda i: (jnp.minimum(i, p_tiles - 1), 0)),
            pl.BlockSpec((tm, width), lambda i: (jnp.maximum(i - p_tiles, 0), 0))]


def _mix(ga_p, ga_s, yb_p, yb_s, z, wa, wb, wp, layer, tm, gate_col):
    m = z.shape[0]
    k = ga_p.shape[1]
    n = wa.shape[-1]
    gblk = gate_col // n
    p_tiles = ga_p.shape[0] // tm
    return pl.pallas_call(
        functools.partial(_mix_body, p_tiles=p_tiles),
        grid=(m // tm,),
        in_specs=_split_specs(tm, k, p_tiles) + _split_specs(tm, k, p_tiles) + [
                  pl.BlockSpec((tm, n), lambda i: (i, gblk)),
                  pl.BlockSpec((tm, n), lambda i: (i, gblk + 1)),
                  _resident((k, n), layer), _resident((k, n), layer), _resident((k, n), layer)],
        out_specs=pl.BlockSpec((tm, n), lambda i: (i, 0)),
        out_shape=jax.ShapeDtypeStruct((m, n), BF16),
        compiler_params=_cparams(("parallel",)),
        name="mix",
    )(ga_p, ga_s, yb_p, yb_s, z, z, wa, wb, wp)


def _resmm_body(a_ref, w_ref, h_ref, g_ref, o_ref, xn_ref):
    a = a_ref[...]
    for cs in _col_chunks(o_ref.shape[1], MXU_COLS):
        o_ref[:, cs] = h_ref[:, cs] + _dot(a, w_ref[:, cs].astype(BF16))
    xn_ref[...] = _rms_bf16(o_ref[...], g_ref[...])


def _resmm(a, w, hres, g_next, layer, tm):
    m, k = a.shape
    n = w.shape[-1]
    return pl.pallas_call(
        _resmm_body,
        grid=(m // tm,),
        in_specs=[pl.BlockSpec((tm, k), lambda i: (i, 0)),
                  _resident((k, n), layer),
                  pl.BlockSpec((tm, n), lambda i: (i, 0)),
                  pl.BlockSpec((None, 1, n), lambda i: (layer, 0, 0))],
        out_specs=[pl.BlockSpec((tm, n), lambda i: (i, 0)), pl.BlockSpec((tm, n), lambda i: (i, 0))],
        out_shape=[jax.ShapeDtypeStruct((m, n), F32), jax.ShapeDtypeStruct((m, n), BF16)],
        compiler_params=_cparams(("parallel",)),
        name="resmm",
    )(a, w, hres, g_next)


def _ple_update(x_ref, g_ref, pp_ref, ps_ref, wp_ref, wg_ref, dst_ref, p_tiles):
    xn = _rms_bf16(x_ref[...], g_ref[...])
    pb = jnp.where(pl.program_id(0) < p_tiles, pp_ref[...], ps_ref[...]).astype(BF16)
    for cs in _col_chunks(dst_ref.shape[1], MXU_COLS):
        gate = jax.nn.sigmoid(_dot(xn, wg_ref[:, cs].astype(BF16)))
        dst_ref[:, cs] = x_ref[:, cs] + _dot(pb, wp_ref[:, cs].astype(BF16)) * gate


def _ple_body(x_ref, g_ref, pp_ref, ps_ref, wp_ref, wg_ref, gn_ref, o_ref, xn_ref, *, p_tiles):
    _ple_update(x_ref, g_ref, pp_ref, ps_ref, wp_ref, wg_ref, o_ref, p_tiles)
    xn_ref[...] = _rms_bf16(o_ref[...], gn_ref[...])


def _ple_final_body(x_ref, g_ref, pp_ref, ps_ref, wp_ref, wg_ref, gf_ref, op_ref, os_ref, h_ref, *, p_tiles):
    _ple_update(x_ref, g_ref, pp_ref, ps_ref, wp_ref, wg_ref, h_ref, p_tiles)
    h = h_ref[...]
    y = h * lax.rsqrt(jnp.mean(h * h, axis=-1, keepdims=True) + RMS_EPS) * gf_ref[...]
    i = pl.program_id(0)

    @pl.when(i < p_tiles)
    def _():
        op_ref[...] = y

    @pl.when(i >= p_tiles)
    def _():
        os_ref[...] = y


def _ple(x, g, p_p, p_s, wp, wg, layer, tm, g_next=None, g_final=None):
    m, d = x.shape
    mp, pd = p_p.shape[1:]
    p_tiles = mp // tm
    in_specs = [pl.BlockSpec((tm, d), lambda i: (i, 0)),
                pl.BlockSpec((None, 1, d), lambda i: (layer, 0, 0)),
                pl.BlockSpec((None, tm, pd), lambda i: (layer, jnp.minimum(i, p_tiles - 1), 0)),
                pl.BlockSpec((None, tm, pd), lambda i: (layer, jnp.maximum(i - p_tiles, 0), 0)),
                _resident((pd, d), layer), _resident((d, d), layer)]
    if g_final is None:
        row = pl.BlockSpec((tm, d), lambda i: (i, 0))
        return pl.pallas_call(
            functools.partial(_ple_body, p_tiles=p_tiles), grid=(m // tm,),
            in_specs=in_specs + [pl.BlockSpec((None, 1, d), lambda i: (layer + 1, 0, 0))],
            out_specs=[row, row],
            out_shape=[jax.ShapeDtypeStruct((m, d), F32), jax.ShapeDtypeStruct((m, d), BF16)],
            compiler_params=_cparams(("parallel",)),
            name="ple",
        )(x, g, p_p, p_s, wp, wg, g_next)
    return pl.pallas_call(
        functools.partial(_ple_final_body, p_tiles=p_tiles),
        grid=(m // tm,),
        in_specs=in_specs + [pl.BlockSpec((1, d), lambda i: (0, 0))],
        out_specs=[pl.BlockSpec((tm, d), lambda i: (jnp.minimum(i, p_tiles - 1), 0)),
                   pl.BlockSpec((tm, d), lambda i: (jnp.maximum(i - p_tiles, 0), 0))],
        out_shape=[jax.ShapeDtypeStruct((mp, d), F32), jax.ShapeDtypeStruct((m - mp, d), F32)],
        scratch_shapes=[pltpu.VMEM((tm, d), F32)],
        compiler_params=_cparams(("arbitrary",)),
        name="ple_final",
    )(x, g, p_p, p_s, wp, wg, g_final)


def _prep_body(xp_ref, xs_ref, g_ref, h_ref, xn_ref, *, p_tiles):
    x = jnp.where(pl.program_id(0) < p_tiles, xp_ref[...], xs_ref[...])
    h_ref[...] = x
    xn_ref[...] = _rms_bf16(x, g_ref[...])


def _prep(x_p, x_s, g, tm):
    d = x_p.shape[1]
    m = x_p.shape[0] + x_s.shape[0]
    p_tiles = x_p.shape[0] // tm
    row = pl.BlockSpec((tm, d), lambda i: (i, 0))
    return pl.pallas_call(
        functools.partial(_prep_body, p_tiles=p_tiles),
        grid=(m // tm,),
        in_specs=_split_specs(tm, d, p_tiles) + [pl.BlockSpec((None, 1, d), lambda i: (0, 0, 0))],
        out_specs=[row, row],
        out_shape=[jax.ShapeDtypeStruct((m, d), F32), jax.ShapeDtypeStruct((m, d), BF16)],
        compiler_params=_cparams(("parallel",)),
        name="prep",
    )(x_p, x_s, g)


def _pick_tile(n, pref):
    t = min(pref, n)
    while n % t:
        t //= 2
    return t


def kernel(x_prompt, x_sample, state_ssm_re, state_ssm_im, state_pool, p_prompt, p_sample, g_ffn1, w_ffn1_gate, w_ffn1_up, w_ffn1_down, g_mix, w_in, ssm_a_re, ssm_a_im, ssm_log_dt, ssm_b_re, ssm_b_im, ssm_c_re, ssm_c_im, ssm_d, w_glu_a, w_glu_b, w_pool, pool_scale, w_pool_up, w_out, g_ffn2, w_ffn2_gate, w_ffn2_up, w_ffn2_down, g_ple, w_ple, w_ple_gate, g_final):
    nb, seq, d = x_prompt.shape
    ns, dseq, _ = x_sample.shape
    depth, n_groups, p_state = ssm_a_re.shape
    h = ssm_b_re.shape[-1]
    sw = n_groups * h
    pw = pool_scale.shape[-1]
    buf = state_pool.shape[2]
    chunk = SSM_CHUNK
    assert h == SSM_GROUP and dseq * 2 == chunk and seq % chunk == 0 and buf == max(POOL_WINDOWS) - 1
    n_chunks = seq // chunk
    assert n_chunks & (n_chunks - 1) == 0 and n_chunks <= 128
    mp, ms = nb * seq, ns * dseq
    m = mp + ms
    tm = _pick_tile(m, 1024)

    g3 = lambda a: a.reshape(depth, 1, -1)
    wg1, wu1, wd1 = w_ffn1_gate, w_ffn1_up, w_ffn1_down
    wg2, wu2, wd2 = w_ffn2_gate, w_ffn2_up, w_ffn2_down
    wga, wgb, wpu, wo = w_glu_a, w_glu_b, w_pool_up, w_out
    wpl, wpg, wpool = w_ple, w_ple_gate, w_pool
    tn_in = _pick_tile(sw + pw, 2048)
    gf1, gmx, gf2, gpl = g3(g_ffn1), g3(g_mix), g3(g_ffn2), g3(g_ple)
    pscale = g3(pool_scale)

    flat = lambda a: a.reshape((depth * n_groups,) + a.shape[2:])
    gpb = min(8, n_groups)
    ops = _ssm_weights(flat(ssm_a_re), flat(ssm_a_im), flat(ssm_log_dt), flat(ssm_b_re), flat(ssm_b_im),
                       flat(ssm_c_re), flat(ssm_c_im), chunk)
    blocked = lambda a: (a.reshape(-1, gpb, a.shape[1], p_state).transpose(0, 2, 1, 3)
                         .reshape(-1, a.shape[1], gpb * p_state))
    ops = ops[:-2] + (blocked(ops[-2]), blocked(ops[-1]))
    d_t = flat(ssm_d).reshape(depth * n_groups, h, 1)

    p_p = p_prompt.reshape(depth, mp, -1)
    p_s = p_sample.transpose(0, 2, 1, 3).reshape(depth, ms, -1)
    tf = _pick_tile(w_ffn1_gate.shape[-1], 512)
    tm2 = _pick_tile(math.gcd(mp, ms), 512)
    hcur, xn = _prep(x_prompt.reshape(mp, d), x_sample.transpose(1, 0, 2).reshape(ms, d), gf1, tm2)
    new_re_p, new_im_p, new_pool_p, new_re_s, new_im_s = [], [], [], [], []
    new_pool_s = None
    for i in range(depth):
        h1 = _ffn(xn, hcur, wg1, wu1, wd1, i, tm, tf)
        z = _inproj(h1, gmx, w_in, i, tm, tn_in, sw + pw)

        ga_p, sr_p, si_p = _ssm(z, 0, mp, d_t, ops, i, n_groups, chunk, n_chunks, gpb=gpb)
        s0 = (state_ssm_re[i].transpose(1, 2, 0), state_ssm_im[i].transpose(1, 2, 0))
        ga_s, sr_s, si_s = _ssm(z, mp // ms, ms, d_t, ops, i, n_groups, dseq, 1, s0=s0, gpb=gpb)
        unblock = lambda a: (a.reshape(-1, nb, gpb, p_state).transpose(1, 0, 2, 3)
                             .reshape(nb, n_groups, p_state))
        new_re_p.append(unblock(sr_p))
        new_im_p.append(unblock(si_p))
        new_re_s.append(sr_s.transpose(2, 0, 1))
        new_im_s.append(si_s.transpose(2, 0, 1))

        yb_p = _pool_seq(z, wpool, pscale, i, nb, seq, pw, sw // pw)
        yb_s, new_pool_s = _pool_step(z, state_pool, wpool, pscale, i, ms, sw // pw, carried=new_pool_s)
        new_pool_p.append(jnp.stack([z[(n + 1) * seq - buf:(n + 1) * seq, sw:sw + pw] for n in range(nb)]))

        merged = _mix(ga_p, ga_s, yb_p, yb_s, z, wga, wgb, wpu, i, tm2, sw + pw)
        h2, xn2 = _resmm(merged, wo, h1, gf2, i, tm2)
        h3 = _ffn(xn2, h2, wg2, wu2, wd2, i, tm, tf)
        if i + 1 < depth:
            hcur, xn = _ple(h3, gpl, p_p, p_s, wpl, wpg, i, tm2, g_next=gf1)
        else:
            y_p, y_s = _ple(h3, gpl, p_p, p_s, wpl, wpg, i, tm2, g_final=g_final.reshape(1, d))

    return (y_p.reshape(nb, seq, d), y_s.reshape(dseq, ns, d).transpose(1, 0, 2),
            jnp.stack(new_re_p), jnp.stack(new_im_p), jnp.stack(new_pool_p),
            jnp.stack(new_re_s), jnp.stack(new_im_s), new_pool_s)
```

```python
import functools
import math

import numpy as np
import jax
import jax.numpy as jnp
from jax import lax
from jax.experimental import pallas as pl
from jax.experimental.pallas import tpu as pltpu

F32 = jnp.float32
BF16 = jnp.bfloat16
RMS_EPS = 1e-6
POOL_WINDOWS = (2, 4, 8, 16)
SSM_GROUP = 16
SSM_CHUNK = 16
GELU_C = math.sqrt(2.0 / math.pi)
VMEM_LIMIT = 62 * 1024 * 1024
MXU_COLS = 256


def _cparams(sem):
    return pltpu.CompilerParams(dimension_semantics=sem, vmem_limit_bytes=VMEM_LIMIT)


def _rms_bf16(x, g):
    inv = lax.rsqrt(jnp.mean(x * x, axis=-1, keepdims=True) + RMS_EPS)
    return (x * inv * g).astype(BF16)


def _dot(a, b):
    return jnp.dot(a, b, preferred_element_type=F32)


def _col_chunks(width, chunk):
    chunk = min(chunk, width)
    return [slice(c0, c0 + chunk) for c0 in range(0, width, chunk)]


def _ffn_body(xn_ref, h_ref, wg_ref, wu_ref, wd_ref, *rest, res_chunks):
    gn_ref, o_ref, xo_ref = rest if len(rest) == 3 else (None,) + rest + (None,)
    j = pl.program_id(1)
    nj = pl.num_programs(1)

    @pl.when(j == 0)
    def _():
        o_ref[...] = jnp.zeros_like(o_ref)

    xn = xn_ref[...]
    mids = []
    for cs in _col_chunks(wg_ref.shape[1], MXU_COLS):
        a = _dot(xn, wg_ref[:, cs].astype(BF16))
        b = _dot(xn, wu_ref[:, cs].astype(BF16))
        mids.append((0.5 * a * jax.nn.sigmoid(a) * b).astype(BF16))
    mid = jnp.concatenate(mids, axis=1)
    for cs in _col_chunks(o_ref.shape[1], 2 * MXU_COLS):
        o_ref[:, cs] += _dot(mid, wd_ref[:, cs].astype(BF16))

    cw = o_ref.shape[1] // res_chunks
    for c in range(res_chunks):
        @pl.when(j == nj - res_chunks + c)
        def _(c=c):
            o_ref[:, c * cw:(c + 1) * cw] += h_ref[...]

    if xo_ref is not None:
        @pl.when(j == nj - 1)
        def _():
            xo_ref[...] = _rms_bf16(o_ref[...], gn_ref[...])


def _ffn(xn, hres, wg, wu, wd, layer, tm, tf, g_next=None):
    m, d = xn.shape
    f = wg.shape[-1]
    nj = f // tf
    res_chunks = min(4, nj)
    row = lambda: pl.BlockSpec((tm, d), lambda i, j: (i, 0))
    in_specs = [
        pl.BlockSpec((tm, d), lambda i, j: (i, 0)),
        pl.BlockSpec((tm, d // res_chunks),
                     lambda i, j: (i, jnp.clip(j - (nj - res_chunks), 0, res_chunks - 1))),
        pl.BlockSpec((None, d, tf), lambda i, j: (layer, 0, j)),
        pl.BlockSpec((None, d, tf), lambda i, j: (layer, 0, j)),
        pl.BlockSpec((None, tf, d), lambda i, j: (layer, j, 0)),
    ]
    args = [xn, hres, wg, wu, wd]
    out_specs, out_shape = row(), jax.ShapeDtypeStruct((m, d), F32)
    if g_next is not None:
        in_specs.append(pl.BlockSpec((None, 1, d), lambda i, j: (layer, 0, 0)))
        args.append(g_next)
        out_specs, out_shape = [row(), row()], [out_shape, jax.ShapeDtypeStruct((m, d), BF16)]
    return pl.pallas_call(
        functools.partial(_ffn_body, res_chunks=res_chunks),
        grid=(m // tm, nj),
        in_specs=in_specs,
        out_specs=out_specs,
        out_shape=out_shape,
        compiler_params=_cparams(("parallel", "arbitrary")),
        name="ffn",
    )(*args)


def _inproj_body(x_ref, g_ref, w_ref, o_ref, *, gate_from):
    is_gate = pl.program_id(0) >= gate_from
    xn = _rms_bf16(x_ref[...], g_ref[...])
    for cs in _col_chunks(o_ref.shape[1], MXU_COLS):
        r = _dot(xn, w_ref[:, cs].astype(BF16))
        o_ref[:, cs] = jnp.where(is_gate, jax.nn.sigmoid(r), r)


def _inproj(x, g, w, layer, tm, tn, n_plain):
    m, d = x.shape
    n = w.shape[-1]
    return pl.pallas_call(
        functools.partial(_inproj_body, gate_from=n_plain // tn),
        grid=(n // tn, m // tm),
        in_specs=[
            pl.BlockSpec((tm, d), lambda j, i: (i, 0)),
            pl.BlockSpec((None, 1, d), lambda j, i: (layer, 0, 0)),
            pl.BlockSpec((None, d, tn), lambda j, i: (layer, 0, j), pipeline_mode=pl.Buffered(1)),
        ],
        out_specs=pl.BlockSpec((tm, tn), lambda j, i: (i, j)),
        out_shape=jax.ShapeDtypeStruct((m, n), F32),
        compiler_params=_cparams(("arbitrary", "arbitrary")),
        name="inproj",
    )(x, g, w)


def _cmul(ar, ai, br, bi):
    return ar * br - ai * bi, ar * bi + ai * br


def _dot_nt3(a, b):
    nt = (((1,), (1,)), ((), ()))
    dot = lambda x, y: lax.dot_general(x, y, nt, preferred_element_type=F32)
    ah, bh = a.astype(BF16), b.astype(BF16)
    al = (a - ah.astype(F32)).astype(BF16)
    bl = (b - bh.astype(F32)).astype(BF16)
    return dot(ah, bh) + dot(ah, bl) + dot(al, bh)


def _ssm_weights_body(ar_ref, ai_ref, ldt_ref, btr_ref, bti_ref, ctr_ref, cti_ref, e_col, e_row,
                      tt_o, wor_o, woi_o, wstr_o, wsti_o, scr_o, sci_o, *, chunk, gblk):
    h = SSM_GROUP
    causal = e_col[...] >= e_row[...]

    def rows(pows):
        width = pows[0][0].shape[1]
        return tuple(jnp.concatenate([jnp.broadcast_to(x[k], (h, width)) for x in pows], axis=0)
                     for k in (0, 1))

    for gl in range(gblk):
        dt = jnp.exp(ldt_ref[gl])
        a_re, a_im = ar_ref[gl], ai_ref[gl]
        mag = jnp.exp(a_re * dt)
        ang = a_im * dt
        lr, li = mag * jnp.cos(ang), mag * jnp.sin(ang)
        den = a_re * a_re + a_im * a_im
        num_re = lr - 1.0
        k_re = (num_re * a_re + li * a_im) / den
        k_im = (li * a_re - num_re * a_im) / den
        inv = 1.0 / (lr * lr + li * li)
        nr, ni = lr * inv, -li * inv
        pw = [(jnp.ones_like(lr), jnp.zeros_like(lr))]
        npw = list(pw)
        for _ in range(chunk):
            pw.append(_cmul(*pw[-1], lr, li))
            npw.append(_cmul(*npw[-1], nr, ni))
        tile = lambda x: jnp.concatenate([x] * chunk, axis=0)
        kb = _cmul(k_re, k_im, tile(btr_ref[gl]), tile(bti_ref[gl]))
        c = (tile(ctr_ref[gl]), tile(cti_ref[gl]))

        l_re, l_im = _cmul(*c, *rows(pw[:chunk]))
        r_re, r_im = _cmul(*rows(npw[:chunk]), *kb)
        kmat = _dot_nt3(l_re, r_re) - _dot_nt3(l_im, r_im)
        tt_o[gl] = jnp.where(causal, kmat, 0.0).astype(BF16)

        e_re, e_im = _cmul(*c, *rows(pw[1:chunk + 1]))
        wor_o[gl] = e_re.astype(BF16)
        woi_o[gl] = (-e_im).astype(BF16)

        s_re, s_im = _cmul(*rows(pw[chunk - 1::-1]), *kb)
        wstr_o[gl] = s_re.astype(BF16)
        wsti_o[gl] = s_im.astype(BF16)

        sc = [pw[chunk]]
        for _ in range(6):
            sc.append(_cmul(*sc[-1], *sc[-1]))
        sc.append(pw[chunk // 2])
        scr_o[gl] = jnp.concatenate([x[0] for x in sc], axis=0)
        sci_o[gl] = jnp.concatenate([x[1] for x in sc], axis=0)


def _ssm_weights(a_re, a_im, log_dt, b_re, b_im, c_re, c_im, chunk):
    dg, p = a_re.shape
    h = b_re.shape[-1]
    th = chunk * h
    tau = np.repeat(np.arange(chunk, dtype=np.float32), h)
    e_col = jnp.asarray(tau.reshape(th, 1))
    e_row = jnp.asarray(tau.reshape(1, th))
    gblk = min(16, dg)
    row = lambda x: x.reshape(dg, 1, p)
    bt = lambda x: jnp.swapaxes(x, 1, 2)
    per_g = lambda *s: pl.BlockSpec((gblk,) + s, lambda g: (g,) + (0,) * len(s))
    const = lambda *s: pl.BlockSpec(s, lambda g: (0,) * len(s))
    tt, wor, woi, wstr, wsti, scr, sci = pl.pallas_call(
        functools.partial(_ssm_weights_body, chunk=chunk, gblk=gblk),
        grid=(dg // gblk,),
        in_specs=[per_g(1, p), per_g(1, p), per_g(1, 1),
                  per_g(h, p), per_g(h, p), per_g(h, p), per_g(h, p),
                  const(th, 1), const(1, th)],
        out_specs=[per_g(th, th), per_g(th, p), per_g(th, p), per_g(th, p), per_g(th, p),
                   per_g(8, p), per_g(8, p)],
        out_shape=[jax.ShapeDtypeStruct((dg, th, th), BF16),
                   jax.ShapeDtypeStruct((dg, th, p), BF16),
                   jax.ShapeDtypeStruct((dg, th, p), BF16),
                   jax.ShapeDtypeStruct((dg, th, p), BF16),
                   jax.ShapeDtypeStruct((dg, th, p), BF16),
                   jax.ShapeDtypeStruct((dg, 8, p), F32),
                   jax.ShapeDtypeStruct((dg, 8, p), F32)],
        compiler_params=_cparams(("parallel",)),
        name="ssm_weights",
    )(row(a_re), row(a_im), log_dt.reshape(dg, 1, 1),
      bt(b_re), bt(b_im), c_re, c_im, e_col, e_row)
    sw = lambda x: jnp.swapaxes(x, 1, 2)
    return tt, wor, woi, sw(wstr), sw(wsti), sw(scr), sw(sci), scr, sci


def _ssm_body(*refs, steps, chunks, gpb, has_init):
    if has_init:
        (u_ref, d_ref, tt_ref, wor_ref, woi_ref, wsr_ref, wsi_ref, scr_ref, sci_ref,
         s0r_ref, s0i_ref, y_ref, sfr_ref, sfi_ref, ys_ref, xr_ref, xi_ref) = refs
    else:
        (u_ref, d_ref, tt_ref, wor_ref, woi_ref, wsr_ref, wsi_ref, scr_ref, sci_ref,
         y_ref, sfr_ref, sfi_ref, ys_ref, xr_ref, xi_ref, st_r, st_i) = refs
    h = SSM_GROUP
    th = steps * h
    w = u_ref.shape[0] // steps
    p = wsr_ref.shape[1]
    ws_off = wsr_ref.shape[2] - th

    def step_rows(t):
        return pl.ds(t, w, stride=steps) if chunks > 1 else pl.ds(t * w, w)

    slabs = [u_ref[step_rows(t), :].T for t in range(steps)]

    for gl in range(gpb):
        rows = slice(gl * h, (gl + 1) * h)
        u = jnp.concatenate([s[rows, :] for s in slabs], axis=0)
        ub = u.astype(BF16)
        y = _dot(tt_ref[gl, :th, :th], ub) + jnp.concatenate([d_ref[gl]] * steps, axis=0) * u
        xr_ref[gl * p:(gl + 1) * p, :] = _dot(wsr_ref[gl, :, ws_off:], ub)
        xi_ref[gl * p:(gl + 1) * p, :] = _dot(wsi_ref[gl, :, ws_off:], ub)
        for t in range(steps):
            ys_ref[t, rows, :] = y[t * h:(t + 1) * h, :]

    gp = gpb * p
    col = lambda ref, k: ref[:, :, k:k + 1].reshape(gp, 1)
    xr, xi = xr_ref[...], xi_ref[...]
    if chunks > 1:
        pad = st_r.shape[0] - w
        c_row = lax.broadcasted_iota(jnp.int32, (w, 1), 0) & (chunks - 1)
        st_r[:pad, :] = jnp.zeros((pad, gp), F32)
        st_i[:pad, :] = jnp.zeros((pad, gp), F32)
        st_r[pad:, :] = xr.T
        st_i[pad:, :] = xi.T
        k = 0
        while (1 << k) < chunks:
            sh = 1 << k
            rr, ri = st_r[pl.ds(pad - sh, w), :], st_i[pl.ds(pad - sh, w), :]
            mr, mi = scr_ref[k:k + 1, :], sci_ref[k:k + 1, :]
            keep = c_row >= sh
            st_r[pad:, :] += jnp.where(keep, mr * rr - mi * ri, 0.0)
            st_i[pad:, :] += jnp.where(keep, mr * ri + mi * rr, 0.0)
            k += 1
        first = c_row >= 1
        xr_ref[...] = jnp.where(first, st_r[pl.ds(pad - 1, w), :], 0.0).T
        xi_ref[...] = jnp.where(first, st_i[pl.ds(pad - 1, w), :], 0.0).T
        for n in range(w // chunks):
            last = pad + n * chunks + chunks - 1
            sfr_ref[n:n + 1, :] = st_r[last:last + 1, :]
            sfi_ref[n:n + 1, :] = st_i[last:last + 1, :]
    else:
        pr, pi = s0r_ref[...].reshape(gp, w), s0i_ref[...].reshape(gp, w)
        lr, li = col(scr_ref, 7), col(sci_ref, 7)
        sfr_ref[...] = (lr * pr - li * pi + xr).reshape(gpb, p, w)
        sfi_ref[...] = (lr * pi + li * pr + xi).reshape(gpb, p, w)
        xr_ref[...] = pr
        xi_ref[...] = pi

    for gl in range(gpb):
        rows = slice(gl * h, (gl + 1) * h)
        pr = xr_ref[gl * p:(gl + 1) * p, :].astype(BF16)
        pi = xi_ref[gl * p:(gl + 1) * p, :].astype(BF16)
        ya = _dot(wor_ref[gl, :th, :], pr) + _dot(woi_ref[gl, :th, :], pi)
        for t in range(steps):
            y = ys_ref[t, rows, :] + ya[t * h:(t + 1) * h, :]
            ys_ref[t, rows, :] = 0.5 * y * (1.0 + jnp.tanh(GELU_C * (y + 0.044715 * (y * y * y))))
    for t in range(steps):
        y_ref[step_rows(t), :] = ys_ref[t].T


def _ssm(z, row_blk, rows, d_t, ops, layer, n_groups, steps, chunks, s0=None, gpb=8):
    tt, wor, woi, wsr, wsi, scr, sci, sc_rows_r, sc_rows_i = ops
    h = SSM_GROUP
    w = rows // steps
    nseq = w // chunks
    p = scr.shape[1]
    gp = gpb * p
    thf = tt.shape[1]
    nblk = n_groups // gpb
    base = layer * nblk
    wblk = lambda *s: pl.BlockSpec((gpb,) + s, lambda g: (base + g,) + (0,) * len(s))
    scratch = [pltpu.VMEM((steps, gpb * h, w), F32), pltpu.VMEM((gp, w), F32), pltpu.VMEM((gp, w), F32)]
    if s0 is None:
        sc_spec = pl.BlockSpec((None, sc_rows_r.shape[1], gp), lambda g: (base + g, 0, 0))
        sc_args = [sc_rows_r, sc_rows_i]
        sf_spec = pl.BlockSpec((None, nseq, gp), lambda g: (g, 0, 0))
        sf_shape = jax.ShapeDtypeStruct((nblk, nseq, gp), F32)
        scratch += [pltpu.VMEM((chunks + w, gp), F32)] * 2
    else:
        sc_spec = wblk(p, scr.shape[2])
        sc_args = [scr, sci]
        sf_spec = pl.BlockSpec((gpb, p, nseq), lambda g: (g, 0, 0))
        sf_shape = jax.ShapeDtypeStruct((n_groups, p, nseq), F32)
    in_specs = [pl.BlockSpec((rows, gpb * h), lambda g: (row_blk, g)),
                wblk(h, 1), wblk(thf, thf), wblk(thf, p), wblk(thf, p), wblk(p, thf), wblk(p, thf),
                sc_spec, sc_spec]
    args = [z, d_t, tt, wor, woi, wsr, wsi] + sc_args
    if s0 is not None:
        in_specs += [pl.BlockSpec((gpb, p, w), lambda g: (g, 0, 0))] * 2
        args += list(s0)
    return pl.pallas_call(
        functools.partial(_ssm_body, steps=steps, chunks=chunks, gpb=gpb, has_init=s0 is not None),
        grid=(nblk,),
        in_specs=in_specs,
        out_specs=[pl.BlockSpec((rows, gpb * h), lambda g: (0, g)), sf_spec, sf_spec],
        out_shape=[jax.ShapeDtypeStruct((rows, n_groups * h), F32), sf_shape, sf_shape],
        scratch_shapes=scratch,
        compiler_params=_cparams(("parallel",)),
        name="ssm_chunks" if s0 is None else "ssm_step",
    )(*args)


def _pool_seq_body(u_ref, w_ref, sc_ref, o_ref, z_ref):
    l, c = u_ref.shape
    pad = z_ref.shape[0] - l
    cg = c // len(POOL_WINDOWS)
    z_ref[:pad, :] = jnp.zeros((pad, c), F32)
    pos1 = (lax.broadcasted_iota(jnp.int32, (l, 1), 0) + 1).astype(F32)
    for gi, win in enumerate(POOL_WINDOWS):
        cols = slice(gi * cg, (gi + 1) * cg)
        cur = u_ref[:, cols]
        tot = cur
        sh = 1
        while sh < win:
            z_ref[pad:, cols] = tot
            tot = tot + z_ref[pad - sh:pad - sh + l, cols]
            sh *= 2
        inv_cnt = 1.0 / jnp.minimum(pos1, float(win))
        mixed = _dot((tot * inv_cnt - cur).astype(BF16), w_ref[gi].astype(BF16))
        o_ref[:, cols] = (mixed * sc_ref[:, cols]).astype(BF16)


def _pool_seq(z, w_pool, scale, layer, nseq, seqlen, width, col_blk):
    return pl.pallas_call(
        _pool_seq_body,
        grid=(nseq,),
        in_specs=[pl.BlockSpec((seqlen, width), lambda n: (n, col_blk)),
                  pl.BlockSpec((None,) + w_pool.shape[1:], lambda n: (layer, 0, 0, 0)),
                  pl.BlockSpec((None, 1, width), lambda n: (layer, 0, 0))],
        out_specs=pl.BlockSpec((seqlen, width), lambda n: (n, 0)),
        out_shape=jax.ShapeDtypeStruct((nseq * seqlen, width), BF16),
        scratch_shapes=[pltpu.VMEM((seqlen + 16, width), F32)],
        compiler_params=_cparams(("parallel",)),
        name="pool_seq",
    )(z, w_pool, scale)


def _pool_step_body(u_ref, prev_ref, w_ref, sc_ref, *rest):
    o_ref, nxt_ref = rest[-2:]
    n, buf, c = prev_ref.shape
    steps = u_ref.shape[0] // n
    cg = c // len(POOL_WINDOWS)

    def row(j, cols):
        return prev_ref[:, j, cols] if j < buf else u_ref[(j - buf) * n:(j - buf + 1) * n, cols]

    for gi, win in enumerate(POOL_WINDOWS):
        cols = slice(gi * cg, (gi + 1) * cg)
        for t in range(steps):
            cur = row(buf + t, cols)
            tot = cur
            for k in range(1, win):
                tot = tot + row(buf + t - k, cols)
            mixed = _dot((tot * (1.0 / win) - cur).astype(BF16), w_ref[gi].astype(BF16))
            o_ref[t * n:(t + 1) * n, cols] = (mixed * sc_ref[:, cols]).astype(BF16)
    for j in range(buf):
        nxt_ref[:, j, :] = row(j + steps, slice(None))


def _pool_step(z, prev, w_pool, scale, layer, rows, col_blk, carried=None):
    n, buf, c = prev.shape[1:]
    row_blk = z.shape[0] // rows - 1
    in_specs = [pl.BlockSpec((rows, c), lambda i: (row_blk, col_blk)),
                pl.BlockSpec((None, n, buf, c), lambda i: (layer, 0, 0, 0)),
                pl.BlockSpec((None,) + w_pool.shape[1:], lambda i: (layer, 0, 0, 0)),
                pl.BlockSpec((None, 1, c), lambda i: (layer, 0, 0))]
    args = [z, prev, w_pool, scale]
    aliases = {}
    if carried is not None:
        in_specs.append(pl.BlockSpec(memory_space=pl.ANY))
        args.append(carried)
        aliases = {len(args) - 1: 1}
    return pl.pallas_call(
        _pool_step_body,
        grid=(1,),
        in_specs=in_specs,
        out_specs=[pl.BlockSpec((rows, c), lambda i: (0, 0)),
                   pl.BlockSpec((None, n, buf, c), lambda i: (layer, 0, 0, 0))],
        out_shape=[jax.ShapeDtypeStruct((rows, c), BF16), jax.ShapeDtypeStruct(prev.shape, F32)],
        input_output_aliases=aliases,
        compiler_params=_cparams(("arbitrary",)),
        name="pool_step",
    )(*args)


def _mix_body(gap_ref, gas_ref, ybp_ref, ybs_ref, sa_ref, sb_ref, wa_ref, wb_ref, wp_ref, o_ref, *, p_tiles):
    first = pl.program_id(0) < p_tiles
    ga = jnp.where(first, gap_ref[...], gas_ref[...]).astype(BF16)
    yb =jnp.where(first, ybp_ref[...], ybs_ref[...])
    for cs in _col_chunks(o_ref.shape[1], MXU_COLS):
        br_a = _dot(ga, wa_ref[:, cs].astype(BF16)) * jax.nn.sigmoid(_dot(ga, wb_ref[:, cs].astype(BF16)))
        br_b = _dot(yb, wp_ref[:, cs].astype(BF16))
        o_ref[:, cs] = (sa_ref[:, cs] * br_a + sb_ref[:, cs] * br_b).astype(BF16)


def _resident(shape, layer):
    return pl.BlockSpec((None,) + shape, lambda i: (layer,) + (0,) * len(shape), pipeline_mode=pl.Buffered(1))


def _split_specs(tm, width, p_tiles):
    return [pl.BlockSpec((tm, width), lambda i: (jnp.minimum(i, p_tiles - 1), 0)),
            pl.BlockSpec((tm, width), lambda i: (jnp.maximum(i - p_tiles, 0), 0))]


def _mix(ga_p, ga_s, yb_p, yb_s, z, wa, wb, wp, layer, tm, gate_col):
    m = z.shape[0]
    k = ga_p.shape[1]
    n = wa.shape[-1]
    gblk = gate_col // n
    p_tiles = ga_p.shape[0] // tm
    return pl.pallas_call(
        functools.partial(_mix_body, p_tiles=p_tiles),
        grid=(m // tm,),
        in_specs=_split_specs(tm, k, p_tiles) + _split_specs(tm, k, p_tiles) + [
                  pl.BlockSpec((tm, n), lambda i: (i, gblk)),
                  pl.BlockSpec((tm, n), lambda i: (i, gblk + 1)),
                  _resident((k, n), layer), _resident((k, n), layer), _resident((k, n), layer)],
        out_specs=pl.BlockSpec((tm, n), lambda i: (i, 0)),
        out_shape=jax.ShapeDtypeStruct((m, n), BF16),
        compiler_params=_cparams(("parallel",)),
        name="mix",
    )(ga_p, ga_s, yb_p, yb_s, z, z, wa, wb, wp)


def _resmm_body(a_ref, w_ref, h_ref, g_ref, o_ref, xn_ref):
    a = a_ref[...]
    for cs in _col_chunks(o_ref.shape[1], MXU_COLS):
        o_ref[:, cs] = h_ref[:, cs] + _dot(a, w_ref[:, cs].astype(BF16))
    xn_ref[...] = _rms_bf16(o_ref[...], g_ref[...])


def _resmm(a, w, hres, g_next, layer, tm):
    m, k = a.shape
    n = w.shape[-1]
    return pl.pallas_call(
        _resmm_body,
        grid=(m // tm,),
        in_specs=[pl.BlockSpec((tm, k), lambda i: (i, 0)),
                  _resident((k, n), layer),
                  pl.BlockSpec((tm, n), lambda i: (i, 0)),
                  pl.BlockSpec((None, 1, n), lambda i: (layer, 0, 0))],
        out_specs=[pl.BlockSpec((tm, n), lambda i: (i, 0)), pl.BlockSpec((tm, n), lambda i: (i, 0))],
        out_shape=[jax.ShapeDtypeStruct((m, n), F32), jax.ShapeDtypeStruct((m, n), BF16)],
        compiler_params=_cparams(("parallel",)),
        name="resmm",
    )(a, w, hres, g_next)


def _ple_update(x_ref, g_ref, pp_ref, ps_ref, wp_ref, wg_ref, dst_ref, p_tiles):
    xn = _rms_bf16(x_ref[...], g_ref[...])
    pb = jnp.where(pl.program_id(0) < p_tiles, pp_ref[...], ps_ref[...]).astype(BF16)
    for cs in _col_chunks(dst_ref.shape[1], MXU_COLS):
        gate = jax.nn.sigmoid(_dot(xn, wg_ref[:, cs].astype(BF16)))
        dst_ref[:, cs] = x_ref[:, cs] + _dot(pb, wp_ref[:, cs].astype(BF16)) * gate


def _ple_body(x_ref, g_ref, pp_ref, ps_ref, wp_ref, wg_ref, gn_ref, o_ref, xn_ref, *, p_tiles):
    _ple_update(x_ref, g_ref, pp_ref, ps_ref, wp_ref, wg_ref, o_ref, p_tiles)
    xn_ref[...] = _rms_bf16(o_ref[...], gn_ref[...])


def _ple_final_body(x_ref, g_ref, pp_ref, ps_ref, wp_ref, wg_ref, gf_ref, op_ref, os_ref, h_ref, *, p_tiles):
    _ple_update(x_ref, g_ref, pp_ref, ps_ref, wp_ref, wg_ref, h_ref, p_tiles)
    h = h_ref[...]
    y = h * lax.rsqrt(jnp.mean(h * h, axis=-1, keepdims=True) + RMS_EPS) * gf_ref[...]
    i = pl.program_id(0)

    @pl.when(i < p_tiles)
    def _():
        op_ref[...] = y

    @pl.when(i >= p_tiles)
    def _():
        os_ref[...] = y


def _ple(x, g, p_p, p_s, wp, wg, layer, tm, g_next=None, g_final=None):
    m, d = x.shape
    mp, pd = p_p.shape[1:]
    p_tiles = mp // tm
    in_specs = [pl.BlockSpec((tm, d), lambda i: (i, 0)),
                pl.BlockSpec((None, 1, d), lambda i: (layer, 0, 0)),
                pl.BlockSpec((None, tm, pd), lambda i: (layer, jnp.minimum(i, p_tiles - 1), 0)),
                pl.BlockSpec((None, tm, pd), lambda i: (layer, jnp.maximum(i - p_tiles, 0), 0)),
                _resident((pd, d), layer), _resident((d, d), layer)]
    if g_final is None:
        row = pl.BlockSpec((tm, d), lambda i: (i, 0))
        return pl.pallas_call(
            functools.partial(_ple_body, p_tiles=p_tiles), grid=(m // tm,),
            in_specs=in_specs + [pl.BlockSpec((None, 1, d), lambda i: (layer + 1, 0, 0))],
            out_specs=[row, row],
            out_shape=[jax.ShapeDtypeStruct((m, d), F32), jax.ShapeDtypeStruct((m, d), BF16)],
            compiler_params=_cparams(("parallel",)),
            name="ple",
        )(x, g, p_p, p_s, wp, wg, g_next)
    return pl.pallas_call(
        functools.partial(_ple_final_body, p_tiles=p_tiles),
        grid=(m // tm,),
        in_specs=in_specs + [pl.BlockSpec((1, d), lambda i: (0, 0))],
        out_specs=[pl.BlockSpec((tm, d), lambda i: (jnp.minimum(i, p_tiles - 1), 0)),
                   pl.BlockSpec((tm, d), lambda i: (jnp.maximum(i - p_tiles, 0), 0))],
        out_shape=[jax.ShapeDtypeStruct((mp, d), F32), jax.ShapeDtypeStruct((m - mp, d), F32)],
        scratch_shapes=[pltpu.VMEM((tm, d), F32)],
        compiler_params=_cparams(("arbitrary",)),
        name="ple_final",
    )(x, g, p_p, p_s, wp, wg, g_final)


def _prep_body(xp_ref, xs_ref, g_ref, h_ref, xn_ref, *, p_tiles):
    x = jnp.where(pl.program_id(0) < p_tiles, xp_ref[...], xs_ref[...])
    h_ref[...] = x
    xn_ref[...] = _rms_bf16(x, g_ref[...])


def _prep(x_p, x_s, g, tm):
    d = x_p.shape[1]
    m = x_p.shape[0] + x_s.shape[0]
    p_tiles = x_p.shape[0] // tm
    row = pl.BlockSpec((tm, d), lambda i: (i, 0))
    return pl.pallas_call(
        functools.partial(_prep_body, p_tiles=p_tiles),
        grid=(m // tm,),
        in_specs=_split_specs(tm, d, p_tiles) + [pl.BlockSpec((None, 1, d), lambda i: (0, 0, 0))],
        out_specs=[row, row],
        out_shape=[jax.ShapeDtypeStruct((m, d), F32), jax.ShapeDtypeStruct((m, d), BF16)],
        compiler_params=_cparams(("parallel",)),
        name="prep",
    )(x_p, x_s, g)


def _pick_tile(n, pref):
    t = min(pref, n)
    while n % t:
        t //= 2
    return t


def kernel(x_prompt, x_sample, state_ssm_re, state_ssm_im, state_pool, p_prompt, p_sample, g_ffn1, w_ffn1_gate, w_ffn1_up, w_ffn1_down, g_mix, w_in, ssm_a_re, ssm_a_im, ssm_log_dt, ssm_b_re, ssm_b_im, ssm_c_re, ssm_c_im, ssm_d, w_glu_a, w_glu_b, w_pool, pool_scale, w_pool_up, w_out, g_ffn2, w_ffn2_gate, w_ffn2_up, w_ffn2_down, g_ple, w_ple, w_ple_gate, g_final):
    nb, seq, d = x_prompt.shape
    ns, dseq, _ = x_sample.shape
    depth, n_groups, p_state = ssm_a_re.shape
    h = ssm_b_re.shape[-1]
    sw = n_groups * h
    pw = pool_scale.shape[-1]
    buf = state_pool.shape[2]
    chunk = SSM_CHUNK
    assert h == SSM_GROUP and dseq * 2 == chunk and seq % chunk == 0 and buf == max(POOL_WINDOWS) - 1
    n_chunks = seq // chunk
    assert n_chunks & (n_chunks - 1) == 0 and n_chunks <= 128
    assert all(w & (w - 1) == 0 for w in POOL_WINDOWS) and sw % pw == 0
    mp, ms = nb * seq, ns * dseq
    m = mp + ms
    tm = _pick_tile(m, 1024)

    g3 = lambda a: a.reshape(depth, 1, -1)
    wg1, wu1, wd1 = w_ffn1_gate, w_ffn1_up, w_ffn1_down
    wg2, wu2, wd2 = w_ffn2_gate, w_ffn2_up, w_ffn2_down
    wga, wgb, wpu, wo = w_glu_a, w_glu_b, w_pool_up, w_out
    wpl, wpg, wpool = w_ple, w_ple_gate, w_pool
    tn_in = _pick_tile(sw + pw, 2048)
    gf1, gmx, gf2, gpl = g3(g_ffn1), g3(g_mix), g3(g_ffn2), g3(g_ple)
    pscale = g3(pool_scale)

    flat = lambda a: a.reshape((depth * n_groups,) + a.shape[2:])
    gpb = min(8, n_groups)
    ops = _ssm_weights(flat(ssm_a_re), flat(ssm_a_im), flat(ssm_log_dt), flat(ssm_b_re), flat(ssm_b_im),
                       flat(ssm_c_re), flat(ssm_c_im), chunk)
    blocked = lambda a: (a.reshape(-1, gpb, a.shape[1], p_state).transpose(0, 2, 1, 3)
                         .reshape(-1, a.shape[1], gpb * p_state))
    ops = ops[:-2] + (blocked(ops[-2]), blocked(ops[-1]))
    d_t = flat(ssm_d).reshape(depth * n_groups, h, 1)

    p_p = p_prompt.reshape(depth, mp, -1)
    p_s = p_sample.transpose(0, 2, 1, 3).reshape(depth, ms, -1)
    tf = _pick_tile(w_ffn1_gate.shape[-1], 512)
    tm2 = _pick_tile(math.gcd(mp, ms), 512)
    hcur, xn = _prep(x_prompt.reshape(mp, d), x_sample.transpose(1, 0, 2).reshape(ms, d), gf1, tm2)
    new_re_p, new_im_p, new_pool_p, new_re_s, new_im_s = [], [], [], [], []
    new_pool_s = None
    for i in range(depth):
        h1 = _ffn(xn, hcur, wg1, wu1, wd1, i, tm, tf)
        z = _inproj(h1, gmx, w_in, i, tm, tn_in, sw + pw)

        ga_p, sr_p, si_p = _ssm(z, 0, mp, d_t, ops, i, n_groups, chunk, n_chunks, gpb=gpb)
        s0 = (state_ssm_re[i].transpose(1, 2, 0), state_ssm_im[i].transpose(1, 2, 0))
        ga_s, sr_s, si_s = _ssm(z, mp // ms, ms, d_t, ops, i, n_groups, dseq, 1, s0=s0, gpb=gpb)
        unblock = lambda a: (a.reshape(-1, nb, gpb, p_state).transpose(1, 0, 2, 3)
                             .reshape(nb, n_groups, p_state))
        new_re_p.append(unblock(sr_p))
        new_im_p.append(unblock(si_p))
        new_re_s.append(sr_s.transpose(2, 0, 1))
        new_im_s.append(si_s.transpose(2, 0, 1))

        yb_p = _pool_seq(z, wpool, pscale, i, nb, seq, pw, sw // pw)
        yb_s, new_pool_s = _pool_step(z, state_pool, wpool, pscale, i, ms, sw // pw, carried=new_pool_s)
        new_pool_p.append(jnp.stack([z[(n + 1) * seq - buf:(n + 1) * seq, sw:sw + pw] for n in range(nb)]))

        merged = _mix(ga_p, ga_s, yb_p, yb_s, z, wga, wgb, wpu, i, tm2, sw + pw)
        h2, xn2 = _resmm(merged, wo, h1, gf2, i, tm2)
        h3 = _ffn(xn2, h2, wg2, wu2, wd2, i, tm, tf)
        if i + 1 < depth:
            hcur, xn = _ple(h3, gpl, p_p, p_s, wpl, wpg, i, tm2, g_next=gf1)
        else:
            y_p, y_s = _ple(h3, gpl, p_p, p_s, wpl, wpg, i, tm2, g_final=g_final.reshape(1, d))

    return (y_p.reshape(nb, seq, d), y_s.reshape(dseq, ns, d).transpose(1, 0, 2),
            jnp.stack(new_re_p), jnp.stack(new_im_p), jnp.stack(new_pool_p),
            jnp.stack(new_re_s), jnp.stack(new_im_s), new_pool_s)
```

```python
import functools
import math

import numpy as np
import jax
import jax.numpy as jnp
from jax import lax
from jax.experimental import pallas as pl
from jax.experimental.pallas import tpu as pltpu

F32 = jnp.float32
BF16 = jnp.bfloat16
RMS_EPS = 1e-6
POOL_WINDOWS = (2, 4, 8, 16)
SSM_GROUP = 16
SSM_CHUNK = 16
GELU_C = math.sqrt(2.0 / math.pi)
VMEM_LIMIT = 62 * 1024 * 1024
MXU_COLS = 256


def _cparams(sem):
    return pltpu.CompilerParams(dimension_semantics=sem, vmem_limit_bytes=VMEM_LIMIT)


def _rms_bf16(x, g):
    inv = lax.rsqrt(jnp.mean(x * x, axis=-1, keepdims=True) + RMS_EPS)
    return (x * inv * g).astype(BF16)


def _dot(a, b):
    return jnp.dot(a, b, preferred_element_type=F32)


def _col_chunks(width, chunk):
    chunk = min(chunk, width)
    return [slice(c0, c0 + chunk) for c0 in range(0, width, chunk)]


def _ffn_body(xn_ref, h_ref, wg_ref, wu_ref, wd_ref, *rest, res_chunks):
    gn_ref, o_ref, xo_ref = rest if len(rest) == 3 else (None,) + rest + (None,)
    j = pl.program_id(1)
    nj = pl.num_programs(1)

    @pl.when(j == 0)
    def _():
        o_ref[...] = jnp.zeros_like(o_ref)

    xn = xn_ref[...]
    mids = []
    for cs in _col_chunks(wg_ref.shape[1], MXU_COLS):
        a = _dot(xn, wg_ref[:, cs].astype(BF16))
        b = _dot(xn, wu_ref[:, cs].astype(BF16))
        mids.append((0.5 * a * jax.nn.sigmoid(a) * b).astype(BF16))
    mid = jnp.concatenate(mids, axis=1)
    chunks = _col_chunks(o_ref.shape[1], 2 * MXU_COLS)
    last = chunks.pop()
    chunks += [slice(c0, c0 + MXU_COLS) for c0 in range(last.start, last.stop, MXU_COLS)]
    for cs in chunks:
        o_ref[:, cs] += _dot(mid, wd_ref[:, cs].astype(BF16))

    cw = o_ref.shape[1] // res_chunks
    for c in range(res_chunks):
        @pl.when(j == nj - res_chunks + c)
        def _(c=c):
            o_ref[:, c * cw:(c + 1) * cw] += h_ref[...]

    if xo_ref is not None:
        @pl.when(j == nj - 1)
        def _():
            xo_ref[...] = _rms_bf16(o_ref[...], gn_ref[...])


def _ffn(xn, hres, wg, wu, wd, layer, tm, tf, g_next=None):
    m, d = xn.shape
    f = wg.shape[-1]
    nj = f // tf
    res_chunks = min(4, nj)
    row = lambda: pl.BlockSpec((tm, d), lambda i, j: (i, 0))
    in_specs = [
        pl.BlockSpec((tm, d), lambda i, j: (i, 0)),
        pl.BlockSpec((tm, d // res_chunks),
                     lambda i, j: (i, jnp.clip(j - (nj - res_chunks), 0, res_chunks - 1))),
        pl.BlockSpec((None, d, tf), lambda i, j: (layer, 0, j)),
        pl.BlockSpec((None, d, tf), lambda i, j: (layer, 0, j)),
        pl.BlockSpec((None, tf, d), lambda i, j: (layer, j, 0)),
    ]
    args = [xn, hres, wg, wu, wd]
    out_specs, out_shape = row(), jax.ShapeDtypeStruct((m, d), F32)
    if g_next is not None:
        in_specs.append(pl.BlockSpec((None, 1, d), lambda i, j: (layer, 0, 0)))
        args.append(g_next)
        out_specs, out_shape = [row(), row()], [out_shape, jax.ShapeDtypeStruct((m, d), BF16)]
    return pl.pallas_call(
        functools.partial(_ffn_body, res_chunks=res_chunks),
        grid=(m // tm, nj),
        in_specs=in_specs,
        out_specs=out_specs,
        out_shape=out_shape,
        compiler_params=_cparams(("parallel", "arbitrary")),
        name="ffn",
    )(*args)


def _inproj_body(x_ref, g_ref, w_ref, o_ref, *, gate_from):
    is_gate = pl.program_id(0) >= gate_from
    xn = _rms_bf16(x_ref[...], g_ref[...])
    for cs in _col_chunks(o_ref.shape[1], MXU_COLS):
        r = _dot(xn, w_ref[:, cs].astype(BF16))
        o_ref[:, cs] = jnp.where(is_gate, jax.nn.sigmoid(r), r)


def _inproj(x, g, w, layer, tm, tn, n_plain):
    m, d = x.shape
    n = w.shape[-1]
    return pl.pallas_call(
        functools.partial(_inproj_body, gate_from=n_plain // tn),
        grid=(n // tn, m // tm),
        in_specs=[
            pl.BlockSpec((tm, d), lambda j, i: (i, 0)),
            pl.BlockSpec((None, 1, d), lambda j, i: (layer, 0, 0)),
            pl.BlockSpec((None, d, tn), lambda j, i: (layer, 0, j), pipeline_mode=pl.Buffered(1)),
        ],
        out_specs=pl.BlockSpec((tm, tn), lambda j, i: (i, j)),
        out_shape=jax.ShapeDtypeStruct((m, n), F32),
        compiler_params=_cparams(("arbitrary", "arbitrary")),
        name="inproj",
    )(x, g, w)


def _cmul(ar, ai, br, bi):
    return ar * br - ai * bi, ar * bi + ai * br


def _dot_nt3(a, b):
    nt = (((1,), (1,)), ((), ()))
    dot = lambda x, y: lax.dot_general(x, y, nt, preferred_element_type=F32)
    ah, bh = a.astype(BF16), b.astype(BF16)
    al = (a - ah.astype(F32)).astype(BF16)
    bl = (b - bh.astype(F32)).astype(BF16)
    return dot(ah, bh) + dot(ah, bl) + dot(al, bh)


def _ssm_weights_body(ar_ref, ai_ref, ldt_ref, btr_ref, bti_ref, ctr_ref, cti_ref, e_col, e_row,
                      tt_o, wor_o, woi_o, wstr_o, wsti_o, scr_o, sci_o, *, chunk, gblk):
    h = SSM_GROUP
    causal = e_col[...] >= e_row[...]

    def rows(pows):
        width = pows[0][0].shape[1]
        return tuple(jnp.concatenate([jnp.broadcast_to(x[k], (h, width)) for x in pows], axis=0)
                     for k in (0, 1))

    for gl in range(gblk):
        dt = jnp.exp(ldt_ref[gl])
        a_re, a_im = ar_ref[gl], ai_ref[gl]
        mag = jnp.exp(a_re * dt)
        ang = a_im * dt
        lr, li = mag * jnp.cos(ang), mag * jnp.sin(ang)
        den = a_re * a_re + a_im * a_im
        num_re = lr - 1.0
        k_re = (num_re * a_re + li * a_im) / den
        k_im = (li * a_re - num_re * a_im) / den
        inv = 1.0 / (lr * lr + li * li)
        nr, ni = lr * inv, -li * inv
        pw = [(jnp.ones_like(lr), jnp.zeros_like(lr))]
        npw = list(pw)
        for _ in range(chunk):
            pw.append(_cmul(*pw[-1], lr, li))
            npw.append(_cmul(*npw[-1], nr, ni))
        tile = lambda x: jnp.concatenate([x] * chunk, axis=0)
        kb = _cmul(k_re, k_im, tile(btr_ref[gl]), tile(bti_ref[gl]))
        c = (tile(ctr_ref[gl]), tile(cti_ref[gl]))

        l_re, l_im = _cmul(*c, *rows(pw[:chunk]))
        r_re, r_im = _cmul(*rows(npw[:chunk]), *kb)
        kmat = _dot_nt3(l_re, r_re) - _dot_nt3(l_im, r_im)
        tt_o[gl] = jnp.where(causal, kmat, 0.0).astype(BF16)

        e_re, e_im = _cmul(*c, *rows(pw[1:chunk + 1]))
        wor_o[gl] = e_re.astype(BF16)
        woi_o[gl] = (-e_im).astype(BF16)

        s_re, s_im = _cmul(*rows(pw[chunk - 1::-1]), *kb)
        wstr_o[gl] = s_re.astype(BF16)
        wsti_o[gl] = s_im.astype(BF16)

        sc = [pw[chunk]]
        for _ in range(6):
            sc.append(_cmul(*sc[-1], *sc[-1]))
        sc.append(pw[chunk // 2])
        scr_o[gl] = jnp.concatenate([x[0] for x in sc], axis=0)
        sci_o[gl] = jnp.concatenate([x[1] for x in sc], axis=0)


def _ssm_weights(a_re, a_im, log_dt, b_re, b_im, c_re, c_im, chunk):
    dg, p = a_re.shape
    h = b_re.shape[-1]
    th = chunk * h
    tau = np.repeat(np.arange(chunk, dtype=np.float32), h)
    e_col = jnp.asarray(tau.reshape(th, 1))
    e_row = jnp.asarray(tau.reshape(1, th))
    gblk = min(16, dg)
    row = lambda x: x.reshape(dg, 1, p)
    bt = lambda x: jnp.swapaxes(x, 1, 2)
    per_g = lambda *s: pl.BlockSpec((gblk,) + s, lambda g: (g,) + (0,) * len(s))
    const = lambda *s: pl.BlockSpec(s, lambda g: (0,) * len(s))
    tt, wor, woi, wstr, wsti, scr, sci = pl.pallas_call(
        functools.partial(_ssm_weights_body, chunk=chunk, gblk=gblk),
        grid=(dg // gblk,),
        in_specs=[per_g(1, p), per_g(1, p), per_g(1, 1),
                  per_g(h, p), per_g(h, p), per_g(h, p), per_g(h, p),
                  const(th, 1), const(1, th)],
        out_specs=[per_g(th, th), per_g(th, p), per_g(th, p), per_g(th, p), per_g(th, p),
                   per_g(8, p), per_g(8, p)],
        out_shape=[jax.ShapeDtypeStruct((dg, th, th), BF16),
                   jax.ShapeDtypeStruct((dg, th, p), BF16),
                   jax.ShapeDtypeStruct((dg, th, p), BF16),
                   jax.ShapeDtypeStruct((dg, th, p), BF16),
                   jax.ShapeDtypeStruct((dg, th, p), BF16),
                   jax.ShapeDtypeStruct((dg, 8, p), F32),
                   jax.ShapeDtypeStruct((dg, 8, p), F32)],
        compiler_params=_cparams(("parallel",)),
        name="ssm_weights",
    )(row(a_re), row(a_im), log_dt.reshape(dg, 1, 1),
      bt(b_re), bt(b_im), c_re, c_im, e_col, e_row)
    sw = lambda x: jnp.swapaxes(x, 1, 2)
    return tt, wor, woi, sw(wstr), sw(wsti), sw(scr), sw(sci), scr, sci


def _ssm_body(*refs, steps, chunks, gpb, has_init):
    if has_init:
        (u_ref, d_ref, tt_ref, wor_ref, woi_ref, wsr_ref, wsi_ref, scr_ref, sci_ref,
         s0r_ref, s0i_ref, y_ref, sfr_ref, sfi_ref, ys_ref, xr_ref, xi_ref) = refs
    else:
        (u_ref, d_ref, tt_ref, wor_ref, woi_ref, wsr_ref, wsi_ref, scr_ref, sci_ref,
         y_ref, sfr_ref, sfi_ref, ys_ref, xr_ref, xi_ref, st_r, st_i) = refs
    h = SSM_GROUP
    th = steps * h
    w = u_ref.shape[0] // steps
    p = wsr_ref.shape[1]
    ws_off = wsr_ref.shape[2] - th

    def step_rows(t):
        return pl.ds(t, w, stride=steps) if chunks > 1 else pl.ds(t * w, w)

    slabs = [u_ref[step_rows(t), :].T for t in range(steps)]

    for gl in range(gpb):
        rows = slice(gl * h, (gl + 1) * h)
        u = jnp.concatenate([s[rows, :] for s in slabs], axis=0)
        ub = u.astype(BF16)
        y = _dot(tt_ref[gl, :th, :th], ub) + jnp.concatenate([d_ref[gl]] * steps, axis=0) * u
        xr_ref[gl * p:(gl + 1) * p, :] = _dot(wsr_ref[gl, :, ws_off:], ub)
        xi_ref[gl * p:(gl + 1) * p, :] = _dot(wsi_ref[gl, :, ws_off:], ub)
        for t in range(steps):
            ys_ref[t, rows, :] = y[t * h:(t + 1) * h, :]

    gp = gpb * p
    col = lambda ref, k: ref[:, :, k:k + 1].reshape(gp, 1)
    xr, xi = xr_ref[...], xi_ref[...]
    if chunks > 1:
        pad = st_r.shape[0] - w
        c_row = lax.broadcasted_iota(jnp.int32, (w, 1), 0) & (chunks - 1)
        st_r[:pad, :] = jnp.zeros((pad, gp), F32)
        st_i[:pad, :] = jnp.zeros((pad, gp), F32)
        st_r[pad:, :] = xr.T
        st_i[pad:, :] = xi.T
        k = 0
        while (1 << k) < chunks:
            sh = 1 << k
            rr, ri = st_r[pl.ds(pad - sh, w), :], st_i[pl.ds(pad - sh, w), :]
            mr, mi = scr_ref[k:k + 1, :], sci_ref[k:k + 1, :]
            keep = c_row >= sh
            st_r[pad:, :] += jnp.where(keep, mr * rr - mi * ri, 0.0)
            st_i[pad:, :] += jnp.where(keep, mr * ri + mi * rr, 0.0)
            k += 1
        first = c_row >= 1
        xr_ref[...] = jnp.where(first, st_r[pl.ds(pad - 1, w), :], 0.0).T
        xi_ref[...] = jnp.where(first, st_i[pl.ds(pad - 1, w), :], 0.0).T
        for n in range(w // chunks):
            last = pad + n * chunks + chunks - 1
            sfr_ref[n:n + 1, :] = st_r[last:last + 1, :]
            sfi_ref[n:n + 1, :] = st_i[last:last + 1, :]
    else:
        pr, pi = s0r_ref[...].reshape(gp, w), s0i_ref[...].reshape(gp, w)
        lr, li = col(scr_ref, 7), col(sci_ref, 7)
        sfr_ref[...] = (lr * pr - li * pi + xr).reshape(gpb, p, w)
        sfi_ref[...] = (lr * pi + li * pr + xi).reshape(gpb, p, w)
        xr_ref[...] = pr
        xi_ref[...] = pi

    for gl in range(gpb):
        rows = slice(gl * h, (gl + 1) * h)
        pr = xr_ref[gl * p:(gl + 1) * p, :].astype(BF16)
        pi = xi_ref[gl * p:(gl + 1) * p, :].astype(BF16)
        ya = _dot(wor_ref[gl, :th, :], pr) + _dot(woi_ref[gl, :th, :], pi)
        for t in range(steps):
            y = ys_ref[t, rows, :] + ya[t * h:(t + 1) * h, :]
            ys_ref[t, rows, :] = 0.5 * y * (1.0 + jnp.tanh(GELU_C * (y + 0.044715 * (y * y * y))))
    for t in range(steps):
        y_ref[step_rows(t), :] = ys_ref[t].T


def _ssm(z, row_blk, rows, d_t, ops, layer, n_groups, steps, chunks, s0=None, gpb=8):
    tt, wor, woi, wsr, wsi, scr, sci, sc_rows_r, sc_rows_i = ops
    h = SSM_GROUP
    w = rows // steps
    nseq = w // chunks
    p = scr.shape[1]
    gp = gpb * p
    thf = tt.shape[1]
    nblk = n_groups // gpb
    base = layer * nblk
    wblk = lambda *s: pl.BlockSpec((gpb,) + s, lambda g: (base + g,) + (0,) * len(s))
    scratch = [pltpu.VMEM((steps, gpb * h, w), F32), pltpu.VMEM((gp, w), F32), pltpu.VMEM((gp, w), F32)]
    if s0 is None:
        sc_spec = pl.BlockSpec((None, sc_rows_r.shape[1], gp), lambda g: (base + g, 0, 0))
        sc_args = [sc_rows_r, sc_rows_i]
        sf_spec = pl.BlockSpec((None, nseq, gp), lambda g: (g, 0, 0))
        sf_shape = jax.ShapeDtypeStruct((nblk, nseq, gp), F32)
        scratch += [pltpu.VMEM((chunks + w, gp), F32)] * 2
    else:
        sc_spec = wblk(p, scr.shape[2])
        sc_args = [scr, sci]
        sf_spec = pl.BlockSpec((gpb, p, nseq), lambda g: (g, 0, 0))
        sf_shape = jax.ShapeDtypeStruct((n_groups, p, nseq), F32)
    in_specs = [pl.BlockSpec((rows, gpb * h), lambda g: (row_blk, g)),
                wblk(h, 1), wblk(thf, thf), wblk(thf, p), wblk(thf, p), wblk(p, thf), wblk(p, thf),
                sc_spec, sc_spec]
    args = [z, d_t, tt, wor, woi, wsr, wsi] + sc_args
    if s0 is not None:
        in_specs += [pl.BlockSpec((gpb, p, w), lambda g: (g, 0, 0))] * 2
        args += list(s0)
    return pl.pallas_call(
        functools.partial(_ssm_body, steps=steps, chunks=chunks, gpb=gpb, has_init=s0 is not None),
        grid=(nblk,),
        in_specs=in_specs,
        out_specs=[pl.BlockSpec((rows, gpb * h), lambda g: (0, g)), sf_spec, sf_spec],
        out_shape=[jax.ShapeDtypeStruct((rows, n_groups * h), F32), sf_shape, sf_shape],
        scratch_shapes=scratch,
        compiler_params=_cparams(("parallel",)),
        name="ssm_chunks" if s0 is None else "ssm_step",
    )(*args)


def _pool_seq_body(u_ref, w_ref, sc_ref, o_ref, z_ref):
    l, c = u_ref.shape
    pad = z_ref.shape[0] - l
    cg = c // len(POOL_WINDOWS)
    z_ref[:pad, :] = jnp.zeros((pad, c), F32)
    pos1 = (lax.broadcasted_iota(jnp.int32, (l, 1), 0) + 1).astype(F32)
    for gi, win in enumerate(POOL_WINDOWS):
        cols = slice(gi * cg, (gi + 1) * cg)
        cur = u_ref[:, cols]
        tot = cur
        sh = 1
        while sh < win:
            z_ref[pad:, cols] = tot
            tot = tot + z_ref[pad - sh:pad - sh + l, cols]
            sh *= 2
        inv_cnt = 1.0 / jnp.minimum(pos1, float(win))
        mixed = _dot((tot * inv_cnt - cur).astype(BF16), w_ref[gi].astype(BF16))
        o_ref[:, cols] = (mixed * sc_ref[:, cols]).astype(BF16)


def _pool_seq(z, w_pool, scale, layer, nseq, seqlen, width, col_blk):
    return pl.pallas_call(
        _pool_seq_body,
        grid=(nseq,),
        in_specs=[pl.BlockSpec((seqlen, width), lambda n: (n, col_blk)),
                  pl.BlockSpec((None,) + w_pool.shape[1:], lambda n: (layer, 0, 0, 0)),
                  pl.BlockSpec((None, 1, width), lambda n: (layer, 0, 0))],
        out_specs=pl.BlockSpec((seqlen, width), lambda n: (n, 0)),
        out_shape=jax.ShapeDtypeStruct((nseq * seqlen, width), BF16),
        scratch_shapes=[pltpu.VMEM((seqlen + 16, width), F32)],
        compiler_params=_cparams(("parallel",)),
        name="pool_seq",
    )(z, w_pool, scale)


def _pool_step_body(u_ref, prev_ref, w_ref, sc_ref, *rest):
    o_ref, nxt_ref = rest[-2:]
    n, buf, c = prev_ref.shape
    steps = u_ref.shape[0] // n
    cg = c // len(POOL_WINDOWS)

    def row(j, cols):
        return prev_ref[:, j, cols] if j < buf else u_ref[(j - buf) * n:(j - buf + 1) * n, cols]

    for gi, win in enumerate(POOL_WINDOWS):
        cols = slice(gi * cg, (gi + 1) * cg)
        for t in range(steps):
            cur = row(buf + t, cols)
            tot = cur
            for k in range(1, win):
                tot = tot + row(buf + t - k, cols)
            mixed = _dot((tot * (1.0 / win) - cur).astype(BF16), w_ref[gi].astype(BF16))
            o_ref[t * n:(t + 1) * n, cols] = (mixed * sc_ref[:, cols]).astype(BF16)
    for j in range(buf):
        nxt_ref[:, j, :] = row(j + steps, slice(None))


def _pool_step(z, prev, w_pool, scale, layer, rows, col_blk, carried=None):
    n, buf, c = prev.shape[1:]
    row_blk = z.shape[0] // rows - 1
    in_specs = [pl.BlockSpec((rows, c), lambda i: (row_blk, col_blk)),
                pl.BlockSpec((None, n, buf, c), lambda i: (layer, 0, 0, 0)),
                pl.BlockSpec((None,) + w_pool.shape[1:], lambda i: (layer, 0, 0, 0)),
                pl.BlockSpec((None, 1, c), lambda i: (layer, 0, 0))]
    args = [z, prev, w_pool, scale]
    aliases = {}
    if carried is not None:
        in_specs.append(pl.BlockSpec(memory_space=pl.ANY))
        args.append(carried)
        aliases = {len(args) - 1: 1}
    return pl.pallas_call(
        _pool_step_body,
        grid=(1,),
        in_specs=in_specs,
        out_specs=[pl.BlockSpec((rows, c), lambda i: (0, 0)),
                   pl.BlockSpec((None, n, buf, c), lambda i: (layer, 0, 0, 0))],
        out_shape=[jax.ShapeDtypeStruct((rows, c), BF16), jax.ShapeDtypeStruct(prev.shape, F32)],
        input_output_aliases=aliases,
        compiler_params=_cparams(("arbitrary",)),
        name="pool_step",
    )(*args)


def _mix_body(gap_ref, gas_ref, ybp_ref, ybs_ref, sa_ref, sb_ref, wa_ref, wb_ref, wp_ref, o_ref, *, p_tiles):
    first = pl.program_id(0) < p_tiles
    ga = jnp.where(first, gap_ref[...], gas_ref[...]).astype(BF16)
    yb =jnp.where(first, ybp_ref[...], ybs_ref[...])
    for cs in _col_chunks(o_ref.shape[1], MXU_COLS):
        br_a = _dot(ga, wa_ref[:, cs].astype(BF16)) * jax.nn.sigmoid(_dot(ga, wb_ref[:, cs].astype(BF16)))
        br_b = _dot(yb, wp_ref[:, cs].astype(BF16))
        o_ref[:, cs] = (sa_ref[:, cs] * br_a + sb_ref[:, cs] * br_b).astype(BF16)


def _resident(shape, layer):
    return pl.BlockSpec((None,) + shape, lambda i: (layer,) + (0,) * len(shape), pipeline_mode=pl.Buffered(1))


def _split_specs(tm, width, p_tiles):
    return [pl.BlockSpec((tm, width), lambda i: (jnp.minimum(i, p_tiles - 1), 0)),
            pl.BlockSpec((tm, width), lambda i: (jnp.maximum(i - p_tiles, 0), 0))]


def _mix(ga_p, ga_s, yb_p, yb_s, z, wa, wb, wp, layer, tm, gate_col):
    m = z.shape[0]
    k = ga_p.shape[1]
    n = wa.shape[-1]
    gblk = gate_col // n
    p_tiles = ga_p.shape[0] // tm
    return pl.pallas_call(
        functools.partial(_mix_body, p_tiles=p_tiles),
        grid=(m // tm,),
        in_specs=_split_specs(tm, k, p_tiles) + _split_specs(tm, k, p_tiles) + [
                  pl.BlockSpec((tm, n), lambda i: (i, gblk)),
                  pl.BlockSpec((tm, n), lambda i: (i, gblk + 1)),
                  _resident((k, n), layer), _resident((k, n), layer), _resident((k, n), layer)],
        out_specs=pl.BlockSpec((tm, n), lambda i: (i, 0)),
        out_shape=jax.ShapeDtypeStruct((m, n), BF16),
        compiler_params=_cparams(("parallel",)),
        name="mix",
    )(ga_p, ga_s, yb_p, yb_s, z, z, wa, wb, wp)


def _resmm_body(a_ref, w_ref, h_ref, g_ref, o_ref, xn_ref):
    a = a_ref[...]
    for cs in _col_chunks(o_ref.shape[1], MXU_COLS):
        o_ref[:, cs] = h_ref[:, cs] + _dot(a, w_ref[:, cs].astype(BF16))
    xn_ref[...] = _rms_bf16(o_ref[...], g_ref[...])


def _resmm(a, w, hres, g_next, layer, tm):
    m, k = a.shape
    n = w.shape[-1]
    return pl.pallas_call(
        _resmm_body,
        grid=(m // tm,),
        in_specs=[pl.BlockSpec((tm, k), lambda i: (i, 0)),
                  _resident((k, n), layer),
                  pl.BlockSpec((tm, n), lambda i: (i, 0)),
                  pl.BlockSpec((None, 1, n), lambda i: (layer, 0, 0))],
        out_specs=[pl.BlockSpec((tm, n), lambda i: (i, 0)), pl.BlockSpec((tm, n), lambda i: (i, 0))],
        out_shape=[jax.ShapeDtypeStruct((m, n), F32), jax.ShapeDtypeStruct((m, n), BF16)],
        compiler_params=_cparams(("parallel",)),
        name="resmm",
    )(a, w, hres, g_next)


def _ple_update(x_ref, g_ref, pp_ref, ps_ref, wp_ref, wg_ref, dst_ref, p_tiles):
    xn = _rms_bf16(x_ref[...], g_ref[...])
    pb = jnp.where(pl.program_id(0) < p_tiles, pp_ref[...], ps_ref[...]).astype(BF16)
    for cs in _col_chunks(dst_ref.shape[1], MXU_COLS):
        gate = jax.nn.sigmoid(_dot(xn, wg_ref[:, cs].astype(BF16)))
        dst_ref[:, cs] = x_ref[:, cs] + _dot(pb, wp_ref[:, cs].astype(BF16)) * gate


def _ple_body(x_ref, g_ref, pp_ref, ps_ref, wp_ref, wg_ref, gn_ref, o_ref, xn_ref, *, p_tiles):
    _ple_update(x_ref, g_ref, pp_ref, ps_ref, wp_ref, wg_ref, o_ref, p_tiles)
    xn_ref[...] = _rms_bf16(o_ref[...], gn_ref[...])


def _ple_final_body(x_ref, g_ref, pp_ref, ps_ref, wp_ref, wg_ref, gf_ref, op_ref, os_ref, h_ref, *, p_tiles):
    _ple_update(x_ref, g_ref, pp_ref, ps_ref, wp_ref, wg_ref, h_ref, p_tiles)
    h = h_ref[...]
    y = h * lax.rsqrt(jnp.mean(h * h, axis=-1, keepdims=True) + RMS_EPS) * gf_ref[...]
    i = pl.program_id(0)

    @pl.when(i < p_tiles)
    def _():
        op_ref[...] = y

    @pl.when(i >= p_tiles)
    def _():
        os_ref[...] = y


def _ple(x, g, p_p, p_s, wp, wg, layer, tm, g_next=None, g_final=None):
    m, d = x.shape
    mp, pd = p_p.shape[1:]
    p_tiles = mp // tm
    in_specs = [pl.BlockSpec((tm, d), lambda i: (i, 0)),
                pl.BlockSpec((None, 1, d), lambda i: (layer, 0, 0)),
                pl.BlockSpec((None, tm, pd), lambda i: (layer, jnp.minimum(i, p_tiles - 1), 0)),
                pl.BlockSpec((None, tm, pd), lambda i: (layer, jnp.maximum(i - p_tiles, 0), 0)),
                _resident((pd, d), layer), _resident((d, d), layer)]
    if g_final is None:
        row = pl.BlockSpec((tm, d), lambda i: (i, 0))
        return pl.pallas_call(
            functools.partial(_ple_body, p_tiles=p_tiles), grid=(m // tm,),
            in_specs=in_specs + [pl.BlockSpec((None, 1, d), lambda i: (layer + 1, 0, 0))],
            out_specs=[row, row],
            out_shape=[jax.ShapeDtypeStruct((m, d), F32), jax.ShapeDtypeStruct((m, d), BF16)],
            compiler_params=_cparams(("parallel",)),
            name="ple",
        )(x, g, p_p, p_s, wp, wg, g_next)
    return pl.pallas_call(
        functools.partial(_ple_final_body, p_tiles=p_tiles),
        grid=(m // tm,),
        in_specs=in_specs + [pl.BlockSpec((1, d), lambda i: (0, 0))],
        out_specs=[pl.BlockSpec((tm, d), lambda i: (jnp.minimum(i, p_tiles - 1), 0)),
                   pl.BlockSpec((tm, d), lambda i: (jnp.maximum(i - p_tiles, 0), 0))],
        out_shape=[jax.ShapeDtypeStruct((mp, d), F32), jax.ShapeDtypeStruct((m - mp, d), F32)],
        scratch_shapes=[pltpu.VMEM((tm, d), F32)],
        compiler_params=_cparams(("arbitrary",)),
        name="ple_final",
    )(x, g, p_p, p_s, wp, wg, g_final)


def _prep_body(xp_ref, xs_ref, g_ref, h_ref, xn_ref, *, p_tiles):
    x = jnp.where(pl.program_id(0) < p_tiles, xp_ref[...], xs_ref[...])
    h_ref[...] = x
    xn_ref[...] = _rms_bf16(x, g_ref[...])


def _prep(x_p, x_s, g, tm):
    d = x_p.shape[1]
    m = x_p.shape[0] + x_s.shape[0]
    p_tiles = x_p.shape[0] // tm
    row = pl.BlockSpec((tm, d), lambda i: (i, 0))
    return pl.pallas_call(
        functools.partial(_prep_body, p_tiles=p_tiles),
        grid=(m // tm,),
        in_specs=_split_specs(tm, d, p_tiles) + [pl.BlockSpec((None, 1, d), lambda i: (0, 0, 0))],
        out_specs=[row, row],
        out_shape=[jax.ShapeDtypeStruct((m, d), F32), jax.ShapeDtypeStruct((m, d), BF16)],
        compiler_params=_cparams(("parallel",)),
        name="prep",
    )(x_p, x_s, g)


def _pick_tile(n, pref):
    t = min(pref, n)
    while n % t:
        t //= 2
    return t


def kernel(x_prompt, x_sample, state_ssm_re, state_ssm_im, state_pool, p_prompt, p_sample, g_ffn1, w_ffn1_gate, w_ffn1_up, w_ffn1_down, g_mix, w_in, ssm_a_re, ssm_a_im, ssm_log_dt, ssm_b_re, ssm_b_im, ssm_c_re, ssm_c_im, ssm_d, w_glu_a, w_glu_b, w_pool, pool_scale, w_pool_up, w_out, g_ffn2, w_ffn2_gate, w_ffn2_up, w_ffn2_down, g_ple, w_ple, w_ple_gate, g_final):
    nb, seq, d = x_prompt.shape
    ns, dseq, _ = x_sample.shape
    depth, n_groups, p_state = ssm_a_re.shape
    h = ssm_b_re.shape[-1]
    sw = n_groups * h
    pw = pool_scale.shape[-1]
    buf = state_pool.shape[2]
    chunk = SSM_CHUNK
    assert h == SSM_GROUP and dseq * 2 == chunk and seq % chunk == 0 and buf == max(POOL_WINDOWS) - 1
    n_chunks = seq // chunk
    assert n_chunks & (n_chunks - 1) == 0 and n_chunks <= 128
    assert all(w & (w - 1) == 0 for w in POOL_WINDOWS) and sw % pw == 0
    mp, ms = nb * seq, ns * dseq
    m = mp + ms
    tm = _pick_tile(m, 1024)

    g3 = lambda a: a.reshape(depth, 1, -1)
    wg1, wu1, wd1 = w_ffn1_gate, w_ffn1_up, w_ffn1_down
    wg2, wu2, wd2 = w_ffn2_gate, w_ffn2_up, w_ffn2_down
    wga, wgb, wpu, wo = w_glu_a, w_glu_b, w_pool_up, w_out
    wpl, wpg, wpool = w_ple, w_ple_gate, w_pool
    tn_in = _pick_tile(sw + pw, 2048)
    gf1, gmx, gf2, gpl = g3(g_ffn1), g3(g_mix), g3(g_ffn2), g3(g_ple)
    pscale = g3(pool_scale)

    flat = lambda a: a.reshape((depth * n_groups,) + a.shape[2:])
    gpb = min(8, n_groups)
    ops = _ssm_weights(flat(ssm_a_re), flat(ssm_a_im), flat(ssm_log_dt), flat(ssm_b_re), flat(ssm_b_im),
                       flat(ssm_c_re), flat(ssm_c_im), chunk)
    blocked = lambda a: (a.reshape(-1, gpb, a.shape[1], p_state).transpose(0, 2, 1, 3)
                         .reshape(-1, a.shape[1], gpb * p_state))
    ops = ops[:-2] + (blocked(ops[-2]), blocked(ops[-1]))
    d_t = flat(ssm_d).reshape(depth * n_groups, h, 1)

    p_p = p_prompt.reshape(depth, mp, -1)
    p_s = p_sample.transpose(0, 2, 1, 3).reshape(depth, ms, -1)
    tf = _pick_tile(w_ffn1_gate.shape[-1], 512)
    tm2 = _pick_tile(math.gcd(mp, ms), 512)
    hcur, xn = _prep(x_prompt.reshape(mp, d), x_sample.transpose(1, 0, 2).reshape(ms, d), gf1, tm2)
    new_re_p, new_im_p, new_pool_p, new_re_s, new_im_s = [], [], [], [], []
    new_pool_s = None
    for i in range(depth):
        h1 = _ffn(xn, hcur, wg1, wu1, wd1, i, tm, tf)
        z = _inproj(h1, gmx, w_in, i, tm, tn_in, sw + pw)

        ga_p, sr_p, si_p = _ssm(z, 0, mp, d_t, ops, i, n_groups, chunk, n_chunks, gpb=gpb)
        s0 = (state_ssm_re[i].transpose(1, 2, 0), state_ssm_im[i].transpose(1, 2, 0))
        ga_s, sr_s, si_s = _ssm(z, mp // ms, ms, d_t, ops, i, n_groups, dseq, 1, s0=s0, gpb=gpb)
        unblock = lambda a: (a.reshape(-1, nb, gpb, p_state).transpose(1, 0, 2, 3)
                             .reshape(nb, n_groups, p_state))
        new_re_p.append(unblock(sr_p))
        new_im_p.append(unblock(si_p))
        new_re_s.append(sr_s.transpose(2, 0, 1))
        new_im_s.append(si_s.transpose(2, 0, 1))

        yb_p = _pool_seq(z, wpool, pscale, i, nb, seq, pw, sw // pw)
        yb_s, new_pool_s = _pool_step(z, state_pool, wpool, pscale, i, ms, sw // pw, carried=new_pool_s)
        new_pool_p.append(jnp.stack([z[(n + 1) * seq - buf:(n + 1) * seq, sw:sw + pw] for n in range(nb)]))

        merged = _mix(ga_p, ga_s, yb_p, yb_s, z, wga, wgb, wpu, i, tm2, sw + pw)
        h2, xn2 = _resmm(merged, wo, h1, gf2, i, tm2)
        h3 = _ffn(xn2, h2, wg2, wu2, wd2, i, tm, tf)
        if i + 1 < depth:
            hcur, xn = _ple(h3, gpl, p_p, p_s, wpl, wpg, i, tm2, g_next=gf1)
        else:
            y_p, y_s = _ple(h3, gpl, p_p, p_s, wpl, wpg, i, tm2, g_final=g_final.reshape(1, d))

    return (y_p.reshape(nb, seq, d), y_s.reshape(dseq, ns, d).transpose(1, 0, 2),
            jnp.stack(new_re_p), jnp.stack(new_im_p), jnp.stack(new_pool_p),
            jnp.stack(new_re_s), jnp.stack(new_im_s), new_pool_s)
```

```python
import functools
import math

import numpy as np
import jax
import jax.numpy as jnp
from jax import lax
from jax.experimental import pallas as pl
from jax.experimental.pallas import tpu as pltpu

F32 = jnp.float32
BF16 = jnp.bfloat16
RMS_EPS = 1e-6
POOL_WINDOWS = (2, 4, 8, 16)
SSM_GROUP = 16
SSM_CHUNK = 16
GELU_C = math.sqrt(2.0 / math.pi)
VMEM_LIMIT = 62 * 1024 * 1024
MXU_COLS = 256


def _cparams(sem):
    return pltpu.CompilerParams(dimension_semantics=sem, vmem_limit_bytes=VMEM_LIMIT)


def _rms_bf16(x, g):
    inv = lax.rsqrt(jnp.mean(x * x, axis=-1, keepdims=True) + RMS_EPS)
    return (x * inv * g).astype(BF16)


def _dot(a, b):
    return jnp.dot(a, b, preferred_element_type=F32)


def _col_chunks(width, chunk):
    chunk = min(chunk, width)
    return [slice(c0, c0 + chunk) for c0 in range(0, width, chunk)]


def _ffn_body(xn_ref, h_ref, wg_ref, wu_ref, wd_ref, *rest, res_chunks):
    gn_ref, o_ref, xo_ref = rest if len(rest) == 3 else (None,) + rest + (None,)
    j = pl.program_id(1)
    nj = pl.num_programs(1)

    @pl.when(j == 0)
    def _():
        o_ref[...] = jnp.zeros_like(o_ref)

    xn = xn_ref[...]
    mids = []
    for cs in _col_chunks(wg_ref.shape[1], MXU_COLS):
        a = _dot(xn, wg_ref[:, cs].astype(BF16))
        b = _dot(xn, wu_ref[:, cs].astype(BF16))
        mids.append((0.5 * a * jax.nn.sigmoid(a) * b).astype(BF16))
    mid = jnp.concatenate(mids, axis=1)
    chunks = _col_chunks(o_ref.shape[1], 2 * MXU_COLS)
    last = chunks.pop()
    chunks += [slice(c0, c0 + MXU_COLS) for c0 in range(last.start, last.stop, MXU_COLS)]
    for cs in chunks:
        o_ref[:, cs] += _dot(mid, wd_ref[:, cs].astype(BF16))

    cw = o_ref.shape[1] // res_chunks
    for c in range(res_chunks):
        @pl.when(j == nj - res_chunks + c)
        def _(c=c):
            o_ref[:, c * cw:(c + 1) * cw] += h_ref[...]

    if xo_ref is not None:
        @pl.when(j == nj - 1)
        def _():
            xo_ref[...] = _rms_bf16(o_ref[...], gn_ref[...])


def _ffn(xn, hres, wg, wu, wd, layer, tm, tf, g_next=None):
    m, d = xn.shape
    f = wg.shape[-1]
    nj = f // tf
    res_chunks = min(4, nj)
    row = lambda: pl.BlockSpec((tm, d), lambda i, j: (i, 0))
    in_specs = [
        pl.BlockSpec((tm, d), lambda i, j: (i, 0)),
        pl.BlockSpec((tm, d // res_chunks),
                     lambda i, j: (i, jnp.clip(j - (nj - res_chunks), 0, res_chunks - 1))),
        pl.BlockSpec((None, d, tf), lambda i, j: (layer, 0, j)),
        pl.BlockSpec((None, d, tf), lambda i, j: (layer, 0, j)),
        pl.BlockSpec((None, tf, d), lambda i, j: (layer, j, 0)),
    ]
    args = [xn, hres, wg, wu, wd]
    out_specs, out_shape = row(), jax.ShapeDtypeStruct((m, d), F32)
    if g_next is not None:
        in_specs.append(pl.BlockSpec((None, 1, d), lambda i, j: (layer, 0, 0)))
        args.append(g_next)
        out_specs, out_shape = [row(), row()], [out_shape, jax.ShapeDtypeStruct((m, d), BF16)]
    return pl.pallas_call(
        functools.partial(_ffn_body, res_chunks=res_chunks),
        grid=(m // tm, nj),
        in_specs=in_specs,
        out_specs=out_specs,
        out_shape=out_shape,
        compiler_params=_cparams(("parallel", "arbitrary")),
        name="ffn",
    )(*args)


def _inproj_body(x_ref, g_ref, w_ref, o_ref, *, gate_from):
    is_gate = pl.program_id(0) >= gate_from
    xn = _rms_bf16(x_ref[...], g_ref[...])
    for cs in _col_chunks(o_ref.shape[1], MXU_COLS):
        r = _dot(xn, w_ref[:, cs].astype(BF16))
        o_ref[:, cs] = jnp.where(is_gate, jax.nn.sigmoid(r), r)


def _inproj(x, g, w, layer, tm, tn, n_plain):
    m, d = x.shape
    n = w.shape[-1]
    return pl.pallas_call(
        functools.partial(_inproj_body, gate_from=n_plain // tn),
        grid=(n // tn, m // tm),
        in_specs=[
            pl.BlockSpec((tm, d), lambda j, i: (i, 0)),
            pl.BlockSpec((None, 1, d), lambda j, i: (layer, 0, 0)),
            pl.BlockSpec((None, d, tn), lambda j, i: (layer, 0, j), pipeline_mode=pl.Buffered(1)),
        ],
        out_specs=pl.BlockSpec((tm, tn), lambda j, i: (i, j)),
        out_shape=jax.ShapeDtypeStruct((m, n), F32),
        compiler_params=_cparams(("arbitrary", "arbitrary")),
        name="inproj",
    )(x, g, w)


def _cmul(ar, ai, br, bi):
    return ar * br - ai * bi, ar * bi + ai * br


def _dot_nt3(a, b):
    nt = (((1,), (1,)), ((), ()))
    dot = lambda x, y: lax.dot_general(x, y, nt, preferred_element_type=F32)
    ah, bh = a.astype(BF16), b.astype(BF16)
    al = (a - ah.astype(F32)).astype(BF16)
    bl = (b - bh.astype(F32)).astype(BF16)
    return dot(ah, bh) + dot(ah, bl) + dot(al, bh)


def _ssm_weights_body(ar_ref, ai_ref, ldt_ref, btr_ref, bti_ref, ctr_ref, cti_ref, e_col, e_row,
                      tt_o, wor_o, woi_o, wstr_o, wsti_o, scr_o, sci_o, *, chunk, gblk):
    h = SSM_GROUP
    causal = e_col[...] >= e_row[...]

    def rows(pows):
        width = pows[0][0].shape[1]
        return tuple(jnp.concatenate([jnp.broadcast_to(x[k], (h, width)) for x in pows], axis=0)
                     for k in (0, 1))

    for gl in range(gblk):
        dt = jnp.exp(ldt_ref[gl])
        a_re, a_im = ar_ref[gl], ai_ref[gl]
        mag = jnp.exp(a_re * dt)
        ang = a_im * dt
        lr, li = mag * jnp.cos(ang), mag * jnp.sin(ang)
        den = a_re * a_re + a_im * a_im
        num_re = lr - 1.0
        k_re = (num_re * a_re + li * a_im) / den
        k_im = (li * a_re - num_re * a_im) / den
        inv = 1.0 / (lr * lr + li * li)
        nr, ni = lr * inv, -li * inv
        pw = [(jnp.ones_like(lr), jnp.zeros_like(lr))]
        npw = list(pw)
        for _ in range(chunk):
            pw.append(_cmul(*pw[-1], lr, li))
            npw.append(_cmul(*npw[-1], nr, ni))
        tile = lambda x: jnp.concatenate([x] * chunk, axis=0)
        kb = _cmul(k_re, k_im, tile(btr_ref[gl]), tile(bti_ref[gl]))
        c = (tile(ctr_ref[gl]), tile(cti_ref[gl]))

        l_re, l_im = _cmul(*c, *rows(pw[:chunk]))
        r_re, r_im = _cmul(*rows(npw[:chunk]), *kb)
        kmat = _dot_nt3(l_re, r_re) - _dot_nt3(l_im, r_im)
        tt_o[gl] = jnp.where(causal, kmat, 0.0).astype(BF16)

        e_re, e_im = _cmul(*c, *rows(pw[1:chunk + 1]))
        wor_o[gl] = e_re.astype(BF16)
        woi_o[gl] = (-e_im).astype(BF16)

        s_re, s_im = _cmul(*rows(pw[chunk - 1::-1]), *kb)
        wstr_o[gl] = s_re.astype(BF16)
        wsti_o[gl] = s_im.astype(BF16)

        sc = [pw[chunk]]
        for _ in range(6):
            sc.append(_cmul(*sc[-1], *sc[-1]))
        sc.append(pw[chunk // 2])
        scr_o[gl] = jnp.concatenate([x[0] for x in sc], axis=0)
        sci_o[gl] = jnp.concatenate([x[1] for x in sc], axis=0)


def _ssm_weights(a_re, a_im, log_dt, b_re, b_im, c_re, c_im, chunk):
    dg, p = a_re.shape
    h = b_re.shape[-1]
    th = chunk * h
    tau = np.repeat(np.arange(chunk, dtype=np.float32), h)
    e_col = jnp.asarray(tau.reshape(th, 1))
    e_row = jnp.asarray(tau.reshape(1, th))
    gblk = min(16, dg)
    row = lambda x: x.reshape(dg, 1, p)
    bt = lambda x: jnp.swapaxes(x, 1, 2)
    per_g = lambda *s: pl.BlockSpec((gblk,) + s, lambda g: (g,) + (0,) * len(s))
    const = lambda *s: pl.BlockSpec(s, lambda g: (0,) * len(s))
    tt, wor, woi, wstr, wsti, scr, sci = pl.pallas_call(
        functools.partial(_ssm_weights_body, chunk=chunk, gblk=gblk),
        grid=(dg // gblk,),
        in_specs=[per_g(1, p), per_g(1, p), per_g(1, 1),
                  per_g(h, p), per_g(h, p), per_g(h, p), per_g(h, p),
                  const(th, 1), const(1, th)],
        out_specs=[per_g(th, th), per_g(th, p), per_g(th, p), per_g(th, p), per_g(th, p),
                   per_g(8, p), per_g(8, p)],
        out_shape=[jax.ShapeDtypeStruct((dg, th, th), BF16),
                   jax.ShapeDtypeStruct((dg, th, p), BF16),
                   jax.ShapeDtypeStruct((dg, th, p), BF16),
                   jax.ShapeDtypeStruct((dg, th, p), BF16),
                   jax.ShapeDtypeStruct((dg, th, p), BF16),
                   jax.ShapeDtypeStruct((dg, 8, p), F32),
                   jax.ShapeDtypeStruct((dg, 8, p), F32)],
        compiler_params=_cparams(("parallel",)),
        name="ssm_weights",
    )(row(a_re), row(a_im), log_dt.reshape(dg, 1, 1),
      bt(b_re), bt(b_im), c_re, c_im, e_col, e_row)
    sw = lambda x: jnp.swapaxes(x, 1, 2)
    return tt, wor, woi, sw(wstr), sw(wsti), sw(scr), sw(sci), scr, sci


def _ssm_body(*refs, steps, chunks, gpb, has_init):
    if has_init:
        (u_ref, d_ref, tt_ref, wor_ref, woi_ref, wsr_ref, wsi_ref, scr_ref, sci_ref,
         s0r_ref, s0i_ref, y_ref, sfr_ref, sfi_ref, ys_ref, xr_ref, xi_ref) = refs
    else:
        (u_ref, d_ref, tt_ref, wor_ref, woi_ref, wsr_ref, wsi_ref, scr_ref, sci_ref,
         y_ref, sfr_ref, sfi_ref, ys_ref, xr_ref, xi_ref, st_r, st_i) = refs
    h = SSM_GROUP
    th = steps * h
    w = u_ref.shape[0] // steps
    p = wsr_ref.shape[1]
    ws_off = wsr_ref.shape[2] - th

    def step_rows(t):
        return pl.ds(t, w, stride=steps) if chunks > 1 else pl.ds(t * w, w)

    slabs = [u_ref[step_rows(t), :].T for t in range(steps)]

    for gl in range(gpb):
        rows = slice(gl * h, (gl + 1) * h)
        u = jnp.concatenate([s[rows, :] for s in slabs], axis=0)
        ub = u.astype(BF16)
        y = _dot(tt_ref[gl, :th, :th], ub) + jnp.concatenate([d_ref[gl]] * steps, axis=0) * u
        xr_ref[gl * p:(gl + 1) * p, :] = _dot(wsr_ref[gl, :, ws_off:], ub)
        xi_ref[gl * p:(gl + 1) * p, :] = _dot(wsi_ref[gl, :, ws_off:], ub)
        for t in range(steps):
            ys_ref[t, rows, :] = y[t * h:(t + 1) * h, :]

    gp = gpb * p
    col = lambda ref, k: ref[:, :, k:k + 1].reshape(gp, 1)
    xr, xi = xr_ref[...], xi_ref[...]
    if chunks > 1:
        pad = st_r.shape[0] - w
        c_row = lax.broadcasted_iota(jnp.int32, (w, 1), 0) & (chunks - 1)
        st_r[:pad, :] = jnp.zeros((pad, gp), F32)
        st_i[:pad, :] = jnp.zeros((pad, gp), F32)
        st_r[pad:, :] = xr.T
        st_i[pad:, :] = xi.T
        k = 0
        while (1 << k) < chunks:
            sh = 1 << k
            rr, ri = st_r[pl.ds(pad - sh, w), :], st_i[pl.ds(pad - sh, w), :]
            mr, mi = scr_ref[k:k + 1, :], sci_ref[k:k + 1, :]
            keep = c_row >= sh
            st_r[pad:, :] += jnp.where(keep, mr * rr - mi * ri, 0.0)
            st_i[pad:, :] += jnp.where(keep, mr * ri + mi * rr, 0.0)
            k += 1
        first = c_row >= 1
        xr_ref[...] = jnp.where(first, st_r[pl.ds(pad - 1, w), :], 0.0).T
        xi_ref[...] = jnp.where(first, st_i[pl.ds(pad - 1, w), :], 0.0).T
        for n in range(w // chunks):
            last = pad + n * chunks + chunks - 1
            sfr_ref[n:n + 1, :] = st_r[last:last + 1, :]
            sfi_ref[n:n + 1, :] = st_i[last:last + 1, :]
    else:
        pr, pi = s0r_ref[...].reshape(gp, w), s0i_ref[...].reshape(gp, w)
        lr, li = col(scr_ref, 7), col(sci_ref, 7)
        sfr_ref[...] = (lr * pr - li * pi + xr).reshape(gpb, p, w)
        sfi_ref[...] = (lr * pi + li * pr + xi).reshape(gpb, p, w)
        xr_ref[...] = pr
        xi_ref[...] = pi

    for gl in range(gpb):
        rows = slice(gl * h, (gl + 1) * h)
        pr = xr_ref[gl * p:(gl + 1) * p, :].astype(BF16)
        pi = xi_ref[gl * p:(gl + 1) * p, :].astype(BF16)
        ya = _dot(wor_ref[gl, :th, :], pr) + _dot(woi_ref[gl, :th, :], pi)
        for t in range(steps):
            y = ys_ref[t, rows, :] + ya[t * h:(t + 1) * h, :]
            ys_ref[t, rows, :] = 0.5 * y * (1.0 + jnp.tanh(GELU_C * (y + 0.044715 * (y * y * y))))
    for t in range(steps):
        y_ref[step_rows(t), :] = ys_ref[t].T


def _ssm(z, row_blk, rows, d_t, ops, layer, n_groups, steps, chunks, s0=None, gpb=8):
    tt, wor, woi, wsr, wsi, scr, sci, sc_rows_r, sc_rows_i = ops
    h = SSM_GROUP
    w = rows // steps
    nseq = w // chunks
    p = scr.shape[1]
    gp = gpb * p
    thf = tt.shape[1]
    nblk = n_groups // gpb
    base = layer * nblk
    wblk = lambda *s: pl.BlockSpec((gpb,) + s, lambda g: (base + g,) + (0,) * len(s))
    scratch = [pltpu.VMEM((steps, gpb * h, w), F32), pltpu.VMEM((gp, w), F32), pltpu.VMEM((gp, w), F32)]
    if s0 is None:
        sc_spec = pl.BlockSpec((None, sc_rows_r.shape[1], gp), lambda g: (base + g, 0, 0))
        sc_args = [sc_rows_r, sc_rows_i]
        sf_spec = pl.BlockSpec((None, nseq, gp), lambda g: (g, 0, 0))
        sf_shape = jax.ShapeDtypeStruct((nblk, nseq, gp), F32)
        scratch += [pltpu.VMEM((chunks + w, gp), F32)] * 2
    else:
        sc_spec = wblk(p, scr.shape[2])
        sc_args = [scr, sci]
        sf_spec = pl.BlockSpec((gpb, p, nseq), lambda g: (g, 0, 0))
        sf_shape = jax.ShapeDtypeStruct((n_groups, p, nseq), F32)
    in_specs = [pl.BlockSpec((rows, gpb * h), lambda g: (row_blk, g)),
                wblk(h, 1), wblk(thf, thf), wblk(thf, p), wblk(thf, p), wblk(p, thf), wblk(p, thf),
                sc_spec, sc_spec]
    args = [z, d_t, tt, wor, woi, wsr, wsi] + sc_args
    if s0 is not None:
        in_specs += [pl.BlockSpec((gpb, p, w), lambda g: (g, 0, 0))] * 2
        args += list(s0)
    return pl.pallas_call(
        functools.partial(_ssm_body, steps=steps, chunks=chunks, gpb=gpb, has_init=s0 is not None),
        grid=(nblk,),
        in_specs=in_specs,
        out_specs=[pl.BlockSpec((rows, gpb * h), lambda g: (0, g)), sf_spec, sf_spec],
        out_shape=[jax.ShapeDtypeStruct((rows, n_groups * h), F32), sf_shape, sf_shape],
        scratch_shapes=scratch,
        compiler_params=_cparams(("parallel",)),
        name="ssm_chunks" if s0 is None else "ssm_step",
    )(*args)


def _pool_seq_body(u_ref, w_ref, sc_ref, o_ref, z_ref):
    l, c = u_ref.shape
    pad = z_ref.shape[0] - l
    cg = c // len(POOL_WINDOWS)
    z_ref[:pad, :] = jnp.zeros((pad, c), F32)
    pos1 = (lax.broadcasted_iota(jnp.int32, (l, 1), 0) + 1).astype(F32)
    for gi, win in enumerate(POOL_WINDOWS):
        cols = slice(gi * cg, (gi + 1) * cg)
        cur = u_ref[:, cols]
        tot = cur
        sh = 1
        while sh < win:
            z_ref[pad:, cols] = tot
            tot = tot + z_ref[pad - sh:pad - sh + l, cols]
            sh *= 2
        inv_cnt = 1.0 / jnp.minimum(pos1, float(win))
        mixed = _dot((tot * inv_cnt - cur).astype(BF16), w_ref[gi].astype(BF16))
        o_ref[:, cols] = (mixed * sc_ref[:, cols]).astype(BF16)


def _pool_seq(z, w_pool, scale, layer, nseq, seqlen, width, col_blk):
    return pl.pallas_call(
        _pool_seq_body,
        grid=(nseq,),
        in_specs=[pl.BlockSpec((seqlen, width), lambda n: (n, col_blk)),
                  pl.BlockSpec((None,) + w_pool.shape[1:], lambda n: (layer, 0, 0, 0)),
                  pl.BlockSpec((None, 1, width), lambda n: (layer, 0, 0))],
        out_specs=pl.BlockSpec((seqlen, width), lambda n: (n, 0)),
        out_shape=jax.ShapeDtypeStruct((nseq * seqlen, width), BF16),
        scratch_shapes=[pltpu.VMEM((seqlen + 16, width), F32)],
        compiler_params=_cparams(("parallel",)),
        name="pool_seq",
    )(z, w_pool, scale)


def _pool_step_body(u_ref, prev_ref, w_ref, sc_ref, *rest):
    o_ref, nxt_ref = rest[-2:]
    n, buf, c = prev_ref.shape
    steps = u_ref.shape[0] // n
    cg = c // len(POOL_WINDOWS)

    def row(j, cols):
        return prev_ref[:, j, cols] if j < buf else u_ref[(j - buf) * n:(j - buf + 1) * n, cols]

    for gi, win in enumerate(POOL_WINDOWS):
        cols = slice(gi * cg, (gi + 1) * cg)
        for t in range(steps):
            cur = row(buf + t, cols)
            tot = cur
            for k in range(1, win):
                tot = tot + row(buf + t - k, cols)
            mixed = _dot((tot * (1.0 / win) - cur).astype(BF16), w_ref[gi].astype(BF16))
            o_ref[t * n:(t + 1) * n, cols] = (mixed * sc_ref[:, cols]).astype(BF16)
    for j in range(buf):
        nxt_ref[:, j, :] = row(j + steps, slice(None))


def _pool_step(z, prev, w_pool, scale, layer, rows, col_blk, carried=None):
    n, buf, c = prev.shape[1:]
    row_blk = z.shape[0] // rows - 1
    in_specs = [pl.BlockSpec((rows, c), lambda i: (row_blk, col_blk)),
                pl.BlockSpec((None, n, buf, c), lambda i: (layer, 0, 0, 0)),
                pl.BlockSpec((None,) + w_pool.shape[1:], lambda i: (layer, 0, 0, 0)),
                pl.BlockSpec((None, 1, c), lambda i: (layer, 0, 0))]
    args = [z, prev, w_pool, scale]
    aliases = {}
    if carried is not None:
        in_specs.append(pl.BlockSpec(memory_space=pl.ANY))
        args.append(carried)
        aliases = {len(args) - 1: 1}
    return pl.pallas_call(
        _pool_step_body,
        grid=(1,),
        in_specs=in_specs,
        out_specs=[pl.BlockSpec((rows, c), lambda i: (0, 0)),
                   pl.BlockSpec((None, n, buf, c), lambda i: (layer, 0, 0, 0))],
        out_shape=[jax.ShapeDtypeStruct((rows, c), BF16), jax.ShapeDtypeStruct(prev.shape, F32)],
        input_output_aliases=aliases,
        compiler_params=_cparams(("arbitrary",)),
        name="pool_step",
    )(*args)


def _mix_body(gap_ref, gas_ref, ybp_ref, ybs_ref, sa_ref, sb_ref, wa_ref, wb_ref, wp_ref, o_ref, *, p_tiles):
    first = pl.program_id(0) < p_tiles
    ga = jnp.where(first, gap_ref[...], gas_ref[...]).astype(BF16)
    yb =jnp.where(first, ybp_ref[...], ybs_ref[...])
    for cs in _col_chunks(o_ref.shape[1], MXU_COLS):
        br_a = _dot(ga, wa_ref[:, cs].astype(BF16)) * jax.nn.sigmoid(_dot(ga, wb_ref[:, cs].astype(BF16)))
        br_b = _dot(yb, wp_ref[:, cs].astype(BF16))
        o_ref[:, cs] = (sa_ref[:, cs] * br_a + sb_ref[:, cs] * br_b).astype(BF16)


def _resident(shape, layer):
    return pl.BlockSpec((None,) + shape, lambda i: (layer,) + (0,) * len(shape), pipeline_mode=pl.Buffered(1))


def _split_specs(tm, width, p_tiles):
    return [pl.BlockSpec((tm, width), lambda i: (jnp.minimum(i, p_tiles - 1), 0)),
            pl.BlockSpec((tm, width), lambda i: (jnp.maximum(i - p_tiles, 0), 0))]


def _mix(ga_p, ga_s, yb_p, yb_s, z, wa, wb, wp, layer, tm, gate_col):
    m = z.shape[0]
    k = ga_p.shape[1]
    n = wa.shape[-1]
    gblk = gate_col // n
    p_tiles = ga_p.shape[0] // tm
    return pl.pallas_call(
        functools.partial(_mix_body, p_tiles=p_tiles),
        grid=(m // tm,),
        in_specs=_split_specs(tm, k, p_tiles) + _split_specs(tm, k, p_tiles) + [
                  pl.BlockSpec((tm, n), lambda i: (i, gblk)),
                  pl.BlockSpec((tm, n), lambda i: (i, gblk + 1)),
                  _resident((k, n), layer), _resident((k, n), layer), _resident((k, n), layer)],
        out_specs=pl.BlockSpec((tm, n), lambda i: (i, 0)),
        out_shape=jax.ShapeDtypeStruct((m, n), BF16),
        compiler_params=_cparams(("parallel",)),
        name="mix",
    )(ga_p, ga_s, yb_p, yb_s, z, z, wa, wb, wp)


def _resmm_body(a_ref, w_ref, h_ref, g_ref, o_ref, xn_ref):
    a = a_ref[...]
    for cs in _col_chunks(o_ref.shape[1], MXU_COLS):
        o_ref[:, cs] = h_ref[:, cs] + _dot(a, w_ref[:, cs].astype(BF16))
    xn_ref[...] = _rms_bf16(o_ref[...], g_ref[...])


def _resmm(a, w, hres, g_next, layer, tm):
    m, k = a.shape
    n = w.shape[-1]
    return pl.pallas_call(
        _resmm_body,
        grid=(m // tm,),
        in_specs=[pl.BlockSpec((tm, k), lambda i: (i, 0)),
                  _resident((k, n), layer),
                  pl.BlockSpec((tm, n), lambda i: (i, 0)),
                  pl.BlockSpec((None, 1, n), lambda i: (layer, 0, 0))],
        out_specs=[pl.BlockSpec((tm, n), lambda i: (i, 0)), pl.BlockSpec((tm, n), lambda i: (i, 0))],
        out_shape=[jax.ShapeDtypeStruct((m, n), F32), jax.ShapeDtypeStruct((m, n), BF16)],
        compiler_params=_cparams(("parallel",)),
        name="resmm",
    )(a, w, hres, g_next)


def _ple_update(x_ref, g_ref, pp_ref, ps_ref, wp_ref, wg_ref, dst_ref, p_tiles):
    xn = _rms_bf16(x_ref[...], g_ref[...])
    pb = jnp.where(pl.program_id(0) < p_tiles, pp_ref[...], ps_ref[...]).astype(BF16)
    for cs in _col_chunks(dst_ref.shape[1], MXU_COLS):
        gate = jax.nn.sigmoid(_dot(xn, wg_ref[:, cs].astype(BF16)))
        dst_ref[:, cs] = x_ref[:, cs] + _dot(pb, wp_ref[:, cs].astype(BF16)) * gate


def _ple_body(x_ref, g_ref, pp_ref, ps_ref, wp_ref, wg_ref, gn_ref, o_ref, xn_ref, *, p_tiles):
    _ple_update(x_ref, g_ref, pp_ref, ps_ref, wp_ref, wg_ref, o_ref, p_tiles)
    xn_ref[...] = _rms_bf16(o_ref[...], gn_ref[...])


def _ple_final_body(x_ref, g_ref, pp_ref, ps_ref, wp_ref, wg_ref, gf_ref, op_ref, os_ref, h_ref, *, p_tiles):
    _ple_update(x_ref, g_ref, pp_ref, ps_ref, wp_ref, wg_ref, h_ref, p_tiles)
    h = h_ref[...]
    y = h * lax.rsqrt(jnp.mean(h * h, axis=-1, keepdims=True) + RMS_EPS) * gf_ref[...]
    i = pl.program_id(0)

    @pl.when(i < p_tiles)
    def _():
        op_ref[...] = y

    @pl.when(i >= p_tiles)
    def _():
        os_ref[...] = y


def _ple(x, g, p_p, p_s, wp, wg, layer, tm, g_next=None, g_final=None):
    m, d = x.shape
    mp, pd = p_p.shape[1:]
    p_tiles = mp // tm
    in_specs = [pl.BlockSpec((tm, d), lambda i: (i, 0)),
                pl.BlockSpec((None, 1, d), lambda i: (layer, 0, 0)),
                pl.BlockSpec((None, tm, pd), lambda i: (layer, jnp.minimum(i, p_tiles - 1), 0)),
                pl.BlockSpec((None, tm, pd), lambda i: (layer, jnp.maximum(i - p_tiles, 0), 0)),
                _resident((pd, d), layer), _resident((d, d), layer)]
    if g_final is None:
        row = pl.BlockSpec((tm, d), lambda i: (i, 0))
        return pl.pallas_call(
            functools.partial(_ple_body, p_tiles=p_tiles), grid=(m // tm,),
            in_specs=in_specs + [pl.BlockSpec((None, 1, d), lambda i: (layer + 1, 0, 0))],
            out_specs=[row, row],
            out_shape=[jax.ShapeDtypeStruct((m, d), F32), jax.ShapeDtypeStruct((m, d), BF16)],
            compiler_params=_cparams(("parallel",)),
            name="ple",
        )(x, g, p_p, p_s, wp, wg, g_next)
    return pl.pallas_call(
        functools.partial(_ple_final_body, p_tiles=p_tiles),
        grid=(m // tm,),
        in_specs=in_specs + [pl.BlockSpec((1, d), lambda i: (0, 0))],
        out_specs=[pl.BlockSpec((tm, d), lambda i: (jnp.minimum(i, p_tiles - 1), 0)),
                   pl.BlockSpec((tm, d), lambda i: (jnp.maximum(i - p_tiles, 0), 0))],
        out_shape=[jax.ShapeDtypeStruct((mp, d), F32), jax.ShapeDtypeStruct((m - mp, d), F32)],
        scratch_shapes=[pltpu.VMEM((tm, d), F32)],
        compiler_params=_cparams(("arbitrary",)),
        name="ple_final",
    )(x, g, p_p, p_s, wp, wg, g_final)


def _prep_body(xp_ref, xs_ref, g_ref, h_ref, xn_ref, *, p_tiles):
    x = jnp.where(pl.program_id(0) < p_tiles, xp_ref[...], xs_ref[...])
    h_ref[...] = x
    xn_ref[...] = _rms_bf16(x, g_ref[...])


def _prep(x_p, x_s, g, tm):
    d = x_p.shape[1]
    m = x_p.shape[0] + x_s.shape[0]
    p_tiles = x_p.shape[0] // tm
    row = pl.BlockSpec((tm, d), lambda i: (i, 0))
    return pl.pallas_call(
        functools.partial(_prep_body, p_tiles=p_tiles),
        grid=(m // tm,),
        in_specs=_split_specs(tm, d, p_tiles) + [pl.BlockSpec((None, 1, d), lambda i: (0, 0, 0))],
        out_specs=[row, row],
        out_shape=[jax.ShapeDtypeStruct((m, d), F32), jax.ShapeDtypeStruct((m, d), BF16)],
        compiler_params=_cparams(("parallel",)),
        name="prep",
    )(x_p, x_s, g)


def _pick_tile(n, pref):
    t = min(pref, n)
    while n % t:
        t //= 2
    return t


def kernel(x_prompt, x_sample, state_ssm_re, state_ssm_im, state_pool, p_prompt, p_sample, g_ffn1, w_ffn1_gate, w_ffn1_up, w_ffn1_down, g_mix, w_in, ssm_a_re, ssm_a_im, ssm_log_dt, ssm_b_re, ssm_b_im, ssm_c_re, ssm_c_im, ssm_d, w_glu_a, w_glu_b, w_pool, pool_scale, w_pool_up, w_out, g_ffn2, w_ffn2_gate, w_ffn2_up, w_ffn2_down, g_ple, w_ple, w_ple_gate, g_final):
    nb, seq, d = x_prompt.shape
    ns, dseq, _ = x_sample.shape
    depth, n_groups, p_state = ssm_a_re.shape
    h = ssm_b_re.shape[-1]
    sw = n_groups * h
    pw = pool_scale.shape[-1]
    buf = state_pool.shape[2]
    chunk = SSM_CHUNK
    assert h == SSM_GROUP and dseq * 2 == chunk and seq % chunk == 0 and buf == max(POOL_WINDOWS) - 1
    n_chunks = seq // chunk
    assert n_chunks & (n_chunks - 1) == 0 and n_chunks <= 128
    assert all(w & (w - 1) == 0 for w in POOL_WINDOWS) and sw % pw == 0
    mp, ms = nb * seq, ns * dseq
    m = mp + ms
    tm = _pick_tile(m, 1024)

    g3 = lambda a: a.reshape(depth, 1, -1)
    wg1, wu1, wd1 = w_ffn1_gate, w_ffn1_up, w_ffn1_down
    wg2, wu2, wd2 = w_ffn2_gate, w_ffn2_up, w_ffn2_down
    wga, wgb, wpu, wo = w_glu_a, w_glu_b, w_pool_up, w_out
    wpl, wpg, wpool = w_ple, w_ple_gate, w_pool
    tn_in = _pick_tile(sw + pw, 2048)
    gf1, gmx, gf2, gpl = g3(g_ffn1), g3(g_mix), g3(g_ffn2), g3(g_ple)
    pscale = g3(pool_scale)

    flat = lambda a: a.reshape((depth * n_groups,) + a.shape[2:])
    gpb = min(8, n_groups)
    ops = _ssm_weights(flat(ssm_a_re), flat(ssm_a_im), flat(ssm_log_dt), flat(ssm_b_re), flat(ssm_b_im),
                       flat(ssm_c_re), flat(ssm_c_im), chunk)
    blocked = lambda a: (a.reshape(-1, gpb, a.shape[1], p_state).transpose(0, 2, 1, 3)
                         .reshape(-1, a.shape[1], gpb * p_state))
    ops = ops[:-2] + (blocked(ops[-2]), blocked(ops[-1]))
    d_t = flat(ssm_d).reshape(depth * n_groups, h, 1)

    p_p = p_prompt.reshape(depth, mp, -1)
    p_s = p_sample.transpose(0, 2, 1, 3).reshape(depth, ms, -1)
    tf = _pick_tile(w_ffn1_gate.shape[-1], 512)
    tm2 = _pick_tile(math.gcd(mp, ms), 512)
    hcur, xn = _prep(x_prompt.reshape(mp, d), x_sample.transpose(1, 0, 2).reshape(ms, d), gf1, tm2)
    new_re_p, new_im_p, new_pool_p, new_re_s, new_im_s = [], [], [], [], []
    new_pool_s = jnp.zeros(state_pool.shape, F32)
    for i in range(depth):
        h1 = _ffn(xn, hcur, wg1, wu1, wd1, i, tm, tf)
        z = _inproj(h1, gmx, w_in, i, tm, tn_in, sw + pw)

        ga_p, sr_p, si_p = _ssm(z, 0, mp, d_t, ops, i, n_groups, chunk, n_chunks, gpb=gpb)
        s0 = (state_ssm_re[i].transpose(1, 2, 0), state_ssm_im[i].transpose(1, 2, 0))
        ga_s, sr_s, si_s = _ssm(z, mp // ms, ms, d_t, ops, i, n_groups, dseq, 1, s0=s0, gpb=gpb)
        unblock = lambda a: (a.reshape(-1, nb, gpb, p_state).transpose(1, 0, 2, 3)
                             .reshape(nb, n_groups, p_state))
        new_re_p.append(unblock(sr_p))
        new_im_p.append(unblock(si_p))
        new_re_s.append(sr_s.transpose(2, 0, 1))
        new_im_s.append(si_s.transpose(2, 0, 1))

        yb_p = _pool_seq(z, wpool, pscale, i, nb, seq, pw, sw // pw)
        yb_s, new_pool_s = _pool_step(z, state_pool, wpool, pscale, i, ms, sw // pw, carried=new_pool_s)
        new_pool_p.append(jnp.stack([z[(n + 1) * seq - buf:(n + 1) * seq, sw:sw + pw] for n in range(nb)]))

        merged = _mix(ga_p, ga_s, yb_p, yb_s, z, wga, wgb, wpu, i, tm2, sw + pw)
        h2, xn2 = _resmm(merged, wo, h1, gf2, i, tm2)
        h3 = _ffn(xn2, h2, wg2, wu2, wd2, i, tm, tf)
        if i + 1 < depth:
            hcur, xn = _ple(h3, gpl, p_p, p_s, wpl, wpg, i, tm2, g_next=gf1)
        else:
            y_p, y_s = _ple(h3, gpl, p_p, p_s, wpl, wpg, i, tm2, g_final=g_final.reshape(1, d))

    return (y_p.reshape(nb, seq, d), y_s.reshape(dseq, ns, d).transpose(1, 0, 2),
            jnp.stack(new_re_p), jnp.stack(new_im_p), jnp.stack(new_pool_p),
            jnp.stack(new_re_s), jnp.stack(new_im_s), new_pool_s)
```

```python
import functools
import math

import numpy as np
import jax
import jax.numpy as jnp
from jax import lax
from jax.experimental import pallas as pl
from jax.experimental.pallas import tpu as pltpu

F32 = jnp.float32
BF16 = jnp.bfloat16
RMS_EPS = 1e-6
POOL_WINDOWS = (2, 4, 8, 16)
SSM_GROUP = 16
SSM_CHUNK = 16
GELU_C = math.sqrt(2.0 / math.pi)
VMEM_LIMIT = 62 * 1024 * 1024
MXU_COLS = 256


def _cparams(sem):
    return pltpu.CompilerParams(dimension_semantics=sem, vmem_limit_bytes=VMEM_LIMIT)


def _rms_bf16(x, g):
    inv = lax.rsqrt(jnp.mean(x * x, axis=-1, keepdims=True) + RMS_EPS)
    return (x * inv * g).astype(BF16)


def _dot(a, b):
    return jnp.dot(a, b, preferred_element_type=F32)


def _col_chunks(width, chunk):
    chunk = min(chunk, width)
    return [slice(c0, c0 + chunk) for c0 in range(0, width, chunk)]


def _ffn_body(xn_ref, h_ref, wg_ref, wu_ref, wd_ref, *rest, res_chunks):
    gn_ref, o_ref, xo_ref = rest if len(rest) == 3 else (None,) + rest + (None,)
    j = pl.program_id(1)
    nj = pl.num_programs(1)

    @pl.when(j == 0)
    def _():
        o_ref[...] = jnp.zeros_like(o_ref)

    xn = xn_ref[...]
    mids = []
    for cs in _col_chunks(wg_ref.shape[1], MXU_COLS):
        a = _dot(xn, wg_ref[:, cs].astype(BF16))
        b = _dot(xn, wu_ref[:, cs].astype(BF16))
        mids.append((0.5 * a * jax.nn.sigmoid(a) * b).astype(BF16))
    mid = jnp.concatenate(mids, axis=1)
    chunks = _col_chunks(o_ref.shape[1], 2 * MXU_COLS)
    last = chunks.pop()
    chunks += [slice(c0, c0 + MXU_COLS) for c0 in range(last.start, last.stop, MXU_COLS)]
    for cs in chunks:
        o_ref[:, cs] += _dot(mid, wd_ref[:, cs].astype(BF16))

    cw = o_ref.shape[1] // res_chunks
    for c in range(res_chunks):
        @pl.when(j == nj - res_chunks + c)
        def _(c=c):
            o_ref[:, c * cw:(c + 1) * cw] += h_ref[...]

    if xo_ref is not None:
        @pl.when(j == nj - 1)
        def _():
            xo_ref[...] = _rms_bf16(o_ref[...], gn_ref[...])


def _ffn(xn, hres, wg, wu, wd, layer, tm, tf, g_next=None):
    m, d = xn.shape
    f = wg.shape[-1]
    nj = f // tf
    res_chunks = min(4, nj)
    row = lambda: pl.BlockSpec((tm, d), lambda i, j: (i, 0))
    in_specs = [
        pl.BlockSpec((tm, d), lambda i, j: (i, 0)),
        pl.BlockSpec((tm, d // res_chunks),
                     lambda i, j: (i, jnp.clip(j - (nj - res_chunks), 0, res_chunks - 1))),
        pl.BlockSpec((None, d, tf), lambda i, j: (layer, 0, j)),
        pl.BlockSpec((None, d, tf), lambda i, j: (layer, 0, j)),
        pl.BlockSpec((None, tf, d), lambda i, j: (layer, j, 0)),
    ]
    args = [xn, hres, wg, wu, wd]
    out_specs, out_shape = row(), jax.ShapeDtypeStruct((m, d), F32)
    if g_next is not None:
        in_specs.append(pl.BlockSpec((None, 1, d), lambda i, j: (layer, 0, 0)))
        args.append(g_next)
        out_specs, out_shape = [row(), row()], [out_shape, jax.ShapeDtypeStruct((m, d), BF16)]
    return pl.pallas_call(
        functools.partial(_ffn_body, res_chunks=res_chunks),
        grid=(m // tm, nj),
        in_specs=in_specs,
        out_specs=out_specs,
        out_shape=out_shape,
        compiler_params=_cparams(("parallel", "arbitrary")),
        name="ffn",
    )(*args)


def _inproj_body(x_ref, g_ref, w_ref, o_ref, *, gate):
    xn = _rms_bf16(x_ref[...], g_ref[...])
    for cs in _col_chunks(o_ref.shape[1], MXU_COLS):
        r = _dot(xn, w_ref[:, cs].astype(BF16))
        o_ref[:, cs] = (jax.nn.sigmoid(r) if gate else r).astype(o_ref.dtype)


def _inproj(x, g, w, layer, tm, tn, col0, ncols, gate):
    m, d = x.shape
    off = col0 // tn
    return pl.pallas_call(
        functools.partial(_inproj_body, gate=gate),
        grid=(ncols // tn, m // tm),
        in_specs=[
            pl.BlockSpec((tm, d), lambda j, i: (i, 0)),
            pl.BlockSpec((None, 1, d), lambda j, i: (layer, 0, 0)),
            pl.BlockSpec((None, d, tn), lambda j, i: (layer, 0, off + j), pipeline_mode=pl.Buffered(1)),
        ],
        out_specs=pl.BlockSpec((tm, tn), lambda j, i: (i, j)),
        out_shape=jax.ShapeDtypeStruct((m, ncols), BF16 if gate else F32),
        compiler_params=_cparams(("arbitrary", "arbitrary")),
        name="inproj_gates" if gate else "inproj",
    )(x, g, w)


def _cmul(ar, ai, br, bi):
    return ar * br - ai * bi, ar * bi + ai * br


def _dot_nt3(a, b):
    nt = (((1,), (1,)), ((), ()))
    dot = lambda x, y: lax.dot_general(x, y, nt, preferred_element_type=F32)
    ah, bh = a.astype(BF16), b.astype(BF16)
    al = (a - ah.astype(F32)).astype(BF16)
    bl = (b - bh.astype(F32)).astype(BF16)
    return dot(ah, bh) + dot(ah, bl) + dot(al, bh)


def _ssm_weights_body(ar_ref, ai_ref, ldt_ref, btr_ref, bti_ref, ctr_ref, cti_ref, e_col, e_row,
                      tt_o, wor_o, woi_o, wstr_o, wsti_o, scr_o, sci_o, *, chunk, gblk):
    h = SSM_GROUP
    causal = e_col[...] >= e_row[...]

    def rows(pows):
        width = pows[0][0].shape[1]
        return tuple(jnp.concatenate([jnp.broadcast_to(x[k], (h, width)) for x in pows], axis=0)
                     for k in (0, 1))

    for gl in range(gblk):
        dt = jnp.exp(ldt_ref[gl])
        a_re, a_im = ar_ref[gl], ai_ref[gl]
        mag = jnp.exp(a_re * dt)
        ang = a_im * dt
        lr, li = mag * jnp.cos(ang), mag * jnp.sin(ang)
        den = a_re * a_re + a_im * a_im
        num_re = lr - 1.0
        k_re = (num_re * a_re + li * a_im) / den
        k_im = (li * a_re - num_re * a_im) / den
        inv = 1.0 / (lr * lr + li * li)
        nr, ni = lr * inv, -li * inv
        pw = [(jnp.ones_like(lr), jnp.zeros_like(lr))]
        npw = list(pw)
        for _ in range(chunk):
            pw.append(_cmul(*pw[-1], lr, li))
            npw.append(_cmul(*npw[-1], nr, ni))
        tile = lambda x: jnp.concatenate([x] * chunk, axis=0)
        kb = _cmul(k_re, k_im, tile(btr_ref[gl]), tile(bti_ref[gl]))
        c = (tile(ctr_ref[gl]), tile(cti_ref[gl]))

        l_re, l_im = _cmul(*c, *rows(pw[:chunk]))
        r_re, r_im = _cmul(*rows(npw[:chunk]), *kb)
        kmat = _dot_nt3(l_re, r_re) - _dot_nt3(l_im, r_im)
        tt_o[gl] = jnp.where(causal, kmat, 0.0).astype(BF16)

        e_re, e_im = _cmul(*c, *rows(pw[1:chunk + 1]))
        wor_o[gl] = e_re.astype(BF16)
        woi_o[gl] = (-e_im).astype(BF16)

        s_re, s_im = _cmul(*rows(pw[chunk - 1::-1]), *kb)
        wstr_o[gl] = s_re.astype(BF16)
        wsti_o[gl] = s_im.astype(BF16)

        sc = [pw[chunk]]
        for _ in range(6):
            sc.append(_cmul(*sc[-1], *sc[-1]))
        sc.append(pw[chunk // 2])
        scr_o[gl] = jnp.concatenate([x[0] for x in sc], axis=0)
        sci_o[gl] = jnp.concatenate([x[1] for x in sc], axis=0)


def _ssm_weights(a_re, a_im, log_dt, b_re, b_im, c_re, c_im, chunk):
    dg, p = a_re.shape
    h = b_re.shape[-1]
    th = chunk * h
    tau = np.repeat(np.arange(chunk, dtype=np.float32), h)
    e_col = jnp.asarray(tau.reshape(th, 1))
    e_row = jnp.asarray(tau.reshape(1, th))
    gblk = min(16, dg)
    row = lambda x: x.reshape(dg, 1, p)
    bt = lambda x: jnp.swapaxes(x, 1, 2)
    per_g = lambda *s: pl.BlockSpec((gblk,) + s, lambda g: (g,) + (0,) * len(s))
    const = lambda *s: pl.BlockSpec(s, lambda g: (0,) * len(s))
    tt, wor, woi, wstr, wsti, scr, sci = pl.pallas_call(
        functools.partial(_ssm_weights_body, chunk=chunk, gblk=gblk),
        grid=(dg // gblk,),
        in_specs=[per_g(1, p), per_g(1, p), per_g(1, 1),
                  per_g(h, p), per_g(h, p), per_g(h, p), per_g(h, p),
                  const(th, 1), const(1, th)],
        out_specs=[per_g(th, th), per_g(th, p), per_g(th, p), per_g(th, p), per_g(th, p),
                   per_g(8, p), per_g(8, p)],
        out_shape=[jax.ShapeDtypeStruct((dg, th, th), BF16),
                   jax.ShapeDtypeStruct((dg, th, p), BF16),
                   jax.ShapeDtypeStruct((dg, th, p), BF16),
                   jax.ShapeDtypeStruct((dg, th, p), BF16),
                   jax.ShapeDtypeStruct((dg, th, p), BF16),
                   jax.ShapeDtypeStruct((dg, 8, p), F32),
                   jax.ShapeDtypeStruct((dg, 8, p), F32)],
        compiler_params=_cparams(("parallel",)),
        name="ssm_weights",
    )(row(a_re), row(a_im), log_dt.reshape(dg, 1, 1),
      bt(b_re), bt(b_im), c_re, c_im, e_col, e_row)
    sw = lambda x: jnp.swapaxes(x, 1, 2)
    return tt, wor, woi, sw(wstr), sw(wsti), sw(scr), sw(sci), scr, sci


def _ssm_body(*refs, steps, chunks, gpb, has_init):
    if has_init:
        (u_ref, d_ref, tt_ref, wor_ref, woi_ref, wsr_ref, wsi_ref, scr_ref, sci_ref,
         s0r_ref, s0i_ref, y_ref, sfr_ref, sfi_ref, ys_ref, xr_ref, xi_ref) = refs
    else:
        (u_ref, d_ref, tt_ref, wor_ref, woi_ref, wsr_ref, wsi_ref, scr_ref, sci_ref,
         y_ref, sfr_ref, sfi_ref, ys_ref, xr_ref, xi_ref, st_r, st_i) = refs
    h = SSM_GROUP
    th = steps * h
    w = u_ref.shape[0] // steps
    p = wsr_ref.shape[1]
    ws_off = wsr_ref.shape[2] - th

    def step_rows(t):
        return pl.ds(t, w, stride=steps) if chunks > 1 else pl.ds(t * w, w)

    slabs = [u_ref[step_rows(t), :].T for t in range(steps)]

    for gl in range(gpb):
        rows = slice(gl * h, (gl + 1) * h)
        u = jnp.concatenate([s[rows, :] for s in slabs], axis=0)
        ub = u.astype(BF16)
        y = _dot(tt_ref[gl, :th, :th], ub) + jnp.concatenate([d_ref[gl]] * steps, axis=0) * u
        xr_ref[gl * p:(gl + 1) * p, :] = _dot(wsr_ref[gl, :, ws_off:], ub)
        xi_ref[gl * p:(gl + 1) * p, :] = _dot(wsi_ref[gl, :, ws_off:], ub)
        for t in range(steps):
            ys_ref[t, rows, :] = y[t * h:(t + 1) * h, :]

    gp = gpb * p
    col = lambda ref, k: ref[:, :, k:k + 1].reshape(gp, 1)
    xr, xi = xr_ref[...], xi_ref[...]
    if chunks > 1:
        pad = st_r.shape[0] - w
        c_row = lax.broadcasted_iota(jnp.int32, (w, 1), 0) & (chunks - 1)
        st_r[:pad, :] = jnp.zeros((pad, gp), F32)
        st_i[:pad, :] = jnp.zeros((pad, gp), F32)
        st_r[pad:, :] = xr.T
        st_i[pad:, :] = xi.T
        k = 0
        while (1 << k) < chunks:
            sh = 1 << k
            rr, ri = st_r[pl.ds(pad - sh, w), :], st_i[pl.ds(pad - sh, w), :]
            mr, mi = scr_ref[k:k + 1, :], sci_ref[k:k + 1, :]
            keep = c_row >= sh
            st_r[pad:, :] += jnp.where(keep, mr * rr - mi * ri, 0.0)
            st_i[pad:, :] += jnp.where(keep, mr * ri + mi * rr, 0.0)
            k += 1
        first = c_row >= 1
        xr_ref[...] = jnp.where(first, st_r[pl.ds(pad - 1, w), :], 0.0).T
        xi_ref[...] = jnp.where(first, st_i[pl.ds(pad - 1, w), :], 0.0).T
        for n in range(w // chunks):
            last = pad + n * chunks + chunks - 1
            sfr_ref[n:n + 1, :] = st_r[last:last + 1, :]
            sfi_ref[n:n + 1, :] = st_i[last:last + 1, :]
    else:
        pr, pi = s0r_ref[...].reshape(gp, w), s0i_ref[...].reshape(gp, w)
        lr, li = col(scr_ref, 7), col(sci_ref, 7)
        sfr_ref[...] = (lr * pr - li * pi + xr).reshape(gpb, p, w)
        sfi_ref[...] = (lr * pi + li * pr + xi).reshape(gpb, p, w)
        xr_ref[...] = pr
        xi_ref[...] = pi

    for gl in range(gpb):
        rows = slice(gl * h, (gl + 1) * h)
        pr = xr_ref[gl * p:(gl + 1) * p, :].astype(BF16)
        pi = xi_ref[gl * p:(gl + 1) * p, :].astype(BF16)
        ya = _dot(wor_ref[gl, :th, :], pr) + _dot(woi_ref[gl, :th, :], pi)
        for t in range(steps):
            y = ys_ref[t, rows, :] + ya[t * h:(t + 1) * h, :]
            ys_ref[t, rows, :] = 0.5 * y * (1.0 + jnp.tanh(GELU_C * (y + 0.044715 * (y * y * y))))
    for t in range(steps):
        y_ref[step_rows(t), :] = ys_ref[t].T


def _ssm(z, row_blk, rows, d_t, ops, layer, n_groups, steps, chunks, s0=None, gpb=8):
    tt, wor, woi, wsr, wsi, scr, sci, sc_rows_r, sc_rows_i = ops
    h = SSM_GROUP
    w = rows // steps
    nseq = w // chunks
    p = scr.shape[1]
    gp = gpb * p
    thf = tt.shape[1]
    nblk = n_groups // gpb
    base = layer * nblk
    wblk = lambda *s: pl.BlockSpec((gpb,) + s, lambda g: (base + g,) + (0,) * len(s))
    scratch = [pltpu.VMEM((steps, gpb * h, w), F32), pltpu.VMEM((gp, w), F32), pltpu.VMEM((gp, w), F32)]
    if s0 is None:
        sc_spec = pl.BlockSpec((None, sc_rows_r.shape[1], gp), lambda g: (base + g, 0, 0))
        sc_args = [sc_rows_r, sc_rows_i]
        sf_spec = pl.BlockSpec((None, nseq, gp), lambda g: (g, 0, 0))
        sf_shape = jax.ShapeDtypeStruct((nblk, nseq, gp), F32)
        scratch += [pltpu.VMEM((chunks + w, gp), F32)] * 2
    else:
        sc_spec = wblk(p, scr.shape[2])
        sc_args = [scr, sci]
        sf_spec = pl.BlockSpec((gpb, p, nseq), lambda g: (g, 0, 0))
        sf_shape = jax.ShapeDtypeStruct((n_groups, p, nseq), F32)
    in_specs = [pl.BlockSpec((rows, gpb * h), lambda g: (row_blk, g)),
                wblk(h, 1), wblk(thf, thf), wblk(thf, p), wblk(thf, p), wblk(p, thf), wblk(p, thf),
                sc_spec, sc_spec]
    args = [z, d_t, tt, wor, woi, wsr, wsi] + sc_args
    if s0 is not None:
        in_specs += [pl.BlockSpec((gpb, p, w), lambda g: (g, 0, 0))] * 2
        args += list(s0)
    return pl.pallas_call(
        functools.partial(_ssm_body, steps=steps, chunks=chunks, gpb=gpb, has_init=s0 is not None),
        grid=(nblk,),
        in_specs=in_specs,
        out_specs=[pl.BlockSpec((rows, gpb * h), lambda g: (0, g)), sf_spec, sf_spec],
        out_shape=[jax.ShapeDtypeStruct((rows, n_groups * h), F32), sf_shape, sf_shape],
        scratch_shapes=scratch,
        compiler_params=_cparams(("parallel",)),
        name="ssm_chunks" if s0 is None else "ssm_step",
    )(*args)


def _pool_seq_body(u_ref, w_ref, sc_ref, o_ref, z_ref):
    l, c = u_ref.shape
    pad = z_ref.shape[0] - l
    cg = c // len(POOL_WINDOWS)
    z_ref[:pad, :] = jnp.zeros((pad, c), F32)
    pos1 = (lax.broadcasted_iota(jnp.int32, (l, 1), 0) + 1).astype(F32)
    for gi, win in enumerate(POOL_WINDOWS):
        cols = slice(gi * cg, (gi + 1) * cg)
        cur = u_ref[:, cols]
        tot = cur
        sh = 1
        while sh < win:
            z_ref[pad:, cols] = tot
            tot = tot + z_ref[pad - sh:pad - sh + l, cols]
            sh *= 2
        inv_cnt = 1.0 / jnp.minimum(pos1, float(win))
        mixed = _dot((tot * inv_cnt - cur).astype(BF16), w_ref[gi].astype(BF16))
        o_ref[:, cols] = (mixed * sc_ref[:, cols]).astype(BF16)


def _pool_seq(z, w_pool, scale, layer, nseq, seqlen, width, col_blk):
    return pl.pallas_call(
        _pool_seq_body,
        grid=(nseq,),
        in_specs=[pl.BlockSpec((seqlen, width), lambda n: (n, col_blk)),
                  pl.BlockSpec((None,) + w_pool.shape[1:], lambda n: (layer, 0, 0, 0)),
                  pl.BlockSpec((None, 1, width), lambda n: (layer, 0, 0))],
        out_specs=pl.BlockSpec((seqlen, width), lambda n: (n, 0)),
        out_shape=jax.ShapeDtypeStruct((nseq * seqlen, width), BF16),
        scratch_shapes=[pltpu.VMEM((seqlen + 16, width), F32)],
        compiler_params=_cparams(("parallel",)),
        name="pool_seq",
    )(z, w_pool, scale)


def _pool_step_body(u_ref, prev_ref, w_ref, sc_ref, *rest):
    o_ref, nxt_ref = rest[-2:]
    n, buf, c = prev_ref.shape
    steps = u_ref.shape[0] // n
    cg = c // len(POOL_WINDOWS)

    def row(j, cols):
        return prev_ref[:, j, cols] if j < buf else u_ref[(j - buf) * n:(j - buf + 1) * n, cols]

    for gi, win in enumerate(POOL_WINDOWS):
        cols = slice(gi * cg, (gi + 1) * cg)
        for t in range(steps):
            cur = row(buf + t, cols)
            tot = cur
            for k in range(1, win):
                tot = tot + row(buf + t - k, cols)
            mixed = _dot((tot * (1.0 / win) - cur).astype(BF16), w_ref[gi].astype(BF16))
            o_ref[t * n:(t + 1) * n, cols] = (mixed * sc_ref[:, cols]).astype(BF16)
    for j in range(buf):
        nxt_ref[:, j, :] = row(j + steps, slice(None))


def _pool_step(z, prev, w_pool, scale, layer, rows, col_blk, carried=None):
    n, buf, c = prev.shape[1:]
    row_blk = z.shape[0] // rows - 1
    in_specs = [pl.BlockSpec((rows, c), lambda i: (row_blk, col_blk)),
                pl.BlockSpec((None, n, buf, c), lambda i: (layer, 0, 0, 0)),
                pl.BlockSpec((None,) + w_pool.shape[1:], lambda i: (layer, 0, 0, 0)),
                pl.BlockSpec((None, 1, c), lambda i: (layer, 0, 0))]
    args = [z, prev, w_pool, scale]
    aliases = {}
    if carried is not None:
        in_specs.append(pl.BlockSpec(memory_space=pl.ANY))
        args.append(carried)
        aliases = {len(args) - 1: 1}
    return pl.pallas_call(
        _pool_step_body,
        grid=(1,),
        in_specs=in_specs,
        out_specs=[pl.BlockSpec((rows, c), lambda i: (0, 0)),
                   pl.BlockSpec((None, n, buf, c), lambda i: (layer, 0, 0, 0))],
        out_shape=[jax.ShapeDtypeStruct((rows, c), BF16), jax.ShapeDtypeStruct(prev.shape, F32)],
        input_output_aliases=aliases,
        compiler_params=_cparams(("arbitrary",)),
        name="pool_step",
    )(*args)


def _mix_body(gap_ref, gas_ref, ybp_ref, ybs_ref, sa_ref, sb_ref, wa_ref, wb_ref, wp_ref, o_ref, *, p_tiles):
    first = pl.program_id(0) < p_tiles
    ga = jnp.where(first, gap_ref[...], gas_ref[...]).astype(BF16)
    yb =jnp.where(first, ybp_ref[...], ybs_ref[...])
    for cs in _col_chunks(o_ref.shape[1], MXU_COLS):
        br_a = _dot(ga, wa_ref[:, cs].astype(BF16)) * jax.nn.sigmoid(_dot(ga, wb_ref[:, cs].astype(BF16)))
        br_b = _dot(yb, wp_ref[:, cs].astype(BF16))
        o_ref[:, cs] = (sa_ref[:, cs] * br_a + sb_ref[:, cs] * br_b).astype(BF16)


def _resident(shape, layer):
    return pl.BlockSpec((None,) + shape, lambda i: (layer,) + (0,) * len(shape), pipeline_mode=pl.Buffered(1))


def _split_specs(tm, width, p_tiles):
    return [pl.BlockSpec((tm, width), lambda i: (jnp.minimum(i, p_tiles - 1), 0)),
            pl.BlockSpec((tm, width), lambda i: (jnp.maximum(i - p_tiles, 0), 0))]


def _mix(ga_p, ga_s, yb_p, yb_s, z, wa, wb, wp, layer, tm, gate_col):
    m = z.shape[0]
    k = ga_p.shape[1]
    n = wa.shape[-1]
    gblk = gate_col // n
    p_tiles = ga_p.shape[0] // tm
    return pl.pallas_call(
        functools.partial(_mix_body, p_tiles=p_tiles),
        grid=(m // tm,),
        in_specs=_split_specs(tm, k, p_tiles) + _split_specs(tm, k, p_tiles) + [
                  pl.BlockSpec((tm, n), lambda i: (i, gblk)),
                  pl.BlockSpec((tm, n), lambda i: (i, gblk + 1)),
                  _resident((k, n), layer), _resident((k, n), layer), _resident((k, n), layer)],
        out_specs=pl.BlockSpec((tm, n), lambda i: (i, 0)),
        out_shape=jax.ShapeDtypeStruct((m, n), BF16),
        compiler_params=_cparams(("parallel",)),
        name="mix",
    )(ga_p, ga_s, yb_p, yb_s, z, z, wa, wb, wp)


def _resmm_body(a_ref, w_ref, h_ref, g_ref, o_ref, xn_ref):
    a = a_ref[...]
    for cs in _col_chunks(o_ref.shape[1], MXU_COLS):
        o_ref[:, cs] = h_ref[:, cs] + _dot(a, w_ref[:, cs].astype(BF16))
    xn_ref[...] = _rms_bf16(o_ref[...], g_ref[...])


def _resmm(a, w, hres, g_next, layer, tm):
    m, k = a.shape
    n = w.shape[-1]
    return pl.pallas_call(
        _resmm_body,
        grid=(m // tm,),
        in_specs=[pl.BlockSpec((tm, k), lambda i: (i, 0)),
                  _resident((k, n), layer),
                  pl.BlockSpec((tm, n), lambda i: (i, 0)),
                  pl.BlockSpec((None, 1, n), lambda i: (layer, 0, 0))],
        out_specs=[pl.BlockSpec((tm, n), lambda i: (i, 0)), pl.BlockSpec((tm, n), lambda i: (i, 0))],
        out_shape=[jax.ShapeDtypeStruct((m, n), F32), jax.ShapeDtypeStruct((m, n), BF16)],
        compiler_params=_cparams(("parallel",)),
        name="resmm",
    )(a, w, hres, g_next)


def _ple_update(x_ref, g_ref, pp_ref, ps_ref, wp_ref, wg_ref, dst_ref, p_tiles):
    xn = _rms_bf16(x_ref[...], g_ref[...])
    pb = jnp.where(pl.program_id(0) < p_tiles, pp_ref[...], ps_ref[...]).astype(BF16)
    for cs in _col_chunks(dst_ref.shape[1], MXU_COLS):
        gate = jax.nn.sigmoid(_dot(xn, wg_ref[:, cs].astype(BF16)))
        dst_ref[:, cs] = x_ref[:, cs] + _dot(pb, wp_ref[:, cs].astype(BF16)) * gate


def _ple_body(x_ref, g_ref, pp_ref, ps_ref, wp_ref, wg_ref, gn_ref, o_ref, xn_ref, *, p_tiles):
    _ple_update(x_ref, g_ref, pp_ref, ps_ref, wp_ref, wg_ref, o_ref, p_tiles)
    xn_ref[...] = _rms_bf16(o_ref[...], gn_ref[...])


def _ple_final_body(x_ref, g_ref, pp_ref, ps_ref, wp_ref, wg_ref, gf_ref, op_ref, os_ref, h_ref, *, p_tiles):
    _ple_update(x_ref, g_ref, pp_ref, ps_ref, wp_ref, wg_ref, h_ref, p_tiles)
    h = h_ref[...]
    y = h * lax.rsqrt(jnp.mean(h * h, axis=-1, keepdims=True) + RMS_EPS) * gf_ref[...]
    i = pl.program_id(0)

    @pl.when(i < p_tiles)
    def _():
        op_ref[...] = y

    @pl.when(i >= p_tiles)
    def _():
        os_ref[...] = y


def _ple(x, g, p_p, p_s, wp, wg, layer, tm, g_next=None, g_final=None):
    m, d = x.shape
    mp, pd = p_p.shape[1:]
    p_tiles = mp // tm
    in_specs = [pl.BlockSpec((tm, d), lambda i: (i, 0)),
                pl.BlockSpec((None, 1, d), lambda i: (layer, 0, 0)),
                pl.BlockSpec((None, tm, pd), lambda i: (layer, jnp.minimum(i, p_tiles - 1), 0)),
                pl.BlockSpec((None, tm, pd), lambda i: (layer, jnp.maximum(i - p_tiles, 0), 0)),
                _resident((pd, d), layer), _resident((d, d), layer)]
    if g_final is None:
        row = pl.BlockSpec((tm, d), lambda i: (i, 0))
        return pl.pallas_call(
            functools.partial(_ple_body, p_tiles=p_tiles), grid=(m // tm,),
            in_specs=in_specs + [pl.BlockSpec((None, 1, d), lambda i: (layer + 1, 0, 0))],
            out_specs=[row, row],
            out_shape=[jax.ShapeDtypeStruct((m, d), F32), jax.ShapeDtypeStruct((m, d), BF16)],
            compiler_params=_cparams(("parallel",)),
            name="ple",
        )(x, g, p_p, p_s, wp, wg, g_next)
    return pl.pallas_call(
        functools.partial(_ple_final_body, p_tiles=p_tiles),
        grid=(m // tm,),
        in_specs=in_specs + [pl.BlockSpec((1, d), lambda i: (0, 0))],
        out_specs=[pl.BlockSpec((tm, d), lambda i: (jnp.minimum(i, p_tiles - 1), 0)),
                   pl.BlockSpec((tm, d), lambda i: (jnp.maximum(i - p_tiles, 0), 0))],
        out_shape=[jax.ShapeDtypeStruct((mp, d), F32), jax.ShapeDtypeStruct((m - mp, d), F32)],
        scratch_shapes=[pltpu.VMEM((tm, d), F32)],
        compiler_params=_cparams(("arbitrary",)),
        name="ple_final",
    )(x, g, p_p, p_s, wp, wg, g_final)


def _prep_body(xp_ref, xs_ref, g_ref, h_ref, xn_ref, *, p_tiles):
    x = jnp.where(pl.program_id(0) < p_tiles, xp_ref[...], xs_ref[...])
    h_ref[...] = x
    xn_ref[...] = _rms_bf16(x, g_ref[...])


def _prep(x_p, x_s, g, tm):
    d = x_p.shape[1]
    m = x_p.shape[0] + x_s.shape[0]
    p_tiles = x_p.shape[0] // tm
    row = pl.BlockSpec((tm, d), lambda i: (i, 0))
    return pl.pallas_call(
        functools.partial(_prep_body, p_tiles=p_tiles),
        grid=(m // tm,),
        in_specs=_split_specs(tm, d, p_tiles) + [pl.BlockSpec((None, 1, d), lambda i: (0, 0, 0))],
        out_specs=[row, row],
        out_shape=[jax.ShapeDtypeStruct((m, d), F32), jax.ShapeDtypeStruct((m, d), BF16)],
        compiler_params=_cparams(("parallel",)),
        name="prep",
    )(x_p, x_s, g)


def _pick_tile(n, pref):
    t = min(pref, n)
    while n % t:
        t //= 2
    return t


def kernel(x_prompt, x_sample, state_ssm_re, state_ssm_im, state_pool, p_prompt, p_sample, g_ffn1, w_ffn1_gate, w_ffn1_up, w_ffn1_down, g_mix, w_in, ssm_a_re, ssm_a_im, ssm_log_dt, ssm_b_re, ssm_b_im, ssm_c_re, ssm_c_im, ssm_d, w_glu_a, w_glu_b, w_pool, pool_scale, w_pool_up, w_out, g_ffn2, w_ffn2_gate, w_ffn2_up, w_ffn2_down, g_ple, w_ple, w_ple_gate, g_final):
    nb, seq, d = x_prompt.shape
    ns, dseq, _ = x_sample.shape
    depth, n_groups, p_state = ssm_a_re.shape
    h = ssm_b_re.shape[-1]
    sw = n_groups * h
    pw = pool_scale.shape[-1]
    buf = state_pool.shape[2]
    chunk = SSM_CHUNK
    assert h == SSM_GROUP and dseq * 2 == chunk and seq % chunk == 0 and buf == max(POOL_WINDOWS) - 1
    n_chunks = seq // chunk
    assert n_chunks & (n_chunks - 1) == 0 and n_chunks <= 128
    assert all(w & (w - 1) == 0 for w in POOL_WINDOWS) and sw % pw == 0
    mp, ms = nb * seq, ns * dseq
    m = mp + ms
    tm = _pick_tile(m, 1024)

    g3 = lambda a: a.reshape(depth, 1, -1)
    wg1, wu1, wd1 = w_ffn1_gate, w_ffn1_up, w_ffn1_down
    wg2, wu2, wd2 = w_ffn2_gate, w_ffn2_up, w_ffn2_down
    wga, wgb, wpu, wo = w_glu_a, w_glu_b, w_pool_up, w_out
    wpl, wpg, wpool = w_ple, w_ple_gate, w_pool
    tn_in = _pick_tile(sw + pw, 2048)
    gf1, gmx, gf2, gpl = g3(g_ffn1), g3(g_mix), g3(g_ffn2), g3(g_ple)
    pscale = g3(pool_scale)

    flat = lambda a: a.reshape((depth * n_groups,) + a.shape[2:])
    gpb = min(8, n_groups)
    ops = _ssm_weights(flat(ssm_a_re), flat(ssm_a_im), flat(ssm_log_dt), flat(ssm_b_re), flat(ssm_b_im),
                       flat(ssm_c_re), flat(ssm_c_im), chunk)
    blocked = lambda a: (a.reshape(-1, gpb, a.shape[1], p_state).transpose(0, 2, 1, 3)
                         .reshape(-1, a.shape[1], gpb * p_state))
    ops = ops[:-2] + (blocked(ops[-2]), blocked(ops[-1]))
    d_t = flat(ssm_d).reshape(depth * n_groups, h, 1)

    p_p = p_prompt.reshape(depth, mp, -1)
    p_s = p_sample.transpose(0, 2, 1, 3).reshape(depth, ms, -1)
    tf = _pick_tile(w_ffn1_gate.shape[-1], 512)
    tm2 = _pick_tile(math.gcd(mp, ms), 512)
    hcur, xn = _prep(x_prompt.reshape(mp, d), x_sample.transpose(1, 0, 2).reshape(ms, d), gf1, tm2)
    new_re_p, new_im_p, new_pool_p, new_re_s, new_im_s = [], [], [], [], []
    new_pool_s = jnp.zeros(state_pool.shape, F32)
    for i in range(depth):
        h1 = _ffn(xn, hcur, wg1, wu1, wd1, i, tm, tf)
        z = _inproj(h1, gmx, w_in, i, tm, tn_in, 0, sw + pw, False)
        gates = _inproj(h1, gmx, w_in, i, tm, tn_in, sw + pw, 2 * d, True)

        ga_p, sr_p, si_p = _ssm(z, 0, mp, d_t, ops, i, n_groups, chunk, n_chunks, gpb=gpb)
        s0 = (state_ssm_re[i].transpose(1, 2, 0), state_ssm_im[i].transpose(1, 2, 0))
        ga_s, sr_s, si_s = _ssm(z, mp // ms, ms, d_t, ops, i, n_groups, dseq, 1, s0=s0, gpb=gpb)
        unblock = lambda a: (a.reshape(-1, nb, gpb, p_state).transpose(1, 0, 2, 3)
                             .reshape(nb, n_groups, p_state))
        new_re_p.append(unblock(sr_p))
        new_im_p.append(unblock(si_p))
        new_re_s.append(sr_s.transpose(2, 0, 1))
        new_im_s.append(si_s.transpose(2, 0, 1))

        yb_p = _pool_seq(z, wpool, pscale, i, nb, seq, pw, sw // pw)
        yb_s, new_pool_s = _pool_step(z, state_pool, wpool, pscale, i, ms, sw // pw, carried=new_pool_s)
        new_pool_p.append(jnp.stack([z[(n + 1) * seq - buf:(n + 1) * seq, sw:sw + pw] for n in range(nb)]))

        merged = _mix(ga_p, ga_s, yb_p, yb_s, gates, wga, wgb, wpu, i, tm2, 0)
        h2, xn2 = _resmm(merged, wo, h1, gf2, i, tm2)
        h3 = _ffn(xn2, h2, wg2, wu2, wd2, i, tm, tf)
        if i + 1 < depth:
            hcur, xn = _ple(h3, gpl, p_p, p_s, wpl, wpg, i, tm2, g_next=gf1)
        else:
            y_p, y_s = _ple(h3, gpl, p_p, p_s, wpl, wpg, i, tm2, g_final=g_final.reshape(1, d))

    return (y_p.reshape(nb, seq, d), y_s.reshape(dseq, ns, d).transpose(1, 0, 2),
            jnp.stack(new_re_p), jnp.stack(new_im_p), jnp.stack(new_pool_p),
            jnp.stack(new_re_s), jnp.stack(new_im_s), new_pool_s)
```
